```python
import jax, jax.numpy as jnp
from jax import lax
import numpy as np

D_MODEL = 2048
BATCH = 2
SEQ = 8192
DEPTH = 2

GRID_W = 64
CTX_LEN = 256
N_MOD = 6
SGU_WIDTH = 1024
SGU_CHUNK = 128
SGU_GROUPS = 8
SGU_GROUP_CH = SGU_WIDTH // SGU_GROUPS
NA_HEADS = 8
HEAD_DIM = 128
NA_WIDTH = NA_HEADS * HEAD_DIM
WIN_ROWS = 8
WIN_COLS = 16
ROPE_THETA = 10000.0
ROPE_PAIRS = HEAD_DIM // 4
IN_PROJ_WIDTH = 2 * SGU_WIDTH + 3 * NA_WIDTH
N_GROUPS = 4
EXPERTS_PER_GROUP = 8
N_EXPERTS = N_GROUPS * EXPERTS_PER_GROUP
TOP_K = 2
D_EXPERT = 512
RMS_EPS = 1e-6

kernel_name = "hybrid_sgu_natten_hmoe_dit"


def rms_norm(x, w):
    xf = x.astype(jnp.float32)
    y = xf * lax.rsqrt(jnp.mean(xf * xf, axis=-1, keepdims=True) + RMS_EPS)
    return (y * w.astype(jnp.float32)).astype(x.dtype)


def modulate(h, shift, scale):
    return h * (1 + scale) + shift


def axial_rope(x):
    B, S, H, hd = x.shape
    t = jnp.arange(S, dtype=jnp.int32)
    pos = jnp.stack([t // GRID_W, t % GRID_W], axis=-1).astype(jnp.float32)
    inv_freq = ROPE_THETA ** (-jnp.arange(ROPE_PAIRS, dtype=jnp.float32) / ROPE_PAIRS)
    ang = pos[:, :, None] * inv_freq
    cos = jnp.cos(ang)[None, :, None]
    sin = jnp.sin(ang)[None, :, None]
    xf = x.astype(jnp.float32).reshape(B, S, H, 2, 2, ROPE_PAIRS)
    x1, x2 = xf[..., 0, :], xf[..., 1, :]
    out = jnp.stack([x1 * cos - x2 * sin, x2 * cos + x1 * sin], axis=-2)
    return out.reshape(B, S, H, hd).astype(x.dtype)


def sgu_branch(p, norm_w, w_s, b_s):
    B, L, _ = p.shape
    u, v = jnp.split(jax.nn.gelu(p[..., :2 * SGU_WIDTH]), 2, axis=-1)
    v = rms_norm(v, norm_w)
    vc = v.reshape(B, L // SGU_CHUNK, SGU_CHUNK, SGU_GROUPS, SGU_GROUP_CH)
    z = jnp.einsum('gnm,bkmgc->bkngc', w_s, vc) + b_s.T[None, None, :, :, None]
    return u * z.reshape(B, L, SGU_WIDTH)


def qkv_heads(p):
    B, L, _ = p.shape
    q, k, v = jnp.split(p[..., 2 * SGU_WIDTH:], 3, axis=-1)
    return (q.reshape(B, L, NA_HEADS, HEAD_DIM),
            k.reshape(B, L, NA_HEADS, HEAD_DIM),
            v.reshape(B, L, NA_HEADS, HEAD_DIM))


def neighbourhood_attention(q_rot, q_plain, k_rot, v, k_ctx, v_ctx, rpb):
    B, S, H, hd = q_rot.shape
    rows = S // GRID_W
    wr = min(WIN_ROWS, rows)
    n_loc = wr * WIN_COLS
    scale = hd ** -0.5
    qg = q_rot.reshape(B, rows, GRID_W, H, hd)
    qpg = q_plain.reshape(B, rows, GRID_W, H, hd)
    kg = k_rot.reshape(B, rows, GRID_W, H, hd)
    vg = v.reshape(B, rows, GRID_W, H, hd)
    col = jnp.arange(GRID_W, dtype=jnp.int32)
    col_start = jnp.clip(col - WIN_COLS // 2, 0, GRID_W - WIN_COLS)
    col_idx = col_start[:, None] + jnp.arange(WIN_COLS, dtype=jnp.int32)
    col_bias_idx = col_idx - col[:, None] + (WIN_COLS - 1)

    def one_row(r):
        r0 = jnp.clip(r - wr // 2, 0, rows - wr)
        k_blk = lax.dynamic_slice_in_dim(kg, r0, wr, axis=1)
        v_blk = lax.dynamic_slice_in_dim(vg, r0, wr, axis=1)
        k_win = k_blk[:, :, col_idx]
        v_win = v_blk[:, :, col_idx]
        q_r = lax.dynamic_index_in_dim(qg, r, axis=1, keepdims=False)
        qp_r = lax.dynamic_index_in_dim(qpg, r, axis=1, keepdims=False)
        row_bias_idx = r0 + jnp.arange(wr, dtype=jnp.int32) - r + (WIN_ROWS - 1)
        bias = rpb[:, row_bias_idx][:, :, col_bias_idx].transpose(0, 2, 1, 3)
        s_loc = jnp.einsum('bchd,brcwhd->bhcrw', q_r, k_win).astype(jnp.float32) * scale
        s_loc = s_loc + bias[None].astype(jnp.float32)
        s_ctx = jnp.einsum('bchd,bjhd->bhcj', qp_r, k_ctx).astype(jnp.float32) * scale
        s = jnp.concatenate([s_loc.reshape(B, H, GRID_W, n_loc), s_ctx], axis=-1)
        p = jax.nn.softmax(s, axis=-1).astype(v.dtype)
        p_loc = p[..., :n_loc].reshape(B, H, GRID_W, wr, WIN_COLS)
        p_ctx = p[..., n_loc:]
        return (jnp.einsum('bhcrw,brcwhd->bchd', p_loc, v_win)
                + jnp.einsum('bhcj,bjhd->bchd', p_ctx, v_ctx))

    out = lax.map(one_row, jnp.arange(rows, dtype=jnp.int32))
    return out.transpose(1, 0, 2, 3, 4).reshape(B, S, H * hd)


def context_attention(q, k, v):
    B, L, H, hd = q.shape
    s = jnp.einsum('bihd,bjhd->bhij', q, k).astype(jnp.float32) * (hd ** -0.5)
    p = jax.nn.softmax(s, axis=-1).astype(v.dtype)
    return jnp.einsum('bhij,bjhd->bihd', p, v).reshape(B, L, H * hd)


def merge_branches(h, a, o, w_gate, b_gate, w_a, w_b, w_o):
    g_a, g_b = jnp.split(jax.nn.sigmoid(h @ w_gate + b_gate), 2, axis=-1)
    return (g_a * (a @ w_a) + g_b * (o @ w_b)) @ w_o


def hierarchical_moe(h, w_rg, b_rg, w_re, b_re, w_gate, w_up, w_down):
    shp = h.shape
    t = h.reshape(-1, shp[-1])
    g_prob = jax.nn.softmax((t @ w_rg + b_rg).astype(jnp.float32), axis=-1)
    g_top_p, g_idx = lax.top_k(g_prob, 1)
    e_logits = (t @ w_re + b_re).astype(jnp.float32).reshape(-1, N_GROUPS, EXPERTS_PER_GROUP)
    e_in_group = jnp.take_along_axis(e_logits, g_idx[:, :, None], axis=1)[:, 0]
    e_top, e_idx = lax.top_k(e_in_group, TOP_K)
    e_w = jax.nn.softmax(e_top, axis=-1) * g_top_p
    expert_id = g_idx * EXPERTS_PER_GROUP + e_idx
    combine = jnp.sum(jax.nn.one_hot(expert_id, N_EXPERTS, dtype=jnp.float32) * e_w[..., None],
                      axis=1).astype(t.dtype)
    y = jnp.zeros_like(t)
    for e in range(N_EXPERTS):
        hid = jax.nn.silu(t @ w_gate[e]) * (t @ w_up[e])
        y = y + combine[:, e:e + 1] * (hid @ w_down[e])
    return y.reshape(shp)


def setup_inputs(seed: int = 0) -> dict:
    key = jax.random.key(seed)
    ks = jax.random.split(key, 26)
    L, D = DEPTH, D_MODEL
    f32 = jnp.float32

    def nrm(k, shape, fan_in, gain=1.0):
        return jax.random.normal(k, shape, f32) * (gain * fan_in ** -0.5)

    def small(k, shape, s):
        return jax.random.normal(k, shape, f32) * s

    def gains(k, shape):
        return 1.0 + 0.05 * jax.random.normal(k, shape, f32)

    return {
        "x": jax.random.normal(ks[0], (BATCH, SEQ, D), f32),
        "c": jax.random.normal(ks[1], (BATCH, D), f32),
        "ctx": jax.random.normal(ks[2], (BATCH, CTX_LEN, D), f32),
        "c_ctx": jax.random.normal(ks[3], (D,), f32),
        "w_ada": nrm(ks[4], (L, D, N_MOD * D), D),
        "b_ada": small(ks[5], (L, N_MOD * D), 0.02),
        "norm_mix_w": gains(ks[6], (L, D)),
        "norm_ffn_w": gains(ks[7], (L, D)),
        "w_in": nrm(ks[8], (L, D, IN_PROJ_WIDTH), D),
        "sgu_norm_w": gains(ks[9], (L, SGU_WIDTH)),
        "sgu_w_s": nrm(ks[10], (L, SGU_GROUPS, SGU_CHUNK, SGU_CHUNK), SGU_CHUNK, 0.5),
        "sgu_b_s": gains(ks[11], (L, SGU_GROUPS, SGU_CHUNK)),
        "na_rpb": small(ks[12], (L, NA_HEADS, 2 * WIN_ROWS - 1, 2 * WIN_COLS - 1), 0.02),
        "w_merge_gate": nrm(ks[13], (L, D, 2 * D), D),
        "b_merge_gate": small(ks[14], (L, 2 * D), 0.02),
        "w_branch_a": nrm(ks[15], (L, SGU_WIDTH, D), SGU_WIDTH),
        "w_branch_b": nrm(ks[16], (L, NA_WIDTH, D), NA_WIDTH),
        "w_out": nrm(ks[17], (L, D, D), D),
        "w_router_group": nrm(ks[18], (L, D, N_GROUPS), D),
        "b_router_group": small(ks[19], (L, N_GROUPS), 0.01),
        "w_router_expert": nrm(ks[20], (L, D, N_EXPERTS), D),
        "b_router_expert": small(ks[21], (L, N_EXPERTS), 0.01),
        "w_exp_gate": nrm(ks[22], (L, N_EXPERTS, D, D_EXPERT), D),
        "w_exp_up": nrm(ks[23], (L, N_EXPERTS, D, D_EXPERT), D),
        "w_exp_down": nrm(ks[24], (L, N_EXPERTS, D_EXPERT, D), D_EXPERT),
        "final_norm_w": gains(ks[25], (D,)),
    }


def reference(x, c, ctx, c_ctx, w_ada, b_ada, norm_mix_w, norm_ffn_w, w_in, sgu_norm_w, sgu_w_s,
              sgu_b_s, na_rpb, w_merge_gate, b_merge_gate, w_branch_a, w_branch_b, w_out,
              w_router_group, b_router_group, w_router_expert, b_router_expert,
              w_exp_gate, w_exp_up, w_exp_down, final_norm_w):
    B = x.shape[0]
    for l in range(DEPTH):
        mod_x = (jax.nn.silu(c) @ w_ada[l] + b_ada[l]).reshape(B, N_MOD, D_MODEL)[:, :, None, :]
        mod_c = (jax.nn.silu(c_ctx) @ w_ada[l] + b_ada[l]).reshape(N_MOD, D_MODEL)

        hx = modulate(rms_norm(x, norm_mix_w[l]), mod_x[:, 0], mod_x[:, 1])
        hc = modulate(rms_norm(ctx, norm_mix_w[l]), mod_c[0], mod_c[1])
        px = hx @ w_in[l]
        pc = hc @ w_in[l]
        qx, kx, vx = qkv_heads(px)
        qc, kc, vc = qkv_heads(pc)
        a_x = sgu_branch(px, sgu_norm_w[l], sgu_w_s[l], sgu_b_s[l])
        o_x = neighbourhood_attention(axial_rope(qx), qx, axial_rope(kx), vx, kc, vc, na_rpb[l])
        x = x + mod_x[:, 2] * merge_branches(hx, a_x, o_x, w_merge_gate[l], b_merge_gate[l],
                                             w_branch_a[l], w_branch_b[l], w_out[l])
        hx = modulate(rms_norm(x, norm_ffn_w[l]), mod_x[:, 3], mod_x[:, 4])
        x = x + mod_x[:, 5] * hierarchical_moe(hx, w_router_group[l], b_router_group[l],
                                               w_router_expert[l], b_router_expert[l],
                                               w_exp_gate[l], w_exp_up[l], w_exp_down[l])

        if l < DEPTH - 1:
            a_c = sgu_branch(pc, sgu_norm_w[l], sgu_w_s[l], sgu_b_s[l])
            o_c = context_attention(qc, kc, vc)
            ctx = ctx + mod_c[2] * merge_branches(hc, a_c, o_c, w_merge_gate[l], b_merge_gate[l],
                                                  w_branch_a[l], w_branch_b[l], w_out[l])
            hc = modulate(rms_norm(ctx, norm_ffn_w[l]), mod_c[3], mod_c[4])
            ctx = ctx + mod_c[5] * hierarchical_moe(hc, w_router_group[l], b_router_group[l],
                                                    w_router_expert[l], b_router_expert[l],
                                                    w_exp_gate[l], w_exp_up[l], w_exp_down[l])
    return rms_norm(x, final_norm_w)
```

```python
import functools

import numpy as np
import jax
import jax.numpy as jnp
from jax import lax
from jax.experimental import pallas as pl
from jax.experimental.pallas import tpu as pltpu

GRID_W = 64
SGU_CHUNK = 128
SGU_GROUPS = 8
NA_HEADS = 8
HEAD_DIM = 128
WIN_ROWS = 8
WIN_COLS = 16
ROPE_THETA = 10000.0
N_GROUPS = 4
EXPERTS_PER_GROUP = 8
RMS_EPS = 1e-6

LANES = 128
Q_ROWS = 4
Q_BLOCK = Q_ROWS * GRID_W
KEY_ROWS = Q_ROWS + WIN_ROWS - 1
MASKED = -1e30
VMEM_LIMIT = 56 * 1024 * 1024

BF16 = jnp.bfloat16
F32 = jnp.float32
U32 = jnp.uint32


def _params(n_grid_dims, vmem=VMEM_LIMIT):
    return pltpu.CompilerParams(dimension_semantics=("arbitrary",) * n_grid_dims, vmem_limit_bytes=vmem)


def _resident(shape):
    nd = len(shape)
    return pl.BlockSpec(shape, lambda *_: (0,) * nd, pipeline_mode=pl.Buffered(1))


def _pack_bf16_pair(lo, hi):
    lo_bits = lax.bitcast_convert_type(lo.astype(BF16).astype(F32), U32)
    hi_bits = lax.bitcast_convert_type(hi.astype(BF16).astype(F32), U32)
    return (hi_bits & jnp.uint32(0xFFFF0000)) | (lo_bits >> 16)


def _unpack_bf16_pair(w):
    lo = lax.bitcast_convert_type(w << 16, F32)
    hi = lax.bitcast_convert_type(w & jnp.uint32(0xFFFF0000), F32)
    return lo, hi


def _rms_modulate(x, norm_w, shift, scale):
    y = x * lax.rsqrt(jnp.mean(x * x, axis=-1, keepdims=True) + RMS_EPS) * norm_w
    return y * (1.0 + scale) + shift


def _ada_body(ct_ref, w_ref, b_ref, o_ref, *, n_rows):
    s = ct_ref[...]
    s = s * jax.nn.sigmoid(s)
    w = w_ref[0]
    o_ref[...] = jnp.zeros_like(o_ref)
    for r in range(n_rows):
        o_ref[0, r:r + 1, :] = jnp.sum(w * s[:, r:r + 1], axis=0, keepdims=True) + b_ref[0]


def _ada_call(cond, w_ada, b_ada):
    n_rows, d = cond.shape
    depth, _, n = w_ada.shape
    tn = min(n, 512)
    ct = jnp.zeros((d, 8), F32).at[:, :n_rows].set(cond.T)
    return pl.pallas_call(
        functools.partial(_ada_body, n_rows=n_rows),
        grid=(depth, n // tn),
        in_specs=[pl.BlockSpec((d, 8), lambda l, j: (0, 0)),
                  pl.BlockSpec((1, d, tn), lambda l, j: (l, 0, j)),
                  pl.BlockSpec((1, 1, tn), lambda l, j: (l, 0, j))],
        out_specs=pl.BlockSpec((1, 8, tn), lambda l, j: (l, 0, j)),
        out_shape=jax.ShapeDtypeStruct((depth, 8, n), F32),
        compiler_params=_params(2),
    )(ct, w_ada, b_ada.reshape(depth, 1, n))


def _inproj_body(*refs, n_main_tiles, two_src, sgu_w, na_w):
    if two_src:
        xa_ref, xb_ref = refs[:2]
        refs = refs[2:]
    else:
        xa_ref = refs[0]
        refs = refs[1:]
    (mod_ref, nw_ref, w_ref, snw_ref, ws_ref, bst_ref, cos_ref, sa_ref, sb_ref,
     h_ref, a_ref, qp_ref, qr_ref, kr_ref, v_ref) = refs
    if two_src:
        x = jnp.where(pl.program_id(0) < n_main_tiles, xa_ref[...], xb_ref[...])
    else:
        x = xa_ref[...]
    tm = x.shape[0]
    h = _rms_modulate(x, nw_ref[...], mod_ref[0, 0:1, :], mod_ref[0, 1:2, :])
    hb = h.astype(BF16)
    h_ref[...] = hb

    def proj(lo, width):
        return jnp.dot(hb, w_ref[:, lo:lo + width], preferred_element_type=F32)

    u = jax.nn.gelu(proj(0, sgu_w))
    v = jax.nn.gelu(proj(sgu_w, sgu_w))
    vn = v * lax.rsqrt(jnp.mean(v * v, axis=-1, keepdims=True) + RMS_EPS) * snw_ref[...]
    vnb = vn.astype(BF16)
    gch = sgu_w // SGU_GROUPS
    for c in range(tm // SGU_CHUNK):
        rows = slice(c * SGU_CHUNK, (c + 1) * SGU_CHUNK)
        for g in range(SGU_GROUPS):
            cols = slice(g * gch, (g + 1) * gch)
            z = jnp.dot(ws_ref[g], vnb[rows, cols], preferred_element_type=F32) + bst_ref[:, g:g + 1]
            a_ref[rows, cols] = (u[rows, cols] * z).astype(BF16)

    cos, sa, sb = cos_ref[...], sa_ref[...], sb_ref[...]

    def rope_into(p, out_ref):
        for hh in range(NA_HEADS):
            cols = slice(hh * HEAD_DIM, (hh + 1) * HEAD_DIM)
            xh = p[:, cols]
            out_ref[:, cols] = (xh * cos + pltpu.roll(xh, HEAD_DIM - 32, 1) * sa
                                + pltpu.roll(xh, 32, 1) * sb).astype(BF16)

    q = proj(2 * sgu_w, na_w)
    qp_ref[...] = q.astype(BF16)
    rope_into(q, qr_ref)
    rope_into(proj(2 * sgu_w + na_w, na_w), kr_ref)
    v_ref[...] = proj(2 * sgu_w + 2 * na_w, na_w).astype(BF16)


def _inproj_call(x_main, x_ctx, mod, norm_w, w_in_b, sgu_norm_w, w_s_b, b_s_t, cos, sa, sb,
                 *, rows_per_mod, tm):
    d = x_main.shape[1]
    n_main = x_main.shape[0]
    two_src = x_ctx is not None
    tt = n_main + (x_ctx.shape[0] if two_src else 0)
    n_main_tiles = n_main // tm
    n_mod = mod.shape[0]
    sgu_w = sgu_norm_w.shape[-1]
    na_w = NA_HEADS * HEAD_DIM
    tiles_per_mod = rows_per_mod // tm

    def row_map(i):
        return (i, 0)

    x_specs = [pl.BlockSpec((tm, d), lambda i: (jnp.minimum(i, n_main_tiles - 1), 0))]
    x_args = [x_main]
    if two_src:
        x_specs.append(pl.BlockSpec((tm, d), lambda i: (jnp.maximum(i - n_main_tiles, 0), 0)))
        x_args.append(x_ctx)
    in_specs = x_specs + [
        pl.BlockSpec((1, 6, d), lambda i: (jnp.minimum(i // tiles_per_mod, n_mod - 1), 0, 0)),
        _resident((1, d)),
        _resident(w_in_b.shape),
        _resident((1, sgu_w)),
        _resident(w_s_b.shape),
        _resident(b_s_t.shape),
        pl.BlockSpec((tm, HEAD_DIM), row_map),
        pl.BlockSpec((tm, HEAD_DIM), row_map),
        pl.BlockSpec((tm, HEAD_DIM), row_map),
    ]
    out_widths = [d, sgu_w, na_w, na_w, na_w, na_w]
    return pl.pallas_call(
        functools.partial(_inproj_body, n_main_tiles=n_main_tiles, two_src=two_src, sgu_w=sgu_w, na_w=na_w),
        grid=(tt // tm,),
        in_specs=in_specs,
        out_specs=[pl.BlockSpec((tm, w), row_map) for w in out_widths],
        out_shape=[jax.ShapeDtypeStruct((tt, w), BF16) for w in out_widths],
        compiler_params=_params(1),
    )(*x_args, mod, norm_w.reshape(1, d), w_in_b, sgu_norm_w.reshape(1, sgu_w), w_s_b, b_s_t, cos, sa, sb)


def _softmax_pv(scores, values):
    m = functools.reduce(jnp.maximum, [jnp.max(s, axis=-1, keepdims=True) for s in scores])
    ps = [jnp.exp(s - m) for s in scores]
    denom = functools.reduce(jnp.add, [jnp.sum(p, axis=-1, keepdims=True) for p in ps])
    acc = functools.reduce(jnp.add, [jnp.dot(p.astype(BF16), v, preferred_element_type=F32)
                                     for p, v in zip(ps, values)])
    return acc / denom


def _qk(q, k):
    return lax.dot_general(q, k, (((1,), (1,)), ((), ())), preferred_element_type=F32) * (HEAD_DIM ** -0.5)


def _attn_body(qr_ref, qp_ref, k_ref, v_ref, kc_ref, vc_ref, bias_ref, o_ref, *, n_blocks, grid_rows):
    i = pl.program_id(2)
    n_keys = KEY_ROWS * GRID_W

    @pl.when(i < n_blocks)
    def _():
        k_row0 = jnp.clip(i * Q_ROWS - WIN_ROWS // 2, 0, grid_rows - KEY_ROWS)
        start = pl.multiple_of(k_row0 * GRID_W, GRID_W)
        k_loc = k_ref[pl.ds(start, n_keys), :]
        v_loc = v_ref[pl.ds(start, n_keys), :]
        s_loc = _qk(qr_ref[...], k_loc) + bias_ref[0, 0]
        s_ctx = _qk(qp_ref[...], kc_ref[...])
        o_ref[...] = _softmax_pv([s_loc, s_ctx], [v_loc, vc_ref[...]]).astype(BF16)

    @pl.when(i >= n_blocks)
    def _():
        s_ctx = _qk(qp_ref[...], kc_ref[...])
        o_ref[...] = _softmax_pv([s_ctx], [vc_ref[...]]).astype(BF16)


def _attn_bias_tables(rpb, grid_rows):
    n_blocks = grid_rows // Q_ROWS
    wr = min(WIN_ROWS, grid_rows)
    assert wr == WIN_ROWS and grid_rows >= KEY_ROWS and grid_rows % Q_ROWS == 0

    def structure(blk):
        k_row0 = int(np.clip(blk * Q_ROWS - WIN_ROWS // 2, 0, grid_rows - KEY_ROWS))
        qi = np.arange(Q_BLOCK)
        ki = np.arange(KEY_ROWS * GRID_W)
        qr, qc = blk * Q_ROWS + qi // GRID_W, qi % GRID_W
        kr, kc = k_row0 + ki // GRID_W, ki % GRID_W
        r0 = np.clip(qr - wr // 2, 0, grid_rows - wr)
        c0 = np.clip(qc - WIN_COLS // 2, 0, GRID_W - WIN_COLS)
        valid = ((kr[None, :] >= r0[:, None]) & (kr[None, :] < r0[:, None] + wr)
                 & (kc[None, :] >= c0[:, None]) & (kc[None, :] < c0[:, None] + WIN_COLS))
        ri = np.clip(kr[None, :] - qr[:, None] + (WIN_ROWS - 1), 0, 2 * WIN_ROWS - 2)
        ci = np.clip(kc[None, :] - qc[:, None] + (WIN_COLS - 1), 0, 2 * WIN_COLS - 2)
        return valid, ri, ci

    kinds = [0, min(1, n_blocks - 1), n_blocks - 1]
    ref_interior = structure(kinds[1])
    for blk in range(1, n_blocks - 1):
        assert all(np.array_equal(a, b) for a, b in zip(structure(blk), ref_interior))
    tables = []
    for blk in kinds:
        valid, ri, ci = structure(blk)
        tables.append(jnp.where(valid[None], rpb[:, ri, ci].astype(F32), MASKED))
    return jnp.stack(tables)


def _attn_call(qr, qp, kr, v, bias_tables, *, batch, seq, ctx_len, with_ctx_queries):
    assert ctx_len == Q_BLOCK and seq % Q_BLOCK == 0
    grid_rows = seq // GRID_W
    n_blocks = grid_rows // Q_ROWS
    n_steps = n_blocks + (1 if with_ctx_queries else 0)
    ctx_block0 = batch * n_blocks
    n_out = batch * seq + (batch * ctx_len if with_ctx_queries else 0)
    n_keys = KEY_ROWS * GRID_W

    def q_map(b, h, i):
        return (jnp.where(i < n_blocks, b * n_blocks + i, ctx_block0 + b), h)

    def kind_map(b, h, i):
        return (jnp.where(i == 0, 0, jnp.where(i >= n_blocks - 1, 2, 1)), h, 0, 0)

    return pl.pallas_call(
        functools.partial(_attn_body, n_blocks=n_blocks, grid_rows=grid_rows),
        grid=(batch, NA_HEADS, n_steps),
        in_specs=[pl.BlockSpec((Q_BLOCK, HEAD_DIM), q_map),
                  pl.BlockSpec((Q_BLOCK, HEAD_DIM), q_map),
                  pl.BlockSpec((seq, HEAD_DIM), lambda b, h, i: (b, h)),
                  pl.BlockSpec((seq, HEAD_DIM), lambda b, h, i: (b, h)),
                  pl.BlockSpec((ctx_len, HEAD_DIM), lambda b, h, i: (ctx_block0 + b, h)),
                  pl.BlockSpec((ctx_len, HEAD_DIM), lambda b, h, i: (ctx_block0 + b, h)),
                  pl.BlockSpec((1, 1, Q_BLOCK, n_keys), kind_map)],
        out_specs=pl.BlockSpec((Q_BLOCK, HEAD_DIM), q_map),
        out_shape=jax.ShapeDtypeStruct((n_out, NA_HEADS * HEAD_DIM), BF16),
        compiler_params=_params(3),
    )(qr, qp, kr, v, kr, v, bias_tables)


def _merge_body(h_ref, a_ref, o_ref, wg_ref, bg_ref, wa_ref, wb_ref, m_ref, *, tn):
    d = m_ref.shape[1]
    hb, ab, ob = h_ref[...], a_ref[...], o_ref[...]
    for n0 in range(0, d, tn):
        cols = slice(n0, n0 + tn)
        gcols = slice(d + n0, d + n0 + tn)
        g_a = jax.nn.sigmoid(jnp.dot(hb, wg_ref[:, cols], preferred_element_type=F32) + bg_ref[:, cols])
        g_b = jax.nn.sigmoid(jnp.dot(hb, wg_ref[:, gcols], preferred_element_type=F32) + bg_ref[:, gcols])
        pa = jnp.dot(ab, wa_ref[:, cols], preferred_element_type=F32)
        pb = jnp.dot(ob, wb_ref[:, cols], preferred_element_type=F32)
        m_ref[:, cols] = (g_a * pa + g_b * pb).astype(BF16)


def _merge_call(h, a, o, w_gate_b, b_gate, w_a_b, w_b_b, *, n_rows, tm):
    d = h.shape[1]

    def row_map(i):
        return (i, 0)

    return pl.pallas_call(
        functools.partial(_merge_body, tn=min(d, 512)),
        grid=(n_rows // tm,),
        in_specs=[pl.BlockSpec((tm, d), row_map),
                  pl.BlockSpec((tm, a.shape[1]), row_map),
                  pl.BlockSpec((tm, o.shape[1]), row_map),
                  _resident(w_gate_b.shape), _resident((1, 2 * d)),
                  _resident(w_a_b.shape), _resident(w_b_b.shape)],
        out_specs=pl.BlockSpec((tm, d), row_map),
        out_shape=jax.ShapeDtypeStruct((n_rows, d), BF16),
        compiler_params=_params(1),
    )(h, a, o, w_gate_b, b_gate.reshape(1, 2 * d), w_a_b, w_b_b)


def _outproj_body(*refs, n_main_tiles, two_src, n_experts):
    if two_src:
        xa_ref, xb_ref = refs[:2]
        refs = refs[2:]
    else:
        xa_ref = refs[0]
        refs = refs[1:]
    m_ref, wo_ref, mod_ref, nw_ref, wr_ref, br_ref, xo_ref, hp_ref, route_ref, cnt_ref, carry_ref = refs
    i = pl.program_id(0)
    if two_src:
        x = jnp.where(i < n_main_tiles, xa_ref[...], xb_ref[...])
    else:
        x = xa_ref[...]
    tm, d = x.shape

    @pl.when(i == 0)
    def _():
        carry_ref[...] = jnp.zeros_like(carry_ref)

    y = jnp.dot(m_ref[...], wo_ref[...], preferred_element_type=F32)
    x_new = x + mod_ref[0, 2:3, :] * y
    xo_ref[...] = x_new
    h = _rms_modulate(x_new, nw_ref[...], mod_ref[0, 3:4, :], mod_ref[0, 4:5, :])
    hp_ref[...] = _pack_bf16_pair(h[:, :d // 2], h[:, d // 2:])

    logits = jnp.dot(h, wr_ref[...], preferred_element_type=F32, precision=lax.Precision.HIGHEST) + br_ref[...]
    lane = lax.broadcasted_iota(jnp.int32, logits.shape, 1)
    lane_f = lane.astype(F32)
    far = jnp.float32(4 * LANES)

    def first_argmax(vals):
        top = jnp.max(vals, axis=-1, keepdims=True)
        return top, jnp.min(jnp.where(vals == top, lane_f, far), axis=-1, keepdims=True)

    g_logits = jnp.where((lane >= n_experts) & (lane < n_experts + N_GROUPS), logits, MASKED)
    g_top, g_lane = first_argmax(g_logits)
    g_prob = 1.0 / jnp.sum(jnp.exp(g_logits - g_top), axis=-1, keepdims=True)
    e_lo = (g_lane - n_experts) * EXPERTS_PER_GROUP
    e_logits = jnp.where((lane_f >= e_lo) & (lane_f < e_lo + EXPERTS_PER_GROUP), logits, MASKED)
    top1, e1 = first_argmax(e_logits)
    top2, e2 = first_argmax(jnp.where(lane_f == e1, MASKED, e_logits))
    t = jnp.exp(top2 - top1)
    w1 = g_prob / (1.0 + t)
    w2 = g_prob * t / (1.0 + t)

    sel1, sel2 = lane_f == e1, lane_f == e2
    onehot = jnp.where(sel1 | sel2, 1.0, 0.0)
    r_i = lax.broadcasted_iota(jnp.int32, (tm, tm), 0)
    c_i = lax.broadcasted_iota(jnp.int32, (tm, tm), 1)
    earlier = jnp.where(c_i < r_i, 1.0, 0.0).astype(BF16)
    before = jnp.dot(earlier, onehot.astype(BF16), preferred_element_type=F32) + carry_ref[...]
    rank1 = jnp.sum(jnp.where(sel1, before, 0.0), axis=-1, keepdims=True)
    rank2 = jnp.sum(jnp.where(sel2, before, 0.0), axis=-1, keepdims=True)
    carry_ref[...] += jnp.sum(onehot, axis=0, keepdims=True)
    cnt_ref[...] = carry_ref[...]

    route = jnp.zeros_like(logits)
    for k, val in enumerate((e1, e2, w1, w2, rank1, rank2)):
        route = jnp.where(lane == k, val, route)
    route_ref[...] = route


def _outproj_call(x_main, x_ctx, m, w_out_b, mod, norm_w, w_router, b_router, *, rows_per_mod, tm, n_experts):
    d = x_main.shape[1]
    two_src = x_ctx is not None
    n_main = x_main.shape[0] if two_src else m.shape[0]
    n_rows = m.shape[0]
    n_main_tiles = n_main // tm
    n_mod = mod.shape[0]
    tiles_per_mod = rows_per_mod // tm

    def row_map(i):
        return (i, 0)

    x_specs = [pl.BlockSpec((tm, d), lambda i: (jnp.minimum(i, n_main_tiles - 1), 0))]
    x_args = [x_main]
    if two_src:
        x_specs.append(pl.BlockSpec((tm, d), lambda i: (jnp.maximum(i - n_main_tiles, 0), 0)))
        x_args.append(x_ctx)
    in_specs = x_specs + [
        pl.BlockSpec((tm, d), row_map),
        _resident(w_out_b.shape),
        pl.BlockSpec((1, 6, d), lambda i: (jnp.minimum(i // tiles_per_mod, n_mod - 1), 0, 0)),
        _resident((1, d)),
        _resident(w_router.shape),
        _resident((1, LANES)),
    ]
    return pl.pallas_call(
        functools.partial(_outproj_body, n_main_tiles=n_main_tiles, two_src=two_src, n_experts=n_experts),
        grid=(n_rows // tm,),
        in_specs=in_specs,
        out_specs=[pl.BlockSpec((tm, d), row_map),
                   pl.BlockSpec((tm, d // 2), row_map),
                   pl.BlockSpec((tm, LANES), row_map),
                   pl.BlockSpec((1, LANES), lambda i: (0, 0))],
        out_shape=[jax.ShapeDtypeStruct((n_rows, d), F32),
                   jax.ShapeDtypeStruct((n_rows, d // 2), U32),
                   jax.ShapeDtypeStruct((n_rows, LANES), F32),
                   jax.ShapeDtypeStruct((1, LANES), F32)],
        scratch_shapes=[pltpu.VMEM((1, LANES), F32)],
        compiler_params=_params(1),
    )(*x_args, m, w_out_b, mod, norm_w.reshape(1, d), w_router, b_router)


def _row_copy(src_ref, src_row, dst_ref, dst_row, sem):
    return pltpu.make_async_copy(src_ref.at[pl.ds(src_row, 1), :], dst_ref.at[pl.ds(dst_row, 1), :], sem)


def _dispatch_body(p1_ref, p2_ref, h_ref, slots_in_ref, slots_ref, sem, *, tm):
    del slots_in_ref
    base = pl.program_id(0) * tm

    def issue(r, carry):
        _row_copy(h_ref, r, slots_ref, p1_ref[base + r], sem).start()
        _row_copy(h_ref, r, slots_ref, p2_ref[base + r], sem).start()
        return carry

    def drain(r, carry):
        _row_copy(h_ref, 0, slots_ref, 0, sem).wait()
        _row_copy(h_ref, 0, slots_ref, 0, sem).wait()
        return carry

    lax.fori_loop(0, tm, issue, 0)
    lax.fori_loop(0, tm, drain, 0)


def _dispatch_call(pos1, pos2, h_packed, n_slots, *, tm):
    n_rows, half = h_packed.shape
    return pl.pallas_call(
        functools.partial(_dispatch_body, tm=tm),
        grid_spec=pltpu.PrefetchScalarGridSpec(
            num_scalar_prefetch=2,
            grid=(n_rows // tm,),
            in_specs=[pl.BlockSpec((tm, half), lambda i, p1, p2: (i, 0)),
                      pl.BlockSpec(memory_space=pl.ANY)],
            out_specs=pl.BlockSpec(memory_space=pl.ANY),
            scratch_shapes=[pltpu.SemaphoreType.DMA(())],
        ),
        out_shape=jax.ShapeDtypeStruct((n_slots, half), U32),
        input_output_aliases={3: 0},
        compiler_params=_params(1),
    )(pos1, pos2, h_packed, jnp.zeros((n_slots, half), U32))


def _ffn_body(te_ref, nu_ref, xs_ref, wg_ref, wu_ref, wd_ref, ys_ref, wgb_ref, wub_ref, wdb_ref):
    j = pl.program_id(0)
    half = xs_ref.shape[1]

    @pl.when((j == 0) | (te_ref[j] != te_ref[jnp.maximum(j - 1, 0)]))
    def _():
        wgb_ref[...] = wg_ref[0].astype(BF16)
        wub_ref[...] = wu_ref[0].astype(BF16)
        wdb_ref[...] = wd_ref[0].astype(BF16)

    @pl.when(j < nu_ref[0])
    def _():
        lo, hi = _unpack_bf16_pair(xs_ref[...])
        lo, hi = lo.astype(BF16), hi.astype(BF16)

        def up(w_ref):
            return (jnp.dot(lo, w_ref[:half, :], preferred_element_type=F32)
                    + jnp.dot(hi, w_ref[half:, :], preferred_element_type=F32))

        g = up(wgb_ref)
        hid = (g * jax.nn.sigmoid(g) * up(wub_ref)).astype(BF16)
        y = jnp.dot(hid, wdb_ref[...], preferred_element_type=F32)
        ys_ref[...] = _pack_bf16_pair(y[:, :half], y[:, half:])

    @pl.when(j >= nu_ref[0])
    def _():
        ys_ref[...] = jnp.zeros_like(ys_ref)


def _ffn_call(tile_expert, n_used, slots, w_gate, w_up, w_down, *, tm):
    n_slots, half = slots.shape
    _, d, de = w_gate.shape
    return pl.pallas_call(
        _ffn_body,
        grid_spec=pltpu.PrefetchScalarGridSpec(
            num_scalar_prefetch=2,
            grid=(n_slots // tm,),
            in_specs=[pl.BlockSpec((tm, half), lambda j, te, nu: (j, 0)),
                      pl.BlockSpec((1, d, de), lambda j, te, nu: (te[j], 0, 0)),
                      pl.BlockSpec((1, d, de), lambda j, te, nu: (te[j], 0, 0)),
                      pl.BlockSpec((1, de, d), lambda j, te, nu: (te[j], 0, 0))],
            out_specs=pl.BlockSpec((tm, half), lambda j, te, nu: (j, 0)),
            scratch_shapes=[pltpu.VMEM((d, de), BF16), pltpu.VMEM((d, de), BF16), pltpu.VMEM((de, d), BF16)],
        ),
        out_shape=jax.ShapeDtypeStruct((n_slots, half), U32),
        compiler_params=_params(1),
    )(tile_expert, n_used, slots, w_gate, w_up, w_down)


def _combine_body(p1_ref, p2_ref, x_ref, route_ref, mod_ref, fw_ref, ys_ref, o_ref, y1_ref, y2_ref, sem,
                  *, tm, final_norm):
    base = pl.program_id(0) * tm

    def issue(r, carry):
        _row_copy(ys_ref, p1_ref[base + r], y1_ref, r, sem).start()
        _row_copy(ys_ref, p2_ref[base + r], y2_ref, r, sem).start()
        return carry

    def drain(r, carry):
        _row_copy(ys_ref, 0, y1_ref, 0, sem).wait()
        _row_copy(ys_ref, 0, y2_ref, 0, sem).wait()
        return carry

    lax.fori_loop(0, tm, issue, 0)
    lax.fori_loop(0, tm, drain, 0)
    w1, w2 = route_ref[:, 2:3], route_ref[:, 3:4]
    lo1, hi1 = _unpack_bf16_pair(y1_ref[...])
    lo2, hi2 = _unpack_bf16_pair(y2_ref[...])
    y = jnp.concatenate([w1 * lo1 + w2 * lo2, w1 * hi1 + w2 * hi2], axis=-1)
    x_new = x_ref[...] + mod_ref[0, 5:6, :] * y
    if final_norm:
        x_new = x_new * lax.rsqrt(jnp.mean(x_new * x_new, axis=-1, keepdims=True) + RMS_EPS) * fw_ref[...]
    o_ref[...] = x_new


def _combine_call(pos1, pos2, x, route, mod, final_w, ys, *, rows_per_mod, tm, final_norm):
    n_rows, d = x.shape
    half = ys.shape[1]
    n_mod = mod.shape[0]
    tiles_per_mod = rows_per_mod // tm
    return pl.pallas_call(
        functools.partial(_combine_body, tm=tm, final_norm=final_norm),
        grid_spec=pltpu.PrefetchScalarGridSpec(
            num_scalar_prefetch=2,
            grid=(n_rows // tm,),
            in_specs=[pl.BlockSpec((tm, d), lambda i, p1, p2: (i, 0)),
                      pl.BlockSpec((tm, LANES), lambda i, p1, p2: (i, 0)),
                      pl.BlockSpec((1, 6, d), lambda i, p1, p2: (jnp.minimum(i // tiles_per_mod, n_mod - 1), 0, 0)),
                      pl.BlockSpec((1, d), lambda i, p1, p2: (0, 0)),
                      pl.BlockSpec(memory_space=pl.ANY)],
            out_specs=pl.BlockSpec((tm, d), lambda i, p1, p2: (i, 0)),
            scratch_shapes=[pltpu.VMEM((tm, half), U32), pltpu.VMEM((tm, half), U32),
                            pltpu.SemaphoreType.DMA(())],
        ),
        out_shape=jax.ShapeDtypeStruct((n_rows, d), F32),
        compiler_params=_params(1),
    )(pos1, pos2, x, route, mod, final_w.reshape(1, d), ys)


def _slot_plan(route, counts, n_experts, tm):
    n_rows = route.shape[0]
    n_tiles = (2 * n_rows) // tm + n_experts
    cnt = counts[0, :n_experts].astype(jnp.int32)
    padded = ((cnt + tm - 1) // tm) * tm
    ends = jnp.cumsum(padded)
    starts = ends - padded
    e1 = route[:, 0].astype(jnp.int32)
    e2 = route[:, 1].astype(jnp.int32)
    pos1 = starts[e1] + route[:, 4].astype(jnp.int32)
    pos2 = starts[e2] + route[:, 5].astype(jnp.int32)
    n_used = ends[-1] // tm
    tile_start = jnp.arange(n_tiles, dtype=jnp.int32) * tm
    tile_expert = jnp.sum((ends[None, :] <= tile_start[:, None]).astype(jnp.int32), axis=1)
    last_used = tile_expert[jnp.maximum(n_used - 1, 0)]
    tile_expert = jnp.where(tile_start < ends[-1], tile_expert, last_used)
    tile_expert = jnp.minimum(tile_expert, n_experts - 1).astype(jnp.int32)
    return pos1, pos2, tile_expert, n_used.reshape(1).astype(jnp.int32), n_tiles * tm


def _rope_tables(batch, seq, ctx_rows):
    pairs = HEAD_DIM // 4
    t = np.arange(seq)
    pos = np.stack([t // GRID_W, t % GRID_W], axis=-1).astype(np.float32)
    inv_freq = (ROPE_THETA ** (-np.arange(pairs, dtype=np.float32) / pairs)).astype(np.float32)
    ang = pos[:, :, None] * inv_freq
    cos = np.repeat(np.cos(ang)[:, :, None, :], 2, axis=2).reshape(seq, HEAD_DIM)
    sin = np.sin(ang)
    zero = np.zeros_like(sin)
    sa = np.stack([-sin, zero], axis=2).reshape(seq, HEAD_DIM)
    sb = np.stack([zero, sin], axis=2).reshape(seq, HEAD_DIM)

    def full(tab, fill):
        return jnp.asarray(np.concatenate([np.tile(tab, (batch, 1)),
                                           np.full((ctx_rows, HEAD_DIM), fill, np.float32)]), F32)

    return full(cos, 1.0), full(sa, 0.0), full(sb, 0.0)


def kernel(x, c, ctx, c_ctx, w_ada, b_ada, norm_mix_w, norm_ffn_w, w_in, sgu_norm_w, sgu_w_s, sgu_b_s, na_rpb,
           w_merge_gate, b_merge_gate, w_branch_a, w_branch_b, w_out, w_router_group, b_router_group,
           w_router_expert, b_router_expert, w_exp_gate, w_exp_up, w_exp_down, final_norm_w):
    batch, seq, d = x.shape
    ctx_len = ctx.shape[1]
    depth = w_ada.shape[0]
    n_experts = w_exp_gate.shape[1]
    n_lat = batch * seq
    n_ctx = batch * ctx_len
    tm = 256
    assert seq % tm == 0 and n_ctx % tm == 0 and tm % SGU_CHUNK == 0 and ctx_len % SGU_CHUNK == 0

    mods = _ada_call(jnp.concatenate([c, c_ctx[None]], axis=0), w_ada, b_ada)
    cos, sa, sb = _rope_tables(batch, seq, n_ctx)
    grid_rows = seq // GRID_W

    x_cur = x.reshape(n_lat, d)
    ctx_rows = ctx.reshape(n_ctx, d)
    x_all = None
    for l in range(depth):
        last = l == depth - 1
        mod = mods[l, :batch + 1].reshape(batch + 1, 6, d)
        w_in_b = w_in[l].astype(BF16)
        w_s_b = sgu_w_s[l].astype(BF16)
        b_s_t = sgu_b_s[l].T
        if x_all is None:
            src = (x_cur, ctx_rows)
        else:
            src = (x_all, None)
        h, a, qp, qr, kr, v = _inproj_call(src[0], src[1], mod, norm_mix_w[l], w_in_b, sgu_norm_w[l], w_s_b, b_s_t,
                                           cos, sa, sb, rows_per_mod=seq, tm=tm)
        o = _attn_call(qr, qp, kr, v, _attn_bias_tables(na_rpb[l], grid_rows), batch=batch, seq=seq,
                       ctx_len=ctx_len, with_ctx_queries=not last)
        n_rows = n_lat if last else n_lat + n_ctx
        m = _merge_call(h, a, o, w_merge_gate[l].astype(BF16), b_merge_gate[l], w_branch_a[l].astype(BF16),
                        w_branch_b[l].astype(BF16), n_rows=n_rows, tm=tm)
        w_router = jnp.zeros((d, LANES), F32)
        w_router = w_router.at[:, :n_experts].set(w_router_expert[l])
        w_router = w_router.at[:, n_experts:n_experts + N_GROUPS].set(w_router_group[l])
        b_router = jnp.zeros((1, LANES), F32)
        b_router = b_router.at[0, :n_experts].set(b_router_expert[l])
        b_router = b_router.at[0, n_experts:n_experts + N_GROUPS].set(b_router_group[l])
        x_mid, h_packed, route, counts = _outproj_call(src[0], src[1], m, w_out[l].astype(BF16), mod, norm_ffn_w[l],
                                                       w_router, b_router, rows_per_mod=seq, tm=tm,
                                                       n_experts=n_experts)
        pos1, pos2, tile_expert, n_used, n_slots = _slot_plan(route, counts, n_experts, tm)
        slots = _dispatch_call(pos1, pos2, h_packed, n_slots, tm=tm)
        ys = _ffn_call(tile_expert, n_used, slots, w_exp_gate[l], w_exp_up[l], w_exp_down[l], tm=tm)
        x_all = _combine_call(pos1, pos2, x_mid, route, mod, final_norm_w, ys, rows_per_mod=seq, tm=tm,
                              final_norm=last)
    return x_all.reshape(batch, seq, d)
```

```python
import functools

import numpy as np
import jax
import jax.numpy as jnp
from jax import lax
from jax.experimental import pallas as pl
from jax.experimental.pallas import tpu as pltpu

GRID_W = 64
SGU_CHUNK = 128
SGU_GROUPS = 8
NA_HEADS = 8
HEAD_DIM = 128
WIN_ROWS = 8
WIN_COLS = 16
ROPE_THETA = 10000.0
N_GROUPS = 4
EXPERTS_PER_GROUP = 8
RMS_EPS = 1e-6

LANES = 128
Q_ROWS = 4
Q_BLOCK = Q_ROWS * GRID_W
KEY_ROWS = Q_ROWS + WIN_ROWS - 1
MASKED = -1e30
VMEM_LIMIT = 56 * 1024 * 1024

BF16 = jnp.bfloat16
F32 = jnp.float32
U32 = jnp.uint32


def _params(n_grid_dims, vmem=VMEM_LIMIT):
    return pltpu.CompilerParams(dimension_semantics=("arbitrary",) * n_grid_dims, vmem_limit_bytes=vmem)


def _resident(shape):
    nd = len(shape)
    return pl.BlockSpec(shape, lambda *_: (0,) * nd, pipeline_mode=pl.Buffered(1))


def _pack_bf16_pair(lo, hi):
    lo_bits = lax.bitcast_convert_type(lo.astype(BF16).astype(F32), U32)
    hi_bits = lax.bitcast_convert_type(hi.astype(BF16).astype(F32), U32)
    return (hi_bits & jnp.uint32(0xFFFF0000)) | (lo_bits >> 16)


def _unpack_bf16_pair(w):
    lo = lax.bitcast_convert_type(w << 16, F32)
    hi = lax.bitcast_convert_type(w & jnp.uint32(0xFFFF0000), F32)
    return lo, hi


def _rms_modulate(x, norm_w, shift, scale):
    y = x * lax.rsqrt(jnp.mean(x * x, axis=-1, keepdims=True) + RMS_EPS) * norm_w
    return y * (1.0 + scale) + shift


def _ada_body(ct_ref, w_ref, b_ref, o_ref, *, n_rows):
    s = ct_ref[...]
    s = s * jax.nn.sigmoid(s)
    w = w_ref[0]
    o_ref[...] = jnp.zeros_like(o_ref)
    for r in range(n_rows):
        o_ref[0, r:r + 1, :] = jnp.sum(w * s[:, r:r + 1], axis=0, keepdims=True) + b_ref[0]


def _ada_call(cond, w_ada, b_ada):
    n_rows, d = cond.shape
    depth, _, n = w_ada.shape
    tn = min(n, 512)
    ct = jnp.zeros((d, 8), F32).at[:, :n_rows].set(cond.T)
    return pl.pallas_call(
        functools.partial(_ada_body, n_rows=n_rows),
        grid=(depth, n // tn),
        in_specs=[pl.BlockSpec((d, 8), lambda l, j: (0, 0)),
                  pl.BlockSpec((1, d, tn), lambda l, j: (l, 0, j)),
                  pl.BlockSpec((1, 1, tn), lambda l, j: (l, 0, j))],
        out_specs=pl.BlockSpec((1, 8, tn), lambda l, j: (l, 0, j)),
        out_shape=jax.ShapeDtypeStruct((depth, 8, n), F32),
        compiler_params=_params(2),
    )(ct, w_ada, b_ada.reshape(depth, 1, n))


def _inproj_body(*refs, n_main_tiles, two_src, sgu_w, na_w):
    if two_src:
        xa_ref, xb_ref = refs[:2]
        refs = refs[2:]
    else:
        xa_ref = refs[0]
        refs = refs[1:]
    (mod_ref, nw_ref, w_ref, snw_ref, ws_ref, bst_ref, cos_ref, sa_ref, sb_ref,
     h_ref, a_ref, qp_ref, qr_ref, kr_ref, v_ref) = refs
    if two_src:
        x = jnp.where(pl.program_id(0) < n_main_tiles, xa_ref[...], xb_ref[...])
    else:
        x = xa_ref[...]
    tm = x.shape[0]
    h = _rms_modulate(x, nw_ref[...], mod_ref[0, 0:1, :], mod_ref[0, 1:2, :])
    hb = h.astype(BF16)
    h_ref[...] = hb

    def proj(lo, width):
        return jnp.dot(hb, w_ref[:, lo:lo + width], preferred_element_type=F32)

    u = jax.nn.gelu(proj(0, sgu_w))
    v = jax.nn.gelu(proj(sgu_w, sgu_w))
    vn = v * lax.rsqrt(jnp.mean(v * v, axis=-1, keepdims=True) + RMS_EPS) * snw_ref[...]
    vnb = vn.astype(BF16)
    gch = sgu_w // SGU_GROUPS
    for c in range(tm // SGU_CHUNK):
        rows = slice(c * SGU_CHUNK, (c + 1) * SGU_CHUNK)
        for g in range(SGU_GROUPS):
            cols = slice(g * gch, (g + 1) * gch)
            z = jnp.dot(ws_ref[g], vnb[rows, cols], preferred_element_type=F32) + bst_ref[:, g:g + 1]
            a_ref[rows, cols] = (u[rows, cols] * z).astype(BF16)

    cos, sa, sb = cos_ref[...], sa_ref[...], sb_ref[...]

    def rope_into(p, out_ref):
        for hh in range(NA_HEADS):
            cols = slice(hh * HEAD_DIM, (hh + 1) * HEAD_DIM)
            xh = p[:, cols]
            out_ref[:, cols] = (xh * cos + pltpu.roll(xh, HEAD_DIM - 32, 1) * sa
                                + pltpu.roll(xh, 32, 1) * sb).astype(BF16)

    q = proj(2 * sgu_w, na_w)
    qp_ref[...] = q.astype(BF16)
    rope_into(q, qr_ref)
    rope_into(proj(2 * sgu_w + na_w, na_w), kr_ref)
    v_ref[...] = proj(2 * sgu_w + 2 * na_w, na_w).astype(BF16)


def _inproj_call(x_main, x_ctx, mod, norm_w, w_in_b, sgu_norm_w, w_s_b, b_s_t, cos, sa, sb,
                 *, rows_per_mod, tm):
    d = x_main.shape[1]
    n_main = x_main.shape[0]
    two_src = x_ctx is not None
    tt = n_main + (x_ctx.shape[0] if two_src else 0)
    n_main_tiles = n_main // tm
    n_mod = mod.shape[0]
    sgu_w = sgu_norm_w.shape[-1]
    na_w = NA_HEADS * HEAD_DIM
    tiles_per_mod = rows_per_mod // tm

    def row_map(i):
        return (i, 0)

    x_specs = [pl.BlockSpec((tm, d), lambda i: (jnp.minimum(i, n_main_tiles - 1), 0))]
    x_args = [x_main]
    if two_src:
        x_specs.append(pl.BlockSpec((tm, d), lambda i: (jnp.maximum(i - n_main_tiles, 0), 0)))
        x_args.append(x_ctx)
    in_specs = x_specs + [
        pl.BlockSpec((1, 6, d), lambda i: (jnp.minimum(i // tiles_per_mod, n_mod - 1), 0, 0)),
        _resident((1, d)),
        _resident(w_in_b.shape),
        _resident((1, sgu_w)),
        _resident(w_s_b.shape),
        _resident(b_s_t.shape),
        pl.BlockSpec((tm, HEAD_DIM), row_map),
        pl.BlockSpec((tm, HEAD_DIM), row_map),
        pl.BlockSpec((tm, HEAD_DIM), row_map),
    ]
    out_widths = [d, sgu_w, na_w, na_w, na_w, na_w]
    return pl.pallas_call(
        functools.partial(_inproj_body, n_main_tiles=n_main_tiles, two_src=two_src, sgu_w=sgu_w, na_w=na_w),
        grid=(tt // tm,),
        in_specs=in_specs,
        out_specs=[pl.BlockSpec((tm, w), row_map) for w in out_widths],
        out_shape=[jax.ShapeDtypeStruct((tt, w), BF16) for w in out_widths],
        compiler_params=_params(1),
    )(*x_args, mod, norm_w.reshape(1, d), w_in_b, sgu_norm_w.reshape(1, sgu_w), w_s_b, b_s_t, cos, sa, sb)


def _softmax_pv(scores, values):
    m = functools.reduce(jnp.maximum, [jnp.max(s, axis=-1, keepdims=True) for s in scores])
    ps = [jnp.exp(s - m) for s in scores]
    denom = functools.reduce(jnp.add, [jnp.sum(p, axis=-1, keepdims=True) for p in ps])
    acc = functools.reduce(jnp.add, [jnp.dot(p.astype(BF16), v, preferred_element_type=F32)
                                     for p, v in zip(ps, values)])
    return acc / denom


def _qk(q, k):
    return lax.dot_general(q, k, (((1,), (1,)), ((), ())), preferred_element_type=F32) * (HEAD_DIM ** -0.5)


def _attn_body(qr_ref, qp_ref, k_ref, v_ref, kc_ref, vc_ref, bias_ref, o_ref, *, n_blocks, grid_rows):
    i = pl.program_id(2)
    n_keys = KEY_ROWS * GRID_W

    @pl.when(i < n_blocks)
    def _():
        k_row0 = jnp.clip(i * Q_ROWS - WIN_ROWS // 2, 0, grid_rows - KEY_ROWS)
        start = pl.multiple_of(k_row0 * GRID_W, GRID_W)
        k_loc = k_ref[pl.ds(start, n_keys), :]
        v_loc = v_ref[pl.ds(start, n_keys), :]
        s_loc = _qk(qr_ref[...], k_loc) + bias_ref[0, 0]
        s_ctx = _qk(qp_ref[...], kc_ref[...])
        o_ref[...] = _softmax_pv([s_loc, s_ctx], [v_loc, vc_ref[...]]).astype(BF16)

    @pl.when(i >= n_blocks)
    def _():
        s_ctx = _qk(qp_ref[...], kc_ref[...])
        o_ref[...] = _softmax_pv([s_ctx], [vc_ref[...]]).astype(BF16)


def _attn_bias_tables(rpb, grid_rows):
    n_blocks = grid_rows // Q_ROWS
    wr = min(WIN_ROWS, grid_rows)
    assert wr == WIN_ROWS and grid_rows >= KEY_ROWS and grid_rows % Q_ROWS == 0
    n_ri, n_ci = 2 * WIN_ROWS - 1, 2 * WIN_COLS - 1

    qc = np.arange(GRID_W)
    c0 = np.clip(qc - WIN_COLS // 2, 0, GRID_W - WIN_COLS)
    col_valid = (qc[None, :] >= c0[:, None]) & (qc[None, :] < c0[:, None] + WIN_COLS)
    col_sel = (col_valid[:, :, None]
               & ((qc[None, :, None] - qc[:, None, None] + (WIN_COLS - 1)) == np.arange(n_ci))).astype(np.float32)

    def row_structure(blk):
        k_row0 = int(np.clip(blk * Q_ROWS - WIN_ROWS // 2, 0, grid_rows - KEY_ROWS))
        qr = blk * Q_ROWS + np.arange(Q_ROWS)
        kr = k_row0 + np.arange(KEY_ROWS)
        r0 = np.clip(qr - wr // 2, 0, grid_rows - wr)
        valid = (kr[None, :] >= r0[:, None]) & (kr[None, :] < r0[:, None] + wr)
        sel = valid[:, :, None] & ((kr[None, :, None] - qr[:, None, None] + (WIN_ROWS - 1)) == np.arange(n_ri))
        return valid, sel.astype(np.float32)

    kinds = [0, min(1, n_blocks - 1), n_blocks - 1]
    interior = row_structure(kinds[1])
    for blk in range(1, n_blocks - 1):
        assert all(np.array_equal(a, b) for a, b in zip(row_structure(blk), interior))

    exact = lax.Precision.HIGHEST
    by_col = jnp.einsum('hrc,qkc->hrqk', rpb.astype(F32), jnp.asarray(col_sel), precision=exact)
    tables = []
    for blk in kinds:
        row_valid, row_sel = row_structure(blk)
        dense = jnp.einsum('jlr,hrqk->hjqlk', jnp.asarray(row_sel), by_col, precision=exact)
        valid = row_valid[:, None, :, None] & col_valid[None, :, None, :]
        dense = jnp.where(jnp.asarray(valid)[None], dense, MASKED)
        tables.append(dense.reshape(NA_HEADS, Q_BLOCK, KEY_ROWS * GRID_W))
    return jnp.stack(tables)


def _attn_call(qr, qp, kr, v, bias_tables, *, batch, seq, ctx_len, with_ctx_queries):
    assert ctx_len == Q_BLOCK and seq % Q_BLOCK == 0
    grid_rows = seq // GRID_W
    n_blocks = grid_rows // Q_ROWS
    n_steps = n_blocks + (1 if with_ctx_queries else 0)
    ctx_block0 = batch * n_blocks
    n_out = batch * seq + (batch * ctx_len if with_ctx_queries else 0)
    n_keys = KEY_ROWS * GRID_W

    def q_map(b, h, i):
        return (jnp.where(i < n_blocks, b * n_blocks + i, ctx_block0 + b), h)

    def kind_map(b, h, i):
        return (jnp.where(i == 0, 0, jnp.where(i >= n_blocks - 1, 2, 1)), h, 0, 0)

    return pl.pallas_call(
        functools.partial(_attn_body, n_blocks=n_blocks, grid_rows=grid_rows),
        grid=(batch, NA_HEADS, n_steps),
        in_specs=[pl.BlockSpec((Q_BLOCK, HEAD_DIM), q_map),
                  pl.BlockSpec((Q_BLOCK, HEAD_DIM), q_map),
                  pl.BlockSpec((seq, HEAD_DIM), lambda b, h, i: (b, h)),
                  pl.BlockSpec((seq, HEAD_DIM), lambda b, h, i: (b, h)),
                  pl.BlockSpec((ctx_len, HEAD_DIM), lambda b, h, i: (ctx_block0 + b, h)),
                  pl.BlockSpec((ctx_len, HEAD_DIM), lambda b, h, i: (ctx_block0 + b, h)),
                  pl.BlockSpec((1, 1, Q_BLOCK, n_keys), kind_map)],
        out_specs=pl.BlockSpec((Q_BLOCK, HEAD_DIM), q_map),
        out_shape=jax.ShapeDtypeStruct((n_out, NA_HEADS * HEAD_DIM), BF16),
        compiler_params=_params(3),
    )(qr, qp, kr, v, kr, v, bias_tables)


def _merge_body(h_ref, a_ref, o_ref, wg_ref, bg_ref, wa_ref, wb_ref, m_ref, *, tn):
    d = m_ref.shape[1]
    hb, ab, ob = h_ref[...], a_ref[...], o_ref[...]
    for n0 in range(0, d, tn):
        cols = slice(n0, n0 + tn)
        gcols = slice(d + n0, d + n0 + tn)
        g_a = jax.nn.sigmoid(jnp.dot(hb, wg_ref[:, cols], preferred_element_type=F32) + bg_ref[:, cols])
        g_b = jax.nn.sigmoid(jnp.dot(hb, wg_ref[:, gcols], preferred_element_type=F32) + bg_ref[:, gcols])
        pa = jnp.dot(ab, wa_ref[:, cols], preferred_element_type=F32)
        pb = jnp.dot(ob, wb_ref[:, cols], preferred_element_type=F32)
        m_ref[:, cols] = (g_a * pa + g_b * pb).astype(BF16)


def _merge_call(h, a, o, w_gate_b, b_gate, w_a_b, w_b_b, *, n_rows, tm):
    d = h.shape[1]

    def row_map(i):
        return (i, 0)

    return pl.pallas_call(
        functools.partial(_merge_body, tn=min(d, 512)),
        grid=(n_rows // tm,),
        in_specs=[pl.BlockSpec((tm, d), row_map),
                  pl.BlockSpec((tm, a.shape[1]), row_map),
                  pl.BlockSpec((tm, o.shape[1]), row_map),
                  _resident(w_gate_b.shape), _resident((1, 2 * d)),
                  _resident(w_a_b.shape), _resident(w_b_b.shape)],
        out_specs=pl.BlockSpec((tm, d), row_map),
        out_shape=jax.ShapeDtypeStruct((n_rows, d), BF16),
        compiler_params=_params(1),
    )(h, a, o, w_gate_b, b_gate.reshape(1, 2 * d), w_a_b, w_b_b)


def _outproj_body(*refs, n_main_tiles, two_src, n_experts):
    if two_src:
        xa_ref, xb_ref = refs[:2]
        refs = refs[2:]
    else:
        xa_ref = refs[0]
        refs = refs[1:]
    m_ref, wo_ref, mod_ref, nw_ref, wr_ref, br_ref, xo_ref, hp_ref, route_ref, cnt_ref, carry_ref = refs
    i = pl.program_id(0)
    if two_src:
        x = jnp.where(i < n_main_tiles, xa_ref[...], xb_ref[...])
    else:
        x = xa_ref[...]
    tm, d = x.shape

    @pl.when(i == 0)
    def _():
        carry_ref[...] = jnp.zeros_like(carry_ref)

    y = jnp.dot(m_ref[...], wo_ref[...], preferred_element_type=F32)
    x_new = x + mod_ref[0, 2:3, :] * y
    xo_ref[...] = x_new
    h = _rms_modulate(x_new, nw_ref[...], mod_ref[0, 3:4, :], mod_ref[0, 4:5, :])
    hp_ref[...] = _pack_bf16_pair(h[:, :d // 2], h[:, d // 2:])

    h_hi = h.astype(BF16)
    h_lo = (h - h_hi.astype(F32)).astype(BF16)
    by_hi = jnp.dot(h_hi, wr_ref[...], preferred_element_type=F32)
    logits = (by_hi[:, :LANES] + by_hi[:, LANES:]
              + jnp.dot(h_lo, wr_ref[:, :LANES], preferred_element_type=F32) + br_ref[...])
    lane = lax.broadcasted_iota(jnp.int32, logits.shape, 1)
    lane_f = lane.astype(F32)
    far = jnp.float32(4 * LANES)

    def first_argmax(vals):
        top = jnp.max(vals, axis=-1, keepdims=True)
        return top, jnp.min(jnp.where(vals == top, lane_f, far), axis=-1, keepdims=True)

    g_logits = jnp.where((lane >= n_experts) & (lane < n_experts + N_GROUPS), logits, MASKED)
    g_top, g_lane = first_argmax(g_logits)
    g_prob = 1.0 / jnp.sum(jnp.exp(g_logits - g_top), axis=-1, keepdims=True)
    e_lo = (g_lane - n_experts) * EXPERTS_PER_GROUP
    e_logits = jnp.where((lane_f >= e_lo) & (lane_f < e_lo + EXPERTS_PER_GROUP), logits, MASKED)
    top1, e1 = first_argmax(e_logits)
    top2, e2 = first_argmax(jnp.where(lane_f == e1, MASKED, e_logits))
    t = jnp.exp(top2 - top1)
    w1 = g_prob / (1.0 + t)
    w2 = g_prob * t / (1.0 + t)

    sel1, sel2 = lane_f == e1, lane_f == e2
    onehot = jnp.where(sel1 | sel2, 1.0, 0.0)
    r_i = lax.broadcasted_iota(jnp.int32, (tm, tm), 0)
    c_i = lax.broadcasted_iota(jnp.int32, (tm, tm), 1)
    earlier = jnp.where(c_i < r_i, 1.0, 0.0).astype(BF16)
    before = jnp.dot(earlier, onehot.astype(BF16), preferred_element_type=F32) + carry_ref[...]
    rank1 = jnp.sum(jnp.where(sel1, before, 0.0), axis=-1, keepdims=True)
    rank2 = jnp.sum(jnp.where(sel2, before, 0.0), axis=-1, keepdims=True)
    carry_ref[...] += jnp.sum(onehot, axis=0, keepdims=True)
    cnt_ref[...] = carry_ref[...]

    route = jnp.zeros_like(logits)
    for k, val in enumerate((e1, e2, w1, w2, rank1, rank2)):
        route = jnp.where(lane == k, val, route)
    route_ref[...] = route


def _outproj_call(x_main, x_ctx, m, w_out_b, mod, norm_w, w_router, b_router, *, rows_per_mod, tm, n_experts):
    d = x_main.shape[1]
    two_src = x_ctx is not None
    n_main = x_main.shape[0] if two_src else m.shape[0]
    n_rows = m.shape[0]
    n_main_tiles = n_main // tm
    n_mod = mod.shape[0]
    tiles_per_mod = rows_per_mod // tm

    def row_map(i):
        return (i, 0)

    x_specs = [pl.BlockSpec((tm, d), lambda i: (jnp.minimum(i, n_main_tiles - 1), 0))]
    x_args = [x_main]
    if two_src:
        x_specs.append(pl.BlockSpec((tm, d), lambda i: (jnp.maximum(i - n_main_tiles, 0), 0)))
        x_args.append(x_ctx)
    in_specs = x_specs + [
        pl.BlockSpec((tm, d), row_map),
        _resident(w_out_b.shape),
        pl.BlockSpec((1, 6, d), lambda i: (jnp.minimum(i // tiles_per_mod, n_mod - 1), 0, 0)),
        _resident((1, d)),
        _resident(w_router.shape),
        _resident((1, LANES)),
    ]
    return pl.pallas_call(
        functools.partial(_outproj_body, n_main_tiles=n_main_tiles, two_src=two_src, n_experts=n_experts),
        grid=(n_rows // tm,),
        in_specs=in_specs,
        out_specs=[pl.BlockSpec((tm, d), row_map),
                   pl.BlockSpec((tm, d // 2), row_map),
                   pl.BlockSpec((tm, LANES), row_map),
                   pl.BlockSpec((1, LANES), lambda i: (0, 0))],
        out_shape=[jax.ShapeDtypeStruct((n_rows, d), F32),
                   jax.ShapeDtypeStruct((n_rows, d // 2), U32),
                   jax.ShapeDtypeStruct((n_rows, LANES), F32),
                   jax.ShapeDtypeStruct((1, LANES), F32)],
        scratch_shapes=[pltpu.VMEM((1, LANES), F32)],
        compiler_params=_params(1),
    )(*x_args, m, w_out_b, mod, norm_w.reshape(1, d), w_router, b_router)


def _row_copy(src_ref, src_row, dst_ref, dst_row, sem):
    return pltpu.make_async_copy(src_ref.at[pl.ds(src_row, 1), :], dst_ref.at[pl.ds(dst_row, 1), :], sem)


def _dispatch_body(p1_ref, p2_ref, h_ref, slots_in_ref, slots_ref, sem, *, tm):
    del slots_in_ref
    base = pl.program_id(0) * tm

    def issue(r, carry):
        _row_copy(h_ref, r, slots_ref, p1_ref[base + r], sem).start()
        _row_copy(h_ref, r, slots_ref, p2_ref[base + r], sem).start()
        return carry

    def drain(r, carry):
        _row_copy(h_ref, 0, slots_ref, 0, sem).wait()
        _row_copy(h_ref, 0, slots_ref, 0, sem).wait()
        return carry

    lax.fori_loop(0, tm, issue, 0)
    lax.fori_loop(0, tm, drain, 0)


def _dispatch_call(pos1, pos2, h_packed, n_slots, *, tm):
    n_rows, half = h_packed.shape
    return pl.pallas_call(
        functools.partial(_dispatch_body, tm=tm),
        grid_spec=pltpu.PrefetchScalarGridSpec(
            num_scalar_prefetch=2,
            grid=(n_rows // tm,),
            in_specs=[pl.BlockSpec((tm, half), lambda i, p1, p2: (i, 0)),
                      pl.BlockSpec(memory_space=pl.ANY)],
            out_specs=pl.BlockSpec(memory_space=pl.ANY),
            scratch_shapes=[pltpu.SemaphoreType.DMA(())],
        ),
        out_shape=jax.ShapeDtypeStruct((n_slots, half), U32),
        input_output_aliases={3: 0},
        compiler_params=_params(1),
    )(pos1, pos2, h_packed, jnp.zeros((n_slots, half), U32))


def _ffn_body(te_ref, nu_ref, xs_ref, wg_ref, wu_ref, wd_ref, ys_ref, wgb_ref, wub_ref, wdb_ref):
    j = pl.program_id(0)
    half = xs_ref.shape[1]

    @pl.when((j == 0) | (te_ref[j] != te_ref[jnp.maximum(j - 1, 0)]))
    def _():
        wgb_ref[...] = wg_ref[0, 0].astype(BF16)
        wub_ref[...] = wu_ref[0, 0].astype(BF16)
        wdb_ref[...] = wd_ref[0, 0].astype(BF16)

    @pl.when(j < nu_ref[0])
    def _():
        lo, hi = _unpack_bf16_pair(xs_ref[...])
        lo, hi = lo.astype(BF16), hi.astype(BF16)

        def up(w_ref):
            return (jnp.dot(lo, w_ref[:half, :], preferred_element_type=F32)
                    + jnp.dot(hi, w_ref[half:, :], preferred_element_type=F32))

        g = up(wgb_ref)
        hid = (g * jax.nn.sigmoid(g) * up(wub_ref)).astype(BF16)
        y = jnp.dot(hid, wdb_ref[...], preferred_element_type=F32)
        ys_ref[...] = _pack_bf16_pair(y[:, :half], y[:, half:])

    @pl.when(j >= nu_ref[0])
    def _():
        ys_ref[...] = jnp.zeros_like(ys_ref)


def _ffn_call(tile_expert, n_used, slots, w_gate, w_up, w_down, *, layer, tm):
    n_slots, half = slots.shape
    _, _, d, de = w_gate.shape
    return pl.pallas_call(
        _ffn_body,
        grid_spec=pltpu.PrefetchScalarGridSpec(
            num_scalar_prefetch=2,
            grid=(n_slots // tm,),
            in_specs=[pl.BlockSpec((tm, half), lambda j, te, nu: (j, 0)),
                      pl.BlockSpec((1, 1, d, de), lambda j, te, nu: (layer, te[j], 0, 0)),
                      pl.BlockSpec((1, 1, d, de), lambda j, te, nu: (layer, te[j], 0, 0)),
                      pl.BlockSpec((1, 1, de, d), lambda j, te, nu: (layer, te[j], 0, 0))],
            out_specs=pl.BlockSpec((tm, half), lambda j, te, nu: (j, 0)),
            scratch_shapes=[pltpu.VMEM((d, de), BF16), pltpu.VMEM((d, de), BF16), pltpu.VMEM((de, d), BF16)],
        ),
        out_shape=jax.ShapeDtypeStruct((n_slots, half), U32),
        compiler_params=_params(1),
    )(tile_expert, n_used, slots, w_gate, w_up, w_down)


def _combine_body(p1_ref, p2_ref, x_ref, route_ref, mod_ref, fw_ref, ys_ref, o_ref, y1_ref, y2_ref, sem,
                  *, tm, final_norm):
    base = pl.program_id(0) * tm

    def issue(r, carry):
        _row_copy(ys_ref, p1_ref[base + r], y1_ref, r, sem).start()
        _row_copy(ys_ref, p2_ref[base + r], y2_ref, r, sem).start()
        return carry

    def drain(r, carry):
        _row_copy(ys_ref, 0, y1_ref, 0, sem).wait()
        _row_copy(ys_ref, 0, y2_ref, 0, sem).wait()
        return carry

    lax.fori_loop(0, tm, issue, 0)
    lax.fori_loop(0, tm, drain, 0)
    w1, w2 = route_ref[:, 2:3], route_ref[:, 3:4]
    lo1, hi1 = _unpack_bf16_pair(y1_ref[...])
    lo2, hi2 = _unpack_bf16_pair(y2_ref[...])
    y = jnp.concatenate([w1 * lo1 + w2 * lo2, w1 * hi1 + w2 * hi2], axis=-1)
    x_new = x_ref[...] + mod_ref[0, 5:6, :] * y
    if final_norm:
        x_new = x_new * lax.rsqrt(jnp.mean(x_new * x_new, axis=-1, keepdims=True) + RMS_EPS) * fw_ref[...]
    o_ref[...] = x_new


def _combine_call(pos1, pos2, x, route, mod, final_w, ys, *, rows_per_mod, tm, final_norm):
    n_rows, d = x.shape
    half = ys.shape[1]
    n_mod = mod.shape[0]
    tiles_per_mod = rows_per_mod // tm
    return pl.pallas_call(
        functools.partial(_combine_body, tm=tm, final_norm=final_norm),
        grid_spec=pltpu.PrefetchScalarGridSpec(
            num_scalar_prefetch=2,
            grid=(n_rows // tm,),
            in_specs=[pl.BlockSpec((tm, d), lambda i, p1, p2: (i, 0)),
                      pl.BlockSpec((tm, LANES), lambda i, p1, p2: (i, 0)),
                      pl.BlockSpec((1, 6, d), lambda i, p1, p2: (jnp.minimum(i // tiles_per_mod, n_mod - 1), 0, 0)),
                      pl.BlockSpec((1, d), lambda i, p1, p2: (0, 0)),
                      pl.BlockSpec(memory_space=pl.ANY)],
            out_specs=pl.BlockSpec((tm, d), lambda i, p1, p2: (i, 0)),
            scratch_shapes=[pltpu.VMEM((tm, half), U32), pltpu.VMEM((tm, half), U32),
                            pltpu.SemaphoreType.DMA(())],
        ),
        out_shape=jax.ShapeDtypeStruct((n_rows, d), F32),
        compiler_params=_params(1),
    )(pos1, pos2, x, route, mod, final_w.reshape(1, d), ys)


def _slot_plan(route, counts, n_experts, tm):
    n_rows = route.shape[0]
    n_tiles = (2 * n_rows) // tm + n_experts
    cnt = counts[0, :n_experts].astype(jnp.int32)
    padded = ((cnt + tm - 1) // tm) * tm
    ends = jnp.cumsum(padded)
    starts = ends - padded
    e1 = route[:, 0].astype(jnp.int32)
    e2 = route[:, 1].astype(jnp.int32)
    pos1 = starts[e1] + route[:, 4].astype(jnp.int32)
    pos2 = starts[e2] + route[:, 5].astype(jnp.int32)
    n_used = ends[-1] // tm
    tile_start = jnp.arange(n_tiles, dtype=jnp.int32) * tm
    tile_expert = jnp.sum((ends[None, :] <= tile_start[:, None]).astype(jnp.int32), axis=1)
    last_used = tile_expert[jnp.maximum(n_used - 1, 0)]
    tile_expert = jnp.where(tile_start < ends[-1], tile_expert, last_used)
    tile_expert = jnp.minimum(tile_expert, n_experts - 1).astype(jnp.int32)
    return pos1, pos2, tile_expert, n_used.reshape(1).astype(jnp.int32), n_tiles * tm


def _rope_tables(batch, seq, ctx_rows):
    pairs = HEAD_DIM // 4
    t = np.arange(seq)
    pos = np.stack([t // GRID_W, t % GRID_W], axis=-1).astype(np.float32)
    inv_freq = (ROPE_THETA ** (-np.arange(pairs, dtype=np.float32) / pairs)).astype(np.float32)
    ang = pos[:, :, None] * inv_freq
    cos = np.repeat(np.cos(ang)[:, :, None, :], 2, axis=2).reshape(seq, HEAD_DIM)
    sin = np.sin(ang)
    zero = np.zeros_like(sin)
    sa = np.stack([-sin, zero], axis=2).reshape(seq, HEAD_DIM)
    sb = np.stack([zero, sin], axis=2).reshape(seq, HEAD_DIM)

    def full(tab, fill):
        return jnp.asarray(np.concatenate([np.tile(tab, (batch, 1)),
                                           np.full((ctx_rows, HEAD_DIM), fill, np.float32)]), F32)

    return full(cos, 1.0), full(sa, 0.0), full(sb, 0.0)


def kernel(x, c, ctx, c_ctx, w_ada, b_ada, norm_mix_w, norm_ffn_w, w_in, sgu_norm_w, sgu_w_s, sgu_b_s, na_rpb,
           w_merge_gate, b_merge_gate, w_branch_a, w_branch_b, w_out, w_router_group, b_router_group,
           w_router_expert, b_router_expert, w_exp_gate, w_exp_up, w_exp_down, final_norm_w):
    batch, seq, d = x.shape
    ctx_len = ctx.shape[1]
    depth = w_ada.shape[0]
    n_experts = w_exp_gate.shape[1]
    n_lat = batch * seq
    n_ctx = batch * ctx_len
    tm = 256
    tm_wide = 512
    assert seq % tm_wide == 0 and n_ctx % tm_wide == 0 and tm % SGU_CHUNK == 0 and ctx_len % SGU_CHUNK == 0

    mods = _ada_call(jnp.concatenate([c, c_ctx[None]], axis=0), w_ada, b_ada)
    cos, sa, sb = _rope_tables(batch, seq, n_ctx)
    grid_rows = seq // GRID_W

    x_cur = x.reshape(n_lat, d)
    ctx_rows = ctx.reshape(n_ctx, d)
    x_all = None
    for l in range(depth):
        last = l == depth - 1
        mod = mods[l, :batch + 1].reshape(batch + 1, 6, d)
        w_in_b = w_in[l].astype(BF16)
        w_s_b = sgu_w_s[l].astype(BF16)
        b_s_t = sgu_b_s[l].T
        if x_all is None:
            src = (x_cur, ctx_rows)
        else:
            src = (x_all, None)
        h, a, qp, qr, kr, v = _inproj_call(src[0], src[1], mod, norm_mix_w[l], w_in_b, sgu_norm_w[l], w_s_b, b_s_t,
                                           cos, sa, sb, rows_per_mod=seq, tm=tm)
        o = _attn_call(qr, qp, kr, v, _attn_bias_tables(na_rpb[l], grid_rows), batch=batch, seq=seq,
                       ctx_len=ctx_len, with_ctx_queries=not last)
        n_rows = n_lat if last else n_lat + n_ctx
        m = _merge_call(h, a, o, w_merge_gate[l].astype(BF16), b_merge_gate[l], w_branch_a[l].astype(BF16),
                        w_branch_b[l].astype(BF16), n_rows=n_rows, tm=tm_wide)
        w_router = jnp.zeros((d, LANES), F32)
        w_router = w_router.at[:, :n_experts].set(w_router_expert[l])
        w_router = w_router.at[:, n_experts:n_experts + N_GROUPS].set(w_router_group[l])
        w_router_hi = w_router.astype(BF16)
        w_router_lo = (w_router - w_router_hi.astype(F32)).astype(BF16)
        w_router = jnp.concatenate([w_router_hi, w_router_lo], axis=1)
        b_router = jnp.zeros((1, LANES), F32)
        b_router = b_router.at[0, :n_experts].set(b_router_expert[l])
        b_router = b_router.at[0, n_experts:n_experts + N_GROUPS].set(b_router_group[l])
        x_mid, h_packed, route, counts = _outproj_call(src[0], src[1], m, w_out[l].astype(BF16), mod, norm_ffn_w[l],
                                                       w_router, b_router, rows_per_mod=seq, tm=tm_wide,
                                                       n_experts=n_experts)
        pos1, pos2, tile_expert, n_used, n_slots = _slot_plan(route, counts, n_experts, tm)
        slots = _dispatch_call(pos1, pos2, h_packed, n_slots, tm=tm)
        ys = _ffn_call(tile_expert, n_used, slots, w_exp_gate, w_exp_up, w_exp_down, layer=l, tm=tm)
        x_all = _combine_call(pos1, pos2, x_mid, route, mod, final_norm_w, ys, rows_per_mod=seq, tm=tm,
                              final_norm=last)
    return x_all.reshape(batch, seq, d)
```

```python
import functools

import numpy as np
import jax
import jax.numpy as jnp
from jax import lax
from jax.experimental import pallas as pl
from jax.experimental.pallas import tpu as pltpu

GRID_W = 64
SGU_CHUNK = 128
SGU_GROUPS = 8
NA_HEADS = 8
HEAD_DIM = 128
WIN_ROWS = 8
WIN_COLS = 16
ROPE_THETA = 10000.0
N_GROUPS = 4
EXPERTS_PER_GROUP = 8
RMS_EPS = 1e-6

LANES = 128
Q_ROWS = 4
Q_BLOCK = Q_ROWS * GRID_W
KEY_ROWS = Q_ROWS + WIN_ROWS - 1
MASKED = -1e30
VMEM_LIMIT = 56 * 1024 * 1024

BF16 = jnp.bfloat16
F32 = jnp.float32
U32 = jnp.uint32


def _params(n_grid_dims, vmem=VMEM_LIMIT):
    return pltpu.CompilerParams(dimension_semantics=("arbitrary",) * n_grid_dims, vmem_limit_bytes=vmem)


def _resident(shape):
    nd = len(shape)
    return pl.BlockSpec(shape, lambda *_: (0,) * nd, pipeline_mode=pl.Buffered(1))


def _pack_bf16_pair(lo, hi):
    lo_bits = lax.bitcast_convert_type(lo.astype(BF16).astype(F32), U32)
    hi_bits = lax.bitcast_convert_type(hi.astype(BF16).astype(F32), U32)
    return (hi_bits & jnp.uint32(0xFFFF0000)) | (lo_bits >> 16)


def _unpack_bf16_pair(w):
    lo = lax.bitcast_convert_type(w << 16, F32)
    hi = lax.bitcast_convert_type(w & jnp.uint32(0xFFFF0000), F32)
    return lo, hi


def _rms_modulate(x, norm_w, shift, scale):
    y = x * lax.rsqrt(jnp.mean(x * x, axis=-1, keepdims=True) + RMS_EPS) * norm_w
    return y * (1.0 + scale) + shift


def _ada_body(ct_ref, w_ref, b_ref, o_ref, *, n_rows):
    s = ct_ref[...]
    s = s * jax.nn.sigmoid(s)
    w = w_ref[0]
    o_ref[...] = jnp.zeros_like(o_ref)
    for r in range(n_rows):
        o_ref[0, r:r + 1, :] = jnp.sum(w * s[:, r:r + 1], axis=0, keepdims=True) + b_ref[0]


def _ada_call(cond, w_ada, b_ada):
    n_rows, d = cond.shape
    depth, _, n = w_ada.shape
    tn = min(n, 512)
    ct = jnp.zeros((d, 8), F32).at[:, :n_rows].set(cond.T)
    return pl.pallas_call(
        functools.partial(_ada_body, n_rows=n_rows),
        grid=(depth, n // tn),
        in_specs=[pl.BlockSpec((d, 8), lambda l, j: (0, 0)),
                  pl.BlockSpec((1, d, tn), lambda l, j: (l, 0, j)),
                  pl.BlockSpec((1, 1, tn), lambda l, j: (l, 0, j))],
        out_specs=pl.BlockSpec((1, 8, tn), lambda l, j: (l, 0, j)),
        out_shape=jax.ShapeDtypeStruct((depth, 8, n), F32),
        compiler_params=_params(2),
    )(ct, w_ada, b_ada.reshape(depth, 1, n))


def _inproj_body(*refs, n_main_tiles, two_src, sgu_w, na_w):
    if two_src:
        xa_ref, xb_ref = refs[:2]
        refs = refs[2:]
    else:
        xa_ref = refs[0]
        refs = refs[1:]
    (mod_ref, nw_ref, w_ref, snw_ref, ws_ref, bst_ref, cos_ref, sa_ref, sb_ref,
     h_ref, a_ref, qp_ref, qr_ref, kr_ref, v_ref) = refs
    if two_src:
        x = jnp.where(pl.program_id(0) < n_main_tiles, xa_ref[...], xb_ref[...])
    else:
        x = xa_ref[...]
    tm = x.shape[0]
    h = _rms_modulate(x, nw_ref[...], mod_ref[0, 0:1, :], mod_ref[0, 1:2, :])
    hb = h.astype(BF16)
    h_ref[...] = hb

    def proj(lo, width):
        return jnp.dot(hb, w_ref[:, lo:lo + width], preferred_element_type=F32)

    u = jax.nn.gelu(proj(0, sgu_w))
    v = jax.nn.gelu(proj(sgu_w, sgu_w))
    vn = v * lax.rsqrt(jnp.mean(v * v, axis=-1, keepdims=True) + RMS_EPS) * snw_ref[...]
    vnb = vn.astype(BF16)
    gch = sgu_w // SGU_GROUPS
    for c in range(tm // SGU_CHUNK):
        rows = slice(c * SGU_CHUNK, (c + 1) * SGU_CHUNK)
        for g in range(SGU_GROUPS):
            cols = slice(g * gch, (g + 1) * gch)
            z = jnp.dot(ws_ref[g], vnb[rows, cols], preferred_element_type=F32) + bst_ref[:, g:g + 1]
            a_ref[rows, cols] = (u[rows, cols] * z).astype(BF16)

    cos, sa, sb = cos_ref[...], sa_ref[...], sb_ref[...]

    def rope_into(p, out_ref):
        for hh in range(NA_HEADS):
            cols = slice(hh * HEAD_DIM, (hh + 1) * HEAD_DIM)
            xh = p[:, cols]
            out_ref[:, cols] = (xh * cos + pltpu.roll(xh, HEAD_DIM - 32, 1) * sa
                                + pltpu.roll(xh, 32, 1) * sb).astype(BF16)

    q = proj(2 * sgu_w, na_w)
    qp_ref[...] = q.astype(BF16)
    rope_into(q, qr_ref)
    rope_into(proj(2 * sgu_w + na_w, na_w), kr_ref)
    v_ref[...] = proj(2 * sgu_w + 2 * na_w, na_w).astype(BF16)


def _inproj_call(x_main, x_ctx, mod, norm_w, w_in_b, sgu_norm_w, w_s_b, b_s_t, cos, sa, sb,
                 *, rows_per_mod, tm):
    d = x_main.shape[1]
    n_main = x_main.shape[0]
    two_src = x_ctx is not None
    tt = n_main + (x_ctx.shape[0] if two_src else 0)
    n_main_tiles = n_main // tm
    n_mod = mod.shape[0]
    sgu_w = sgu_norm_w.shape[-1]
    na_w = NA_HEADS * HEAD_DIM
    tiles_per_mod = rows_per_mod // tm

    def row_map(i):
        return (i, 0)

    x_specs = [pl.BlockSpec((tm, d), lambda i: (jnp.minimum(i, n_main_tiles - 1), 0))]
    x_args = [x_main]
    if two_src:
        x_specs.append(pl.BlockSpec((tm, d), lambda i: (jnp.maximum(i - n_main_tiles, 0), 0)))
        x_args.append(x_ctx)
    in_specs = x_specs + [
        pl.BlockSpec((1, 6, d), lambda i: (jnp.minimum(i // tiles_per_mod, n_mod - 1), 0, 0)),
        _resident((1, d)),
        _resident(w_in_b.shape),
        _resident((1, sgu_w)),
        _resident(w_s_b.shape),
        _resident(b_s_t.shape),
        pl.BlockSpec((tm, HEAD_DIM), row_map),
        pl.BlockSpec((tm, HEAD_DIM), row_map),
        pl.BlockSpec((tm, HEAD_DIM), row_map),
    ]
    out_widths = [d, sgu_w, na_w, na_w, na_w, na_w]
    return pl.pallas_call(
        functools.partial(_inproj_body, n_main_tiles=n_main_tiles, two_src=two_src, sgu_w=sgu_w, na_w=na_w),
        grid=(tt // tm,),
        in_specs=in_specs,
        out_specs=[pl.BlockSpec((tm, w), row_map) for w in out_widths],
        out_shape=[jax.ShapeDtypeStruct((tt, w), BF16) for w in out_widths],
        compiler_params=_params(1),
    )(*x_args, mod, norm_w.reshape(1, d), w_in_b, sgu_norm_w.reshape(1, sgu_w), w_s_b, b_s_t, cos, sa, sb)


def _softmax_pv(scores, values):
    m = functools.reduce(jnp.maximum, [jnp.max(s, axis=-1, keepdims=True) for s in scores])
    ps = [jnp.exp(s - m) for s in scores]
    denom = functools.reduce(jnp.add, [jnp.sum(p, axis=-1, keepdims=True) for p in ps])
    acc = functools.reduce(jnp.add, [jnp.dot(p.astype(BF16), v, preferred_element_type=F32)
                                     for p, v in zip(ps, values)])
    return acc / denom


def _qk(q, k):
    return lax.dot_general(q, k, (((1,), (1,)), ((), ())), preferred_element_type=F32) * (HEAD_DIM ** -0.5)


def _attn_body(qr_ref, qp_ref, k_ref, v_ref, kc_ref, vc_ref, bias_ref, o_ref, *, n_blocks, grid_rows):
    i = pl.program_id(2)
    n_keys = KEY_ROWS * GRID_W

    @pl.when(i < n_blocks)
    def _():
        k_row0 = jnp.clip(i * Q_ROWS - WIN_ROWS // 2, 0, grid_rows - KEY_ROWS)
        start = pl.multiple_of(k_row0 * GRID_W, GRID_W)
        k_loc = k_ref[pl.ds(start, n_keys), :]
        v_loc = v_ref[pl.ds(start, n_keys), :]
        s_loc = _qk(qr_ref[...], k_loc) + bias_ref[0, 0]
        s_ctx = _qk(qp_ref[...], kc_ref[...])
        o_ref[...] = _softmax_pv([s_loc, s_ctx], [v_loc, vc_ref[...]]).astype(BF16)

    @pl.when(i >= n_blocks)
    def _():
        s_ctx = _qk(qp_ref[...], kc_ref[...])
        o_ref[...] = _softmax_pv([s_ctx], [vc_ref[...]]).astype(BF16)


def _attn_bias_tables(rpb, grid_rows):
    n_blocks = grid_rows // Q_ROWS
    wr = min(WIN_ROWS, grid_rows)
    assert wr == WIN_ROWS and grid_rows >= KEY_ROWS and grid_rows % Q_ROWS == 0
    n_ri, n_ci = 2 * WIN_ROWS - 1, 2 * WIN_COLS - 1

    qc = np.arange(GRID_W)
    c0 = np.clip(qc - WIN_COLS // 2, 0, GRID_W - WIN_COLS)
    col_valid = (qc[None, :] >= c0[:, None]) & (qc[None, :] < c0[:, None] + WIN_COLS)
    col_sel = (col_valid[:, :, None]
               & ((qc[None, :, None] - qc[:, None, None] + (WIN_COLS - 1)) == np.arange(n_ci))).astype(np.float32)

    def row_structure(blk):
        k_row0 = int(np.clip(blk * Q_ROWS - WIN_ROWS // 2, 0, grid_rows - KEY_ROWS))
        qr = blk * Q_ROWS + np.arange(Q_ROWS)
        kr = k_row0 + np.arange(KEY_ROWS)
        r0 = np.clip(qr - wr // 2, 0, grid_rows - wr)
        valid = (kr[None, :] >= r0[:, None]) & (kr[None, :] < r0[:, None] + wr)
        sel = valid[:, :, None] & ((kr[None, :, None] - qr[:, None, None] + (WIN_ROWS - 1)) == np.arange(n_ri))
        return valid, sel.astype(np.float32)

    kinds = [0, min(1, n_blocks - 1), n_blocks - 1]
    interior = row_structure(kinds[1])
    for blk in range(1, n_blocks - 1):
        assert all(np.array_equal(a, b) for a, b in zip(row_structure(blk), interior))

    exact = lax.Precision.HIGHEST
    by_col = jnp.einsum('hrc,qkc->hrqk', rpb.astype(F32), jnp.asarray(col_sel), precision=exact)
    tables = []
    for blk in kinds:
        row_valid, row_sel = row_structure(blk)
        dense = jnp.einsum('jlr,hrqk->hjqlk', jnp.asarray(row_sel), by_col, precision=exact)
        valid = row_valid[:, None, :, None] & col_valid[None, :, None, :]
        dense = jnp.where(jnp.asarray(valid)[None], dense, MASKED)
        tables.append(dense.reshape(NA_HEADS, Q_BLOCK, KEY_ROWS * GRID_W))
    return jnp.stack(tables)


def _attn_call(qr, qp, kr, v, bias_tables, *, batch, seq, ctx_len, with_ctx_queries):
    assert ctx_len == Q_BLOCK and seq % Q_BLOCK == 0
    grid_rows = seq // GRID_W
    n_blocks = grid_rows // Q_ROWS
    n_steps = n_blocks + (1 if with_ctx_queries else 0)
    ctx_block0 = batch * n_blocks
    n_out = batch * seq + (batch * ctx_len if with_ctx_queries else 0)
    n_keys = KEY_ROWS * GRID_W

    def q_map(b, h, i):
        return (jnp.where(i < n_blocks, b * n_blocks + i, ctx_block0 + b), h)

    def kind_map(b, h, i):
        return (jnp.where(i == 0, 0, jnp.where(i >= n_blocks - 1, 2, 1)), h, 0, 0)

    return pl.pallas_call(
        functools.partial(_attn_body, n_blocks=n_blocks, grid_rows=grid_rows),
        grid=(batch, NA_HEADS, n_steps),
        in_specs=[pl.BlockSpec((Q_BLOCK, HEAD_DIM), q_map),
                  pl.BlockSpec((Q_BLOCK, HEAD_DIM), q_map),
                  pl.BlockSpec((seq, HEAD_DIM), lambda b, h, i: (b, h)),
                  pl.BlockSpec((seq, HEAD_DIM), lambda b, h, i: (b, h)),
                  pl.BlockSpec((ctx_len, HEAD_DIM), lambda b, h, i: (ctx_block0 + b, h)),
                  pl.BlockSpec((ctx_len, HEAD_DIM), lambda b, h, i: (ctx_block0 + b, h)),
                  pl.BlockSpec((1, 1, Q_BLOCK, n_keys), kind_map)],
        out_specs=pl.BlockSpec((Q_BLOCK, HEAD_DIM), q_map),
        out_shape=jax.ShapeDtypeStruct((n_out, NA_HEADS * HEAD_DIM), BF16),
        compiler_params=_params(3),
    )(qr, qp, kr, v, kr, v, bias_tables)


def _merge_body(h_ref, a_ref, o_ref, wg_ref, bg_ref, wa_ref, wb_ref, m_ref, *, tn):
    d = m_ref.shape[1]
    hb, ab, ob = h_ref[...], a_ref[...], o_ref[...]
    for n0 in range(0, d, tn):
        cols = slice(n0, n0 + tn)
        gcols = slice(d + n0, d + n0 + tn)
        g_a = jax.nn.sigmoid(jnp.dot(hb, wg_ref[:, cols], preferred_element_type=F32) + bg_ref[:, cols])
        g_b = jax.nn.sigmoid(jnp.dot(hb, wg_ref[:, gcols], preferred_element_type=F32) + bg_ref[:, gcols])
        pa = jnp.dot(ab, wa_ref[:, cols], preferred_element_type=F32)
        pb = jnp.dot(ob, wb_ref[:, cols], preferred_element_type=F32)
        m_ref[:, cols] = (g_a * pa + g_b * pb).astype(BF16)


def _merge_call(h, a, o, w_gate_b, b_gate, w_a_b, w_b_b, *, n_rows, tm):
    d = h.shape[1]

    def row_map(i):
        return (i, 0)

    return pl.pallas_call(
        functools.partial(_merge_body, tn=min(d, 512)),
        grid=(n_rows // tm,),
        in_specs=[pl.BlockSpec((tm, d), row_map),
                  pl.BlockSpec((tm, a.shape[1]), row_map),
                  pl.BlockSpec((tm, o.shape[1]), row_map),
                  _resident(w_gate_b.shape), _resident((1, 2 * d)),
                  _resident(w_a_b.shape), _resident(w_b_b.shape)],
        out_specs=pl.BlockSpec((tm, d), row_map),
        out_shape=jax.ShapeDtypeStruct((n_rows, d), BF16),
        compiler_params=_params(1),
    )(h, a, o, w_gate_b, b_gate.reshape(1, 2 * d), w_a_b, w_b_b)


def _outproj_body(*refs, n_main_tiles, two_src, n_experts):
    if two_src:
        xa_ref, xb_ref = refs[:2]
        refs = refs[2:]
    else:
        xa_ref = refs[0]
        refs = refs[1:]
    m_ref, wo_ref, mod_ref, nw_ref, wr_ref, br_ref, xo_ref, hp_ref, route_ref, cnt_ref, carry_ref = refs
    i = pl.program_id(0)
    if two_src:
        x = jnp.where(i < n_main_tiles, xa_ref[...], xb_ref[...])
    else:
        x = xa_ref[...]
    tm, d = x.shape

    @pl.when(i == 0)
    def _():
        carry_ref[...] = jnp.zeros_like(carry_ref)

    y = jnp.dot(m_ref[...], wo_ref[...], preferred_element_type=F32)
    x_new = x + mod_ref[0, 2:3, :] * y
    xo_ref[...] = x_new
    h = _rms_modulate(x_new, nw_ref[...], mod_ref[0, 3:4, :], mod_ref[0, 4:5, :])
    hp_ref[...] = _pack_bf16_pair(h[:, :d // 2], h[:, d // 2:])

    h_hi = h.astype(BF16)
    h_lo = (h - h_hi.astype(F32)).astype(BF16)
    by_hi = jnp.dot(h_hi, wr_ref[...], preferred_element_type=F32)
    logits = (by_hi[:, :LANES] + by_hi[:, LANES:]
              + jnp.dot(h_lo, wr_ref[:, :LANES], preferred_element_type=F32) + br_ref[...])
    lane = lax.broadcasted_iota(jnp.int32, logits.shape, 1)
    lane_f = lane.astype(F32)
    far = jnp.float32(4 * LANES)

    def first_argmax(vals):
        top = jnp.max(vals, axis=-1, keepdims=True)
        return top, jnp.min(jnp.where(vals == top, lane_f, far), axis=-1, keepdims=True)

    g_logits = jnp.where((lane >= n_experts) & (lane < n_experts + N_GROUPS), logits, MASKED)
    g_top, g_lane = first_argmax(g_logits)
    g_prob = 1.0 / jnp.sum(jnp.exp(g_logits - g_top), axis=-1, keepdims=True)
    e_lo = (g_lane - n_experts) * EXPERTS_PER_GROUP
    e_logits = jnp.where((lane_f >= e_lo) & (lane_f < e_lo + EXPERTS_PER_GROUP), logits, MASKED)
    top1, e1 = first_argmax(e_logits)
    top2, e2 = first_argmax(jnp.where(lane_f == e1, MASKED, e_logits))
    t = jnp.exp(top2 - top1)
    w1 = g_prob / (1.0 + t)
    w2 = g_prob * t / (1.0 + t)

    sel1, sel2 = lane_f == e1, lane_f == e2
    onehot = jnp.where(sel1 | sel2, 1.0, 0.0)
    r_i = lax.broadcasted_iota(jnp.int32, (tm, tm), 0)
    c_i = lax.broadcasted_iota(jnp.int32, (tm, tm), 1)
    earlier = jnp.where(c_i < r_i, 1.0, 0.0).astype(BF16)
    before = jnp.dot(earlier, onehot.astype(BF16), preferred_element_type=F32) + carry_ref[...]
    rank1 = jnp.sum(jnp.where(sel1, before, 0.0), axis=-1, keepdims=True)
    rank2 = jnp.sum(jnp.where(sel2, before, 0.0), axis=-1, keepdims=True)
    carry_ref[...] += jnp.sum(onehot, axis=0, keepdims=True)
    cnt_ref[...] = carry_ref[...]

    route = jnp.zeros_like(logits)
    for k, val in enumerate((e1, e2, w1, w2, rank1, rank2)):
        route = jnp.where(lane == k, val, route)
    route_ref[...] = route


def _outproj_call(x_main, x_ctx, m, w_out_b, mod, norm_w, w_router, b_router, *, rows_per_mod, tm, n_experts):
    d = x_main.shape[1]
    two_src = x_ctx is not None
    n_main = x_main.shape[0] if two_src else m.shape[0]
    n_rows = m.shape[0]
    n_main_tiles = n_main // tm
    n_mod = mod.shape[0]
    tiles_per_mod = rows_per_mod // tm

    def row_map(i):
        return (i, 0)

    x_specs = [pl.BlockSpec((tm, d), lambda i: (jnp.minimum(i, n_main_tiles - 1), 0))]
    x_args = [x_main]
    if two_src:
        x_specs.append(pl.BlockSpec((tm, d), lambda i: (jnp.maximum(i - n_main_tiles, 0), 0)))
        x_args.append(x_ctx)
    in_specs = x_specs + [
        pl.BlockSpec((tm, d), row_map),
        _resident(w_out_b.shape),
        pl.BlockSpec((1, 6, d), lambda i: (jnp.minimum(i // tiles_per_mod, n_mod - 1), 0, 0)),
        _resident((1, d)),
        _resident(w_router.shape),
        _resident((1, LANES)),
    ]
    return pl.pallas_call(
        functools.partial(_outproj_body, n_main_tiles=n_main_tiles, two_src=two_src, n_experts=n_experts),
        grid=(n_rows // tm,),
        in_specs=in_specs,
        out_specs=[pl.BlockSpec((tm, d), row_map),
                   pl.BlockSpec((tm, d // 2), row_map),
                   pl.BlockSpec((tm, LANES), row_map),
                   pl.BlockSpec((1, LANES), lambda i: (0, 0))],
        out_shape=[jax.ShapeDtypeStruct((n_rows, d), F32),
                   jax.ShapeDtypeStruct((n_rows, d // 2), U32),
                   jax.ShapeDtypeStruct((n_rows, LANES), F32),
                   jax.ShapeDtypeStruct((1, LANES), F32)],
        scratch_shapes=[pltpu.VMEM((1, LANES), F32)],
        compiler_params=_params(1),
    )(*x_args, m, w_out_b, mod, norm_w.reshape(1, d), w_router, b_router)


def _row_copy(src_ref, src_row, dst_ref, dst_row, sem):
    return pltpu.make_async_copy(src_ref.at[pl.ds(src_row, 1), :], dst_ref.at[pl.ds(dst_row, 1), :], sem)


def _slot_rows_body(p1_ref, p2_ref, init_ref, dst_ref, stage_ref, sem, *, tokens_per_step, row_stride, lead):
    i = pl.program_id(0)

    @pl.when(i == 0)
    def _():
        load = pltpu.make_async_copy(init_ref, stage_ref, sem)
        load.start()
        load.wait()

    def put(r, carry):
        t = i * tokens_per_step + r
        s1 = p1_ref[t] + lead
        s2 = p2_ref[t] + lead
        stage_ref[s1 // LANES, s1 % LANES] = t
        stage_ref[s2 // LANES, s2 % LANES] = row_stride + t
        return carry

    lax.fori_loop(0, tokens_per_step, put, 0, unroll=8)

    @pl.when(i == pl.num_programs(0) - 1)
    def _():
        store = pltpu.make_async_copy(stage_ref, dst_ref, sem)
        store.start()
        store.wait()


def _slot_rows_call(pos1, pos2, init_rows, *, row_stride, lead, tokens_per_step):
    n_tok = pos1.shape[0]
    rows = init_rows.shape[0]
    return pl.pallas_call(
        functools.partial(_slot_rows_body, tokens_per_step=tokens_per_step, row_stride=row_stride, lead=lead),
        grid_spec=pltpu.PrefetchScalarGridSpec(
            num_scalar_prefetch=2,
            grid=(n_tok // tokens_per_step,),
            in_specs=[pl.BlockSpec(memory_space=pl.ANY)],
            out_specs=pl.BlockSpec(memory_space=pl.ANY),
            scratch_shapes=[pltpu.SMEM((rows, LANES), jnp.int32), pltpu.SemaphoreType.DMA(())],
        ),
        out_shape=jax.ShapeDtypeStruct((rows, LANES), jnp.int32),
        compiler_params=_params(1),
    )(pos1, pos2, init_rows)


def _ffn_body(te_ref, nu_ref, src_ref, dst_ref, h_ref, wg_ref, wu_ref, wd_ref, yk_ref,
              wgb_ref, wub_ref, wdb_ref, xbuf, ybuf, sem_g, sem_s, *, tm, spare_rows):
    j = pl.program_id(0)
    n_used = nu_ref[0]
    slot = j % 2
    other = 1 - slot
    half = xbuf.shape[2]

    def gather_row(tile, r, buf):
        return _row_copy(h_ref, src_ref[(tile + 1) * tm + r], xbuf.at[buf], r, sem_g.at[buf])

    def scatter_row(tile, r, buf):
        return _row_copy(ybuf.at[buf], r, yk_ref, dst_ref[(tile + 1) * tm + r], sem_s.at[buf])

    def wait_tile(buf_ref, sem):
        pltpu.make_async_copy(buf_ref, buf_ref, sem).wait()

    @pl.when(j == 0)
    def _():
        ybuf[...] = jnp.zeros_like(ybuf)

        def prime(r, carry):
            _row_copy(ybuf.at[0], r, yk_ref, spare_rows + r, sem_s.at[0]).start()
            gather_row(0, r, 0).start()
            return carry

        lax.fori_loop(0, tm, prime, 0)

    @pl.when((j == 0) | (te_ref[j] != te_ref[jnp.maximum(j - 1, 0)]))
    def _():
        wgb_ref[...] = wg_ref[0, 0].astype(BF16)
        wub_ref[...] = wu_ref[0, 0].astype(BF16)
        wdb_ref[...] = wd_ref[0, 0].astype(BF16)

    @pl.when(j < n_used)
    def _():
        wait_tile(xbuf.at[slot], sem_g.at[slot])
        lo, hi = _unpack_bf16_pair(xbuf[slot])
        lo, hi = lo.astype(BF16), hi.astype(BF16)
        wait_tile(ybuf.at[slot], sem_s.at[slot])
        for r in range(tm):
            gather_row(j + 1, r, other).start()
            scatter_row(j - 1, r, other).start()

        def up(w_ref):
            return (jnp.dot(lo, w_ref[:half, :], preferred_element_type=F32)
                    + jnp.dot(hi, w_ref[half:, :], preferred_element_type=F32))

        g = up(wgb_ref)
        hid = (g * jax.nn.sigmoid(g) * up(wub_ref)).astype(BF16)
        y = jnp.dot(hid, wdb_ref[...], preferred_element_type=F32)
        ybuf[slot] = _pack_bf16_pair(y[:, :half], y[:, half:])

    @pl.when(j == n_used - 1)
    def _():
        def flush(r, carry):
            scatter_row(j, r, slot).start()
            return carry

        wait_tile(ybuf.at[other], sem_s.at[other])
        lax.fori_loop(0, tm, flush, 0)
        wait_tile(ybuf.at[slot], sem_s.at[slot])
        wait_tile(xbuf.at[other], sem_g.at[other])


def _ffn_call(tile_expert, n_used, dst_rows, h_packed, w_gate, w_up, w_down, *, layer, tm, row_stride, n_out_rows):
    n_tok, half = h_packed.shape
    _, _, d, de = w_gate.shape
    n_tiles = tile_expert.shape[0]
    src_rows = jnp.where(dst_rows < 2 * row_stride, dst_rows % row_stride, 0)

    def w_map(j, te, nu, sr, dr):
        return (layer, te[j], 0, 0)

    return pl.pallas_call(
        functools.partial(_ffn_body, tm=tm, spare_rows=2 * row_stride),
        grid_spec=pltpu.PrefetchScalarGridSpec(
            num_scalar_prefetch=4,
            grid=(n_tiles,),
            in_specs=[pl.BlockSpec(memory_space=pl.ANY),
                      pl.BlockSpec((1, 1, d, de), w_map),
                      pl.BlockSpec((1, 1, d, de), w_map),
                      pl.BlockSpec((1, 1, de, d), w_map)],
            out_specs=pl.BlockSpec(memory_space=pl.ANY),
            scratch_shapes=[pltpu.VMEM((d, de), BF16), pltpu.VMEM((d, de), BF16), pltpu.VMEM((de, d), BF16),
                            pltpu.VMEM((2, tm, half), U32), pltpu.VMEM((2, tm, half), U32),
                            pltpu.SemaphoreType.DMA((2,)), pltpu.SemaphoreType.DMA((2,))],
        ),
        out_shape=jax.ShapeDtypeStruct((n_out_rows, half), U32),
        compiler_params=_params(1),
    )(tile_expert, n_used, src_rows, dst_rows, h_packed, w_gate, w_up, w_down)


def _combine_body(x_ref, y1_ref, y2_ref, route_ref, mod_ref, fw_ref, o_ref, *, final_norm):
    w1, w2 = route_ref[:, 2:3], route_ref[:, 3:4]
    lo1, hi1 = _unpack_bf16_pair(y1_ref[...])
    lo2, hi2 = _unpack_bf16_pair(y2_ref[...])
    y = jnp.concatenate([w1 * lo1 + w2 * lo2, w1 * hi1 + w2 * hi2], axis=-1)
    x_new = x_ref[...] + mod_ref[0, 5:6, :] * y
    if final_norm:
        x_new = x_new * lax.rsqrt(jnp.mean(x_new * x_new, axis=-1, keepdims=True) + RMS_EPS) * fw_ref[...]
    o_ref[...] = x_new


def _combine_call(x, yk, route, mod, final_w, *, rows_per_mod, tm, row_stride, final_norm):
    n_rows, d = x.shape
    half = yk.shape[1]
    n_mod = mod.shape[0]
    tiles_per_mod = rows_per_mod // tm
    second = row_stride // tm

    def row_map(i):
        return (i, 0)

    return pl.pallas_call(
        functools.partial(_combine_body, final_norm=final_norm),
        grid=(n_rows // tm,),
        in_specs=[pl.BlockSpec((tm, d), row_map),
                  pl.BlockSpec((tm, half), row_map),
                  pl.BlockSpec((tm, half), lambda i: (second + i, 0)),
                  pl.BlockSpec((tm, LANES), row_map),
                  pl.BlockSpec((1, 6, d), lambda i: (jnp.minimum(i // tiles_per_mod, n_mod - 1), 0, 0)),
                  pl.BlockSpec((1, d), lambda i: (0, 0))],
        out_specs=pl.BlockSpec((tm, d), row_map),
        out_shape=jax.ShapeDtypeStruct((n_rows, d), F32),
        compiler_params=_params(1),
    )(x, yk, yk, route, mod, final_w.reshape(1, d))


def _slot_plan(route, counts, n_experts, tm):
    n_rows = route.shape[0]
    n_tiles = (2 * n_rows) // tm + n_experts
    cnt = counts[0, :n_experts].astype(jnp.int32)
    padded = ((cnt + tm - 1) // tm) * tm
    ends = jnp.cumsum(padded)
    starts = ends - padded
    e1 = route[:, 0].astype(jnp.int32)
    e2 = route[:, 1].astype(jnp.int32)
    pos1 = starts[e1] + route[:, 4].astype(jnp.int32)
    pos2 = starts[e2] + route[:, 5].astype(jnp.int32)
    n_used = ends[-1] // tm
    tile_start = jnp.arange(n_tiles, dtype=jnp.int32) * tm
    tile_expert = jnp.sum((ends[None, :] <= tile_start[:, None]).astype(jnp.int32), axis=1)
    last_used = tile_expert[jnp.maximum(n_used - 1, 0)]
    tile_expert = jnp.where(tile_start < ends[-1], tile_expert, last_used)
    tile_expert = jnp.minimum(tile_expert, n_experts - 1).astype(jnp.int32)
    row_stride = n_rows
    spare = 2 * row_stride
    n_map = (n_tiles + 2) * tm
    init_rows = spare + jnp.arange(n_map, dtype=jnp.int32) % tm
    dst_rows = _slot_rows_call(pos1, pos2, init_rows.reshape(n_map // LANES, LANES), row_stride=row_stride, lead=tm,
                               tokens_per_step=2 * tm)
    n_out_rows = spare + tm
    return (tile_expert, n_used.reshape(1).astype(jnp.int32), dst_rows.reshape(n_map), row_stride, n_out_rows)


def _rope_tables(batch, seq, ctx_rows):
    pairs = HEAD_DIM // 4
    t = np.arange(seq)
    pos = np.stack([t // GRID_W, t % GRID_W], axis=-1).astype(np.float32)
    inv_freq = (ROPE_THETA ** (-np.arange(pairs, dtype=np.float32) / pairs)).astype(np.float32)
    ang = pos[:, :, None] * inv_freq
    cos = np.repeat(np.cos(ang)[:, :, None, :], 2, axis=2).reshape(seq, HEAD_DIM)
    sin = np.sin(ang)
    zero = np.zeros_like(sin)
    sa = np.stack([-sin, zero], axis=2).reshape(seq, HEAD_DIM)
    sb = np.stack([zero, sin], axis=2).reshape(seq, HEAD_DIM)

    def full(tab, fill):
        return jnp.asarray(np.concatenate([np.tile(tab, (batch, 1)),
                                           np.full((ctx_rows, HEAD_DIM), fill, np.float32)]), F32)

    return full(cos, 1.0), full(sa, 0.0), full(sb, 0.0)


def kernel(x, c, ctx, c_ctx, w_ada, b_ada, norm_mix_w, norm_ffn_w, w_in, sgu_norm_w, sgu_w_s, sgu_b_s, na_rpb,
           w_merge_gate, b_merge_gate, w_branch_a, w_branch_b, w_out, w_router_group, b_router_group,
           w_router_expert, b_router_expert, w_exp_gate, w_exp_up, w_exp_down, final_norm_w):
    batch, seq, d = x.shape
    ctx_len = ctx.shape[1]
    depth = w_ada.shape[0]
    n_experts = w_exp_gate.shape[1]
    n_lat = batch * seq
    n_ctx = batch * ctx_len
    tm = 256
    tm_wide = 512
    assert seq % tm_wide == 0 and n_ctx % tm_wide == 0 and tm % SGU_CHUNK == 0 and ctx_len % SGU_CHUNK == 0

    mods = _ada_call(jnp.concatenate([c, c_ctx[None]], axis=0), w_ada, b_ada)
    cos, sa, sb = _rope_tables(batch, seq, n_ctx)
    grid_rows = seq // GRID_W

    x_cur = x.reshape(n_lat, d)
    ctx_rows = ctx.reshape(n_ctx, d)
    x_all = None
    for l in range(depth):
        last = l == depth - 1
        mod = mods[l, :batch + 1].reshape(batch + 1, 6, d)
        w_in_b = w_in[l].astype(BF16)
        w_s_b = sgu_w_s[l].astype(BF16)
        b_s_t = sgu_b_s[l].T
        if x_all is None:
            src = (x_cur, ctx_rows)
        else:
            src = (x_all, None)
        h, a, qp, qr, kr, v = _inproj_call(src[0], src[1], mod, norm_mix_w[l], w_in_b, sgu_norm_w[l], w_s_b, b_s_t,
                                           cos, sa, sb, rows_per_mod=seq, tm=tm)
        o = _attn_call(qr, qp, kr, v, _attn_bias_tables(na_rpb[l], grid_rows), batch=batch, seq=seq,
                       ctx_len=ctx_len, with_ctx_queries=not last)
        n_rows = n_lat if last else n_lat + n_ctx
        m = _merge_call(h, a, o, w_merge_gate[l].astype(BF16), b_merge_gate[l], w_branch_a[l].astype(BF16),
                        w_branch_b[l].astype(BF16), n_rows=n_rows, tm=tm_wide)
        w_router = jnp.zeros((d, LANES), F32)
        w_router = w_router.at[:, :n_experts].set(w_router_expert[l])
        w_router = w_router.at[:, n_experts:n_experts + N_GROUPS].set(w_router_group[l])
        w_router_hi = w_router.astype(BF16)
        w_router_lo = (w_router - w_router_hi.astype(F32)).astype(BF16)
        w_router = jnp.concatenate([w_router_hi, w_router_lo], axis=1)
        b_router = jnp.zeros((1, LANES), F32)
        b_router = b_router.at[0, :n_experts].set(b_router_expert[l])
        b_router = b_router.at[0, n_experts:n_experts + N_GROUPS].set(b_router_group[l])
        x_mid, h_packed, route, counts = _outproj_call(src[0], src[1], m, w_out[l].astype(BF16), mod, norm_ffn_w[l],
                                                       w_router, b_router, rows_per_mod=seq, tm=tm_wide,
                                                       n_experts=n_experts)
        tile_expert, n_used, dst_rows, row_stride, n_out_rows = _slot_plan(route, counts, n_experts, tm)
        yk = _ffn_call(tile_expert, n_used, dst_rows, h_packed, w_exp_gate, w_exp_up, w_exp_down, layer=l, tm=tm,
                       row_stride=row_stride, n_out_rows=n_out_rows)
        x_all = _combine_call(x_mid, yk, route, mod, final_norm_w, rows_per_mod=seq, tm=tm,
                              row_stride=row_stride, final_norm=last)
    return x_all.reshape(batch, seq, d)
```

```python
import functools

import numpy as np
import jax
import jax.numpy as jnp
from jax import lax
from jax.experimental import pallas as pl
from jax.experimental.pallas import tpu as pltpu

GRID_W = 64
SGU_CHUNK = 128
SGU_GROUPS = 8
NA_HEADS = 8
HEAD_DIM = 128
WIN_ROWS = 8
WIN_COLS = 16
ROPE_THETA = 10000.0
N_GROUPS = 4
EXPERTS_PER_GROUP = 8
RMS_EPS = 1e-6

LANES = 128
Q_ROWS = 4
Q_BLOCK = Q_ROWS * GRID_W
KEY_ROWS = Q_ROWS + WIN_ROWS - 1
MASKED = -1e30
VMEM_LIMIT = 56 * 1024 * 1024

BF16 = jnp.bfloat16
F32 = jnp.float32
U32 = jnp.uint32


def _params(n_grid_dims, vmem=VMEM_LIMIT):
    return pltpu.CompilerParams(dimension_semantics=("arbitrary",) * n_grid_dims, vmem_limit_bytes=vmem)


def _resident(shape):
    nd = len(shape)
    return pl.BlockSpec(shape, lambda *_: (0,) * nd, pipeline_mode=pl.Buffered(1))


def _pack_bf16_pair(lo, hi):
    lo_bits = lax.bitcast_convert_type(lo.astype(BF16).astype(F32), U32)
    hi_bits = lax.bitcast_convert_type(hi.astype(BF16).astype(F32), U32)
    return (hi_bits & jnp.uint32(0xFFFF0000)) | (lo_bits >> 16)


def _unpack_bf16_pair(w):
    lo = lax.bitcast_convert_type(w << 16, F32)
    hi = lax.bitcast_convert_type(w & jnp.uint32(0xFFFF0000), F32)
    return lo, hi


def _rms_modulate(x, norm_w, shift, scale):
    y = x * lax.rsqrt(jnp.mean(x * x, axis=-1, keepdims=True) + RMS_EPS) * norm_w
    return y * (1.0 + scale) + shift


def _ada_body(ct_ref, w_ref, b_ref, o_ref, *, n_rows):
    s = ct_ref[...]
    s = s * jax.nn.sigmoid(s)
    w = w_ref[0]
    o_ref[...] = jnp.zeros_like(o_ref)
    for r in range(n_rows):
        o_ref[0, r:r + 1, :] = jnp.sum(w * s[:, r:r + 1], axis=0, keepdims=True) + b_ref[0]


def _ada_call(cond, w_ada, b_ada):
    n_rows, d = cond.shape
    depth, _, n = w_ada.shape
    tn = min(n, 512)
    ct = jnp.zeros((d, 8), F32).at[:, :n_rows].set(cond.T)
    return pl.pallas_call(
        functools.partial(_ada_body, n_rows=n_rows),
        grid=(depth, n // tn),
        in_specs=[pl.BlockSpec((d, 8), lambda l, j: (0, 0)),
                  pl.BlockSpec((1, d, tn), lambda l, j: (l, 0, j)),
                  pl.BlockSpec((1, 1, tn), lambda l, j: (l, 0, j))],
        out_specs=pl.BlockSpec((1, 8, tn), lambda l, j: (l, 0, j)),
        out_shape=jax.ShapeDtypeStruct((depth, 8, n), F32),
        compiler_params=_params(2),
    )(ct, w_ada, b_ada.reshape(depth, 1, n))


def _inproj_body(*refs, n_main_tiles, two_src, sgu_w, na_w):
    if two_src:
        xa_ref, xb_ref = refs[:2]
        refs = refs[2:]
    else:
        xa_ref = refs[0]
        refs = refs[1:]
    (mod_ref, nw_ref, w_ref, snw_ref, ws_ref, bst_ref, cos_ref, sa_ref, sb_ref,
     h_ref, a_ref, qp_ref, qr_ref, kr_ref, v_ref) = refs
    if two_src:
        x = jnp.where(pl.program_id(0) < n_main_tiles, xa_ref[...], xb_ref[...])
    else:
        x = xa_ref[...]
    tm = x.shape[0]
    h = _rms_modulate(x, nw_ref[...], mod_ref[0, 0:1, :], mod_ref[0, 1:2, :])
    hb = h.astype(BF16)
    h_ref[...] = hb

    def proj(lo, width):
        return jnp.dot(hb, w_ref[:, lo:lo + width], preferred_element_type=F32)

    u = jax.nn.gelu(proj(0, sgu_w))
    v = jax.nn.gelu(proj(sgu_w, sgu_w))
    vn = v * lax.rsqrt(jnp.mean(v * v, axis=-1, keepdims=True) + RMS_EPS) * snw_ref[...]
    vnb = vn.astype(BF16)
    gch = sgu_w // SGU_GROUPS
    for c in range(tm // SGU_CHUNK):
        rows = slice(c * SGU_CHUNK, (c + 1) * SGU_CHUNK)
        for g in range(SGU_GROUPS):
            cols = slice(g * gch, (g + 1) * gch)
            z = jnp.dot(ws_ref[g], vnb[rows, cols], preferred_element_type=F32) + bst_ref[:, g:g + 1]
            a_ref[rows, cols] = (u[rows, cols] * z).astype(BF16)

    cos, sa, sb = cos_ref[...], sa_ref[...], sb_ref[...]

    def rope_into(p, out_ref):
        for hh in range(NA_HEADS):
            cols = slice(hh * HEAD_DIM, (hh + 1) * HEAD_DIM)
            xh = p[:, cols]
            out_ref[:, cols] = (xh * cos + pltpu.roll(xh, HEAD_DIM - 32, 1) * sa
                                + pltpu.roll(xh, 32, 1) * sb).astype(BF16)

    q = proj(2 * sgu_w, na_w)
    qp_ref[...] = q.astype(BF16)
    rope_into(q, qr_ref)
    rope_into(proj(2 * sgu_w + na_w, na_w), kr_ref)
    v_ref[...] = proj(2 * sgu_w + 2 * na_w, na_w).astype(BF16)


def _inproj_call(x_main, x_ctx, mod, norm_w, w_in_b, sgu_norm_w, w_s_b, b_s_t, cos, sa, sb,
                 *, rows_per_mod, tm):
    d = x_main.shape[1]
    n_main = x_main.shape[0]
    two_src = x_ctx is not None
    tt = n_main + (x_ctx.shape[0] if two_src else 0)
    n_main_tiles = n_main // tm
    n_mod = mod.shape[0]
    sgu_w = sgu_norm_w.shape[-1]
    na_w = NA_HEADS * HEAD_DIM
    tiles_per_mod = rows_per_mod // tm

    def row_map(i):
        return (i, 0)

    x_specs = [pl.BlockSpec((tm, d), lambda i: (jnp.minimum(i, n_main_tiles - 1), 0))]
    x_args = [x_main]
    if two_src:
        x_specs.append(pl.BlockSpec((tm, d), lambda i: (jnp.maximum(i - n_main_tiles, 0), 0)))
        x_args.append(x_ctx)
    in_specs = x_specs + [
        pl.BlockSpec((1, 6, d), lambda i: (jnp.minimum(i // tiles_per_mod, n_mod - 1), 0, 0)),
        _resident((1, d)),
        _resident(w_in_b.shape),
        _resident((1, sgu_w)),
        _resident(w_s_b.shape),
        _resident(b_s_t.shape),
        pl.BlockSpec((tm, HEAD_DIM), row_map),
        pl.BlockSpec((tm, HEAD_DIM), row_map),
        pl.BlockSpec((tm, HEAD_DIM), row_map),
    ]
    out_widths = [d, sgu_w, na_w, na_w, na_w, na_w]
    return pl.pallas_call(
        functools.partial(_inproj_body, n_main_tiles=n_main_tiles, two_src=two_src, sgu_w=sgu_w, na_w=na_w),
        grid=(tt // tm,),
        in_specs=in_specs,
        out_specs=[pl.BlockSpec((tm, w), row_map) for w in out_widths],
        out_shape=[jax.ShapeDtypeStruct((tt, w), BF16) for w in out_widths],
        compiler_params=_params(1),
    )(*x_args, mod, norm_w.reshape(1, d), w_in_b, sgu_norm_w.reshape(1, sgu_w), w_s_b, b_s_t, cos, sa, sb)


def _softmax_pv(scores, values):
    m = functools.reduce(jnp.maximum, [jnp.max(s, axis=-1, keepdims=True) for s in scores])
    ps = [jnp.exp(s - m) for s in scores]
    denom = functools.reduce(jnp.add, [jnp.sum(p, axis=-1, keepdims=True) for p in ps])
    acc = functools.reduce(jnp.add, [jnp.dot(p.astype(BF16), v, preferred_element_type=F32)
                                     for p, v in zip(ps, values)])
    return acc / denom


def _qk(q, k):
    return lax.dot_general(q, k, (((1,), (1,)), ((), ())), preferred_element_type=F32) * (HEAD_DIM ** -0.5)


def _attn_body(qr_ref, qp_ref, k_ref, v_ref, kc_ref, vc_ref, bias_ref, o_ref, *, n_blocks, grid_rows):
    i = pl.program_id(2)
    n_keys = KEY_ROWS * GRID_W

    @pl.when(i < n_blocks)
    def _():
        k_row0 = jnp.clip(i * Q_ROWS - WIN_ROWS // 2, 0, grid_rows - KEY_ROWS)
        start = pl.multiple_of(k_row0 * GRID_W, GRID_W)
        k_loc = k_ref[pl.ds(start, n_keys), :]
        v_loc = v_ref[pl.ds(start, n_keys), :]
        s_loc = _qk(qr_ref[...], k_loc) + bias_ref[0, 0]
        s_ctx = _qk(qp_ref[...], kc_ref[...])
        o_ref[...] = _softmax_pv([s_loc, s_ctx], [v_loc, vc_ref[...]]).astype(BF16)

    @pl.when(i >= n_blocks)
    def _():
        s_ctx = _qk(qp_ref[...], kc_ref[...])
        o_ref[...] = _softmax_pv([s_ctx], [vc_ref[...]]).astype(BF16)


def _attn_bias_tables(rpb, grid_rows):
    n_blocks = grid_rows // Q_ROWS
    wr = min(WIN_ROWS, grid_rows)
    assert wr == WIN_ROWS and grid_rows >= KEY_ROWS and grid_rows % Q_ROWS == 0
    n_ri, n_ci = 2 * WIN_ROWS - 1, 2 * WIN_COLS - 1

    qc = np.arange(GRID_W)
    c0 = np.clip(qc - WIN_COLS // 2, 0, GRID_W - WIN_COLS)
    col_valid = (qc[None, :] >= c0[:, None]) & (qc[None, :] < c0[:, None] + WIN_COLS)
    col_sel = (col_valid[:, :, None]
               & ((qc[None, :, None] - qc[:, None, None] + (WIN_COLS - 1)) == np.arange(n_ci))).astype(np.float32)

    def row_structure(blk):
        k_row0 = int(np.clip(blk * Q_ROWS - WIN_ROWS // 2, 0, grid_rows - KEY_ROWS))
        qr = blk * Q_ROWS + np.arange(Q_ROWS)
        kr = k_row0 + np.arange(KEY_ROWS)
        r0 = np.clip(qr - wr // 2, 0, grid_rows - wr)
        valid = (kr[None, :] >= r0[:, None]) & (kr[None, :] < r0[:, None] + wr)
        sel = valid[:, :, None] & ((kr[None, :, None] - qr[:, None, None] + (WIN_ROWS - 1)) == np.arange(n_ri))
        return valid, sel.astype(np.float32)

    kinds = [0, min(1, n_blocks - 1), n_blocks - 1]
    interior = row_structure(kinds[1])
    for blk in range(1, n_blocks - 1):
        assert all(np.array_equal(a, b) for a, b in zip(row_structure(blk), interior))

    exact = lax.Precision.HIGHEST
    by_col = jnp.einsum('hrc,qkc->hrqk', rpb.astype(F32), jnp.asarray(col_sel), precision=exact)
    tables = []
    for blk in kinds:
        row_valid, row_sel = row_structure(blk)
        dense = jnp.einsum('jlr,hrqk->hjqlk', jnp.asarray(row_sel), by_col, precision=exact)
        valid = row_valid[:, None, :, None] & col_valid[None, :, None, :]
        dense = jnp.where(jnp.asarray(valid)[None], dense, MASKED)
        tables.append(dense.reshape(NA_HEADS, Q_BLOCK, KEY_ROWS * GRID_W))
    return jnp.stack(tables)


def _attn_call(qr, qp, kr, v, bias_tables, *, batch, seq, ctx_len, with_ctx_queries):
    assert ctx_len == Q_BLOCK and seq % Q_BLOCK == 0
    grid_rows = seq // GRID_W
    n_blocks = grid_rows // Q_ROWS
    n_steps = n_blocks + (1 if with_ctx_queries else 0)
    ctx_block0 = batch * n_blocks
    n_out = batch * seq + (batch * ctx_len if with_ctx_queries else 0)
    n_keys = KEY_ROWS * GRID_W

    def q_map(b, h, i):
        return (jnp.where(i < n_blocks, b * n_blocks + i, ctx_block0 + b), h)

    def kind_map(b, h, i):
        return (jnp.where(i == 0, 0, jnp.where(i >= n_blocks - 1, 2, 1)), h, 0, 0)

    return pl.pallas_call(
        functools.partial(_attn_body, n_blocks=n_blocks, grid_rows=grid_rows),
        grid=(batch, NA_HEADS, n_steps),
        in_specs=[pl.BlockSpec((Q_BLOCK, HEAD_DIM), q_map),
                  pl.BlockSpec((Q_BLOCK, HEAD_DIM), q_map),
                  pl.BlockSpec((seq, HEAD_DIM), lambda b, h, i: (b, h)),
                  pl.BlockSpec((seq, HEAD_DIM), lambda b, h, i: (b, h)),
                  pl.BlockSpec((ctx_len, HEAD_DIM), lambda b, h, i: (ctx_block0 + b, h)),
                  pl.BlockSpec((ctx_len, HEAD_DIM), lambda b, h, i: (ctx_block0 + b, h)),
                  pl.BlockSpec((1, 1, Q_BLOCK, n_keys), kind_map)],
        out_specs=pl.BlockSpec((Q_BLOCK, HEAD_DIM), q_map),
        out_shape=jax.ShapeDtypeStruct((n_out, NA_HEADS * HEAD_DIM), BF16),
        compiler_params=_params(3),
    )(qr, qp, kr, v, kr, v, bias_tables)


def _merge_body(h_ref, a_ref, o_ref, wg_ref, bg_ref, wa_ref, wb_ref, m_ref, *, tn):
    d = m_ref.shape[1]
    hb, ab, ob = h_ref[...], a_ref[...], o_ref[...]
    for n0 in range(0, d, tn):
        cols = slice(n0, n0 + tn)
        gcols = slice(d + n0, d + n0 + tn)
        g_a = jax.nn.sigmoid(jnp.dot(hb, wg_ref[:, cols], preferred_element_type=F32) + bg_ref[:, cols])
        g_b = jax.nn.sigmoid(jnp.dot(hb, wg_ref[:, gcols], preferred_element_type=F32) + bg_ref[:, gcols])
        pa = jnp.dot(ab, wa_ref[:, cols], preferred_element_type=F32)
        pb = jnp.dot(ob, wb_ref[:, cols], preferred_element_type=F32)
        m_ref[:, cols] = (g_a * pa + g_b * pb).astype(BF16)


def _merge_call(h, a, o, w_gate_b, b_gate, w_a_b, w_b_b, *, n_rows, tm):
    d = h.shape[1]

    def row_map(i):
        return (i, 0)

    return pl.pallas_call(
        functools.partial(_merge_body, tn=min(d, 512)),
        grid=(n_rows // tm,),
        in_specs=[pl.BlockSpec((tm, d), row_map),
                  pl.BlockSpec((tm, a.shape[1]), row_map),
                  pl.BlockSpec((tm, o.shape[1]), row_map),
                  _resident(w_gate_b.shape), _resident((1, 2 * d)),
                  _resident(w_a_b.shape), _resident(w_b_b.shape)],
        out_specs=pl.BlockSpec((tm, d), row_map),
        out_shape=jax.ShapeDtypeStruct((n_rows, d), BF16),
        compiler_params=_params(1),
    )(h, a, o, w_gate_b, b_gate.reshape(1, 2 * d), w_a_b, w_b_b)


def _outproj_body(*refs, n_main_tiles, two_src, n_experts):
    if two_src:
        xa_ref, xb_ref = refs[:2]
        refs = refs[2:]
    else:
        xa_ref = refs[0]
        refs = refs[1:]
    m_ref, wo_ref, mod_ref, nw_ref, wr_ref, br_ref, xo_ref, hp_ref, route_ref, cnt_ref, carry_ref = refs
    i = pl.program_id(0)
    if two_src:
        x = jnp.where(i < n_main_tiles, xa_ref[...], xb_ref[...])
    else:
        x = xa_ref[...]
    tm, d = x.shape

    @pl.when(i == 0)
    def _():
        carry_ref[...] = jnp.zeros_like(carry_ref)

    y = jnp.dot(m_ref[...], wo_ref[...], preferred_element_type=F32)
    x_new = x + mod_ref[0, 2:3, :] * y
    xo_ref[...] = x_new
    h = _rms_modulate(x_new, nw_ref[...], mod_ref[0, 3:4, :], mod_ref[0, 4:5, :])
    hp_ref[...] = _to_token_tiles(_pack_bf16_pair(h[:, :d // 2], h[:, d // 2:]))

    h_hi = h.astype(BF16)
    h_lo = (h - h_hi.astype(F32)).astype(BF16)
    by_hi = jnp.dot(h_hi, wr_ref[...], preferred_element_type=F32)
    logits = (by_hi[:, :LANES] + by_hi[:, LANES:]
              + jnp.dot(h_lo, wr_ref[:, :LANES], preferred_element_type=F32) + br_ref[...])
    lane = lax.broadcasted_iota(jnp.int32, logits.shape, 1)
    lane_f = lane.astype(F32)
    far = jnp.float32(4 * LANES)

    def first_argmax(vals):
        top = jnp.max(vals, axis=-1, keepdims=True)
        return top, jnp.min(jnp.where(vals == top, lane_f, far), axis=-1, keepdims=True)

    g_logits = jnp.where((lane >= n_experts) & (lane < n_experts + N_GROUPS), logits, MASKED)
    g_top, g_lane = first_argmax(g_logits)
    g_prob = 1.0 / jnp.sum(jnp.exp(g_logits - g_top), axis=-1, keepdims=True)
    e_lo = (g_lane - n_experts) * EXPERTS_PER_GROUP
    e_logits = jnp.where((lane_f >= e_lo) & (lane_f < e_lo + EXPERTS_PER_GROUP), logits, MASKED)
    top1, e1 = first_argmax(e_logits)
    top2, e2 = first_argmax(jnp.where(lane_f == e1, MASKED, e_logits))
    t = jnp.exp(top2 - top1)
    w1 = g_prob / (1.0 + t)
    w2 = g_prob * t / (1.0 + t)

    sel1, sel2 = lane_f == e1, lane_f == e2
    onehot = jnp.where(sel1 | sel2, 1.0, 0.0)
    r_i = lax.broadcasted_iota(jnp.int32, (tm, tm), 0)
    c_i = lax.broadcasted_iota(jnp.int32, (tm, tm), 1)
    earlier = jnp.where(c_i < r_i, 1.0, 0.0).astype(BF16)
    before = jnp.dot(earlier, onehot.astype(BF16), preferred_element_type=F32) + carry_ref[...]
    rank1 = jnp.sum(jnp.where(sel1, before, 0.0), axis=-1, keepdims=True)
    rank2 = jnp.sum(jnp.where(sel2, before, 0.0), axis=-1, keepdims=True)
    carry_ref[...] += jnp.sum(onehot, axis=0, keepdims=True)
    cnt_ref[...] = carry_ref[...]

    route = jnp.zeros_like(logits)
    for k, val in enumerate((e1, e2, w1, w2, rank1, rank2)):
        route = jnp.where(lane == k, val, route)
    route_ref[...] = route


def _outproj_call(x_main, x_ctx, m, w_out_b, mod, norm_w, w_router, b_router, *, rows_per_mod, tm, n_experts):
    d = x_main.shape[1]
    two_src = x_ctx is not None
    n_main = x_main.shape[0] if two_src else m.shape[0]
    n_rows = m.shape[0]
    n_main_tiles = n_main // tm
    n_mod = mod.shape[0]
    tiles_per_mod = rows_per_mod // tm

    def row_map(i):
        return (i, 0)

    x_specs = [pl.BlockSpec((tm, d), lambda i: (jnp.minimum(i, n_main_tiles - 1), 0))]
    x_args = [x_main]
    if two_src:
        x_specs.append(pl.BlockSpec((tm, d), lambda i: (jnp.maximum(i - n_main_tiles, 0), 0)))
        x_args.append(x_ctx)
    in_specs = x_specs + [
        pl.BlockSpec((tm, d), row_map),
        _resident(w_out_b.shape),
        pl.BlockSpec((1, 6, d), lambda i: (jnp.minimum(i // tiles_per_mod, n_mod - 1), 0, 0)),
        _resident((1, d)),
        _resident(w_router.shape),
        _resident((1, LANES)),
    ]
    return pl.pallas_call(
        functools.partial(_outproj_body, n_main_tiles=n_main_tiles, two_src=two_src, n_experts=n_experts),
        grid=(n_rows // tm,),
        in_specs=in_specs,
        out_specs=[pl.BlockSpec((tm, d), row_map),
                   pl.BlockSpec((tm, d // 2 // LANES, LANES), lambda i: (i, 0, 0)),
                   pl.BlockSpec((tm, LANES), row_map),
                   pl.BlockSpec((1, LANES), lambda i: (0, 0))],
        out_shape=[jax.ShapeDtypeStruct((n_rows, d), F32),
                   jax.ShapeDtypeStruct((n_rows, d // 2 // LANES, LANES), U32),
                   jax.ShapeDtypeStruct((n_rows, LANES), F32),
                   jax.ShapeDtypeStruct((1, LANES), F32)],
        scratch_shapes=[pltpu.VMEM((1, LANES), F32)],
        compiler_params=_params(1),
    )(*x_args, m, w_out_b, mod, norm_w.reshape(1, d), w_router, b_router)


def _row_copy(src_ref, src_row, dst_ref, dst_row, sem):
    return pltpu.make_async_copy(src_ref.at[src_row], dst_ref.at[dst_row], sem)


def _to_token_tiles(rows):
    return rows.reshape(rows.shape[0], rows.shape[1] // LANES, LANES)


def _from_token_tiles(tiles):
    return tiles.reshape(tiles.shape[0], tiles.shape[1] * LANES)


def _slot_rows_body(p1_ref, p2_ref, init_ref, dst_ref, stage_ref, sem, *, tokens_per_step, row_stride):
    i = pl.program_id(0)

    @pl.when(i == 0)
    def _():
        load = pltpu.make_async_copy(init_ref, stage_ref, sem)
        load.start()
        load.wait()

    def put(r, carry):
        t = i * tokens_per_step + r
        stage_ref[p1_ref[t]] = t
        stage_ref[p2_ref[t]] = row_stride + t
        return carry

    lax.fori_loop(0, tokens_per_step, put, 0, unroll=8)

    @pl.when(i == pl.num_programs(0) - 1)
    def _():
        store = pltpu.make_async_copy(stage_ref, dst_ref, sem)
        store.start()
        store.wait()


def _slot_rows_call(pos1, pos2, init_rows, *, row_stride, tokens_per_step):
    n_tok = pos1.shape[0]
    n_map = init_rows.shape[0]
    return pl.pallas_call(
        functools.partial(_slot_rows_body, tokens_per_step=tokens_per_step, row_stride=row_stride),
        grid_spec=pltpu.PrefetchScalarGridSpec(
            num_scalar_prefetch=2,
            grid=(n_tok // tokens_per_step,),
            in_specs=[pl.BlockSpec(memory_space=pl.ANY)],
            out_specs=pl.BlockSpec(memory_space=pl.ANY),
            scratch_shapes=[pltpu.SMEM((n_map,), jnp.int32), pltpu.SemaphoreType.DMA(())],
        ),
        out_shape=jax.ShapeDtypeStruct((n_map,), jnp.int32),
        compiler_params=_params(1),
    )(pos1, pos2, init_rows)


def _ffn_body(te_ref, nu_ref, src_ref, dst_ref, h_ref, wg_ref, wu_ref, wd_ref, yk_ref,
              wgb_ref, wub_ref, wdb_ref, xbuf, ybuf, sem_g, sem_s, *, tm, spare_rows):
    j = pl.program_id(0)
    n_used = nu_ref[0]
    slot = j % 2
    other = 1 - slot
    half = xbuf.shape[2] * LANES

    def gather_row(tile, r, buf):
        return _row_copy(h_ref, src_ref[(tile + 1) * tm + r], xbuf.at[buf], r, sem_g.at[buf])

    def scatter_row(tile, r, buf):
        return _row_copy(ybuf.at[buf], r, yk_ref, dst_ref[(tile + 1) * tm + r], sem_s.at[buf])

    def wait_tile(buf_ref, sem):
        pltpu.make_async_copy(buf_ref, buf_ref, sem).wait()

    @pl.when(j == 0)
    def _():
        ybuf[...] = jnp.zeros_like(ybuf)

        def prime(r, carry):
            _row_copy(ybuf.at[0], r, yk_ref, spare_rows + r, sem_s.at[0]).start()
            gather_row(0, r, 0).start()
            return carry

        lax.fori_loop(0, tm, prime, 0)

    @pl.when((j == 0) | (te_ref[j] != te_ref[jnp.maximum(j - 1, 0)]))
    def _():
        wgb_ref[...] = wg_ref[0, 0].astype(BF16)
        wub_ref[...] = wu_ref[0, 0].astype(BF16)
        wdb_ref[...] = wd_ref[0, 0].astype(BF16)

    @pl.when(j < n_used)
    def _():
        wait_tile(xbuf.at[slot], sem_g.at[slot])
        lo, hi = _unpack_bf16_pair(_from_token_tiles(xbuf[slot]))
        lo, hi = lo.astype(BF16), hi.astype(BF16)
        wait_tile(ybuf.at[slot], sem_s.at[slot])
        for r in range(tm):
            gather_row(j + 1, r, other).start()
            scatter_row(j - 1, r, other).start()

        def up(w_ref):
            return (jnp.dot(lo, w_ref[:half, :], preferred_element_type=F32)
                    + jnp.dot(hi, w_ref[half:, :], preferred_element_type=F32))

        g = up(wgb_ref)
        hid = (g * jax.nn.sigmoid(g) * up(wub_ref)).astype(BF16)
        y = jnp.dot(hid, wdb_ref[...], preferred_element_type=F32)
        ybuf[slot] = _to_token_tiles(_pack_bf16_pair(y[:, :half], y[:, half:]))

    @pl.when(j == n_used - 1)
    def _():
        def flush(r, carry):
            scatter_row(j, r, slot).start()
            return carry

        wait_tile(ybuf.at[other], sem_s.at[other])
        lax.fori_loop(0, tm, flush, 0)
        wait_tile(ybuf.at[slot], sem_s.at[slot])
        wait_tile(xbuf.at[other], sem_g.at[other])


def _ffn_call(tile_expert, n_used, dst_rows, h_packed, w_gate, w_up, w_down, *, layer, tm, row_stride, n_out_rows):
    _, sub, _ = h_packed.shape
    _, _, d, de = w_gate.shape
    n_tiles = tile_expert.shape[0]
    src_rows = jnp.where(dst_rows < 2 * row_stride, dst_rows % row_stride, 0)

    def w_map(j, te, nu, sr, dr):
        return (layer, te[j], 0, 0)

    return pl.pallas_call(
        functools.partial(_ffn_body, tm=tm, spare_rows=2 * row_stride),
        grid_spec=pltpu.PrefetchScalarGridSpec(
            num_scalar_prefetch=4,
            grid=(n_tiles,),
            in_specs=[pl.BlockSpec(memory_space=pl.ANY),
                      pl.BlockSpec((1, 1, d, de), w_map),
                      pl.BlockSpec((1, 1, d, de), w_map),
                      pl.BlockSpec((1, 1, de, d), w_map)],
            out_specs=pl.BlockSpec(memory_space=pl.ANY),
            scratch_shapes=[pltpu.VMEM((d, de), BF16), pltpu.VMEM((d, de), BF16), pltpu.VMEM((de, d), BF16),
                            pltpu.VMEM((2, tm, sub, LANES), U32), pltpu.VMEM((2, tm, sub, LANES), U32),
                            pltpu.SemaphoreType.DMA((2,)), pltpu.SemaphoreType.DMA((2,))],
        ),
        out_shape=jax.ShapeDtypeStruct((n_out_rows, sub, LANES), U32),
        compiler_params=_params(1),
    )(tile_expert, n_used, src_rows, dst_rows, h_packed, w_gate, w_up, w_down)


def _combine_body(x_ref, y1_ref, y2_ref, route_ref, mod_ref, fw_ref, o_ref, *, final_norm):
    w1, w2 = route_ref[:, 2:3], route_ref[:, 3:4]
    lo1, hi1 = _unpack_bf16_pair(_from_token_tiles(y1_ref[...]))
    lo2, hi2 = _unpack_bf16_pair(_from_token_tiles(y2_ref[...]))
    y = jnp.concatenate([w1 * lo1 + w2 * lo2, w1 * hi1 + w2 * hi2], axis=-1)
    x_new = x_ref[...] + mod_ref[0, 5:6, :] * y
    if final_norm:
        x_new = x_new * lax.rsqrt(jnp.mean(x_new * x_new, axis=-1, keepdims=True) + RMS_EPS) * fw_ref[...]
    o_ref[...] = x_new


def _combine_call(x, yk, route, mod, final_w, *, rows_per_mod, tm, row_stride, final_norm):
    n_rows, d = x.shape
    sub = yk.shape[1]
    n_mod = mod.shape[0]
    tiles_per_mod = rows_per_mod // tm
    second = row_stride // tm

    def row_map(i):
        return (i, 0)

    return pl.pallas_call(
        functools.partial(_combine_body, final_norm=final_norm),
        grid=(n_rows // tm,),
        in_specs=[pl.BlockSpec((tm, d), row_map),
                  pl.BlockSpec((tm, sub, LANES), lambda i: (i, 0, 0)),
                  pl.BlockSpec((tm, sub, LANES), lambda i: (second + i, 0, 0)),
                  pl.BlockSpec((tm, LANES), row_map),
                  pl.BlockSpec((1, 6, d), lambda i: (jnp.minimum(i // tiles_per_mod, n_mod - 1), 0, 0)),
                  pl.BlockSpec((1, d), lambda i: (0, 0))],
        out_specs=pl.BlockSpec((tm, d), row_map),
        out_shape=jax.ShapeDtypeStruct((n_rows, d), F32),
        compiler_params=_params(1),
    )(x, yk, yk, route, mod, final_w.reshape(1, d))


def _slot_plan(route, counts, n_experts, tm):
    n_rows = route.shape[0]
    n_tiles = (2 * n_rows) // tm + n_experts
    cnt = counts[0, :n_experts].astype(jnp.int32)
    padded = ((cnt + tm - 1) // tm) * tm
    ends = jnp.cumsum(padded)
    starts = ends - padded
    e1 = route[:, 0].astype(jnp.int32)
    e2 = route[:, 1].astype(jnp.int32)
    pos1 = starts[e1] + route[:, 4].astype(jnp.int32)
    pos2 = starts[e2] + route[:, 5].astype(jnp.int32)
    n_used = ends[-1] // tm
    tile_start = jnp.arange(n_tiles, dtype=jnp.int32) * tm
    tile_expert = jnp.sum((ends[None, :] <= tile_start[:, None]).astype(jnp.int32), axis=1)
    last_used = tile_expert[jnp.maximum(n_used - 1, 0)]
    tile_expert = jnp.where(tile_start < ends[-1], tile_expert, last_used)
    tile_expert = jnp.minimum(tile_expert, n_experts - 1).astype(jnp.int32)
    row_stride = n_rows
    spare = 2 * row_stride
    n_map = pl.cdiv((n_tiles + 2) * tm, 1024) * 1024
    init_rows = spare + jnp.arange(n_map, dtype=jnp.int32) % tm
    dst_rows = _slot_rows_call(pos1 + tm, pos2 + tm, init_rows, row_stride=row_stride, tokens_per_step=2 * tm)
    n_out_rows = spare + tm
    return (tile_expert, n_used.reshape(1).astype(jnp.int32), dst_rows, row_stride, n_out_rows)


def _rope_tables(batch, seq, ctx_rows):
    pairs = HEAD_DIM // 4
    t = np.arange(seq)
    pos = np.stack([t // GRID_W, t % GRID_W], axis=-1).astype(np.float32)
    inv_freq = (ROPE_THETA ** (-np.arange(pairs, dtype=np.float32) / pairs)).astype(np.float32)
    ang = pos[:, :, None] * inv_freq
    cos = np.repeat(np.cos(ang)[:, :, None, :], 2, axis=2).reshape(seq, HEAD_DIM)
    sin = np.sin(ang)
    zero = np.zeros_like(sin)
    sa = np.stack([-sin, zero], axis=2).reshape(seq, HEAD_DIM)
    sb = np.stack([zero, sin], axis=2).reshape(seq, HEAD_DIM)

    def full(tab, fill):
        return jnp.asarray(np.concatenate([np.tile(tab, (batch, 1)),
                                           np.full((ctx_rows, HEAD_DIM), fill, np.float32)]), F32)

    return full(cos, 1.0), full(sa, 0.0), full(sb, 0.0)


def kernel(x, c, ctx, c_ctx, w_ada, b_ada, norm_mix_w, norm_ffn_w, w_in, sgu_norm_w, sgu_w_s, sgu_b_s, na_rpb,
           w_merge_gate, b_merge_gate, w_branch_a, w_branch_b, w_out, w_router_group, b_router_group,
           w_router_expert, b_router_expert, w_exp_gate, w_exp_up, w_exp_down, final_norm_w):
    batch, seq, d = x.shape
    ctx_len = ctx.shape[1]
    depth = w_ada.shape[0]
    n_experts = w_exp_gate.shape[1]
    n_lat = batch * seq
    n_ctx = batch * ctx_len
    tm = 256
    tm_wide = 512
    assert seq % tm_wide == 0 and n_ctx % tm_wide == 0 and tm % SGU_CHUNK == 0 and ctx_len % SGU_CHUNK == 0

    mods = _ada_call(jnp.concatenate([c, c_ctx[None]], axis=0), w_ada, b_ada)
    cos, sa, sb = _rope_tables(batch, seq, n_ctx)
    grid_rows = seq // GRID_W

    x_cur = x.reshape(n_lat, d)
    ctx_rows = ctx.reshape(n_ctx, d)
    x_all = None
    for l in range(depth):
        last = l == depth - 1
        mod = mods[l, :batch + 1].reshape(batch + 1, 6, d)
        w_in_b = w_in[l].astype(BF16)
        w_s_b = sgu_w_s[l].astype(BF16)
        b_s_t = sgu_b_s[l].T
        if x_all is None:
            src = (x_cur, ctx_rows)
        else:
            src = (x_all, None)
        h, a, qp, qr, kr, v = _inproj_call(src[0], src[1], mod, norm_mix_w[l], w_in_b, sgu_norm_w[l], w_s_b, b_s_t,
                                           cos, sa, sb, rows_per_mod=seq, tm=tm)
        o = _attn_call(qr, qp, kr, v, _attn_bias_tables(na_rpb[l], grid_rows), batch=batch, seq=seq,
                       ctx_len=ctx_len, with_ctx_queries=not last)
        n_rows = n_lat if last else n_lat + n_ctx
        m = _merge_call(h, a, o, w_merge_gate[l].astype(BF16), b_merge_gate[l], w_branch_a[l].astype(BF16),
                        w_branch_b[l].astype(BF16), n_rows=n_rows, tm=tm_wide)
        w_router = jnp.zeros((d, LANES), F32)
        w_router = w_router.at[:, :n_experts].set(w_router_expert[l])
        w_router = w_router.at[:, n_experts:n_experts + N_GROUPS].set(w_router_group[l])
        w_router_hi = w_router.astype(BF16)
        w_router_lo = (w_router - w_router_hi.astype(F32)).astype(BF16)
        w_router = jnp.concatenate([w_router_hi, w_router_lo], axis=1)
        b_router = jnp.zeros((1, LANES), F32)
        b_router = b_router.at[0, :n_experts].set(b_router_expert[l])
        b_router = b_router.at[0, n_experts:n_experts + N_GROUPS].set(b_router_group[l])
        x_mid, h_packed, route, counts = _outproj_call(src[0], src[1], m, w_out[l].astype(BF16), mod, norm_ffn_w[l],
                                                       w_router, b_router, rows_per_mod=seq, tm=tm_wide,
                                                       n_experts=n_experts)
        tile_expert, n_used, dst_rows, row_stride, n_out_rows = _slot_plan(route, counts, n_experts, tm)
        yk = _ffn_call(tile_expert, n_used, dst_rows, h_packed, w_exp_gate, w_exp_up, w_exp_down, layer=l, tm=tm,
                       row_stride=row_stride, n_out_rows=n_out_rows)
        x_all = _combine_call(x_mid, yk, route, mod, final_norm_w, rows_per_mod=seq, tm=tm,
                              row_stride=row_stride, final_norm=last)
    return x_all.reshape(batch, seq, d)
```

```python
import functools

import numpy as np
import jax
import jax.numpy as jnp
from jax import lax
from jax.experimental import pallas as pl
from jax.experimental.pallas import tpu as pltpu

GRID_W = 64
SGU_CHUNK = 128
SGU_GROUPS = 8
NA_HEADS = 8
HEAD_DIM = 128
WIN_ROWS = 8
WIN_COLS = 16
ROPE_THETA = 10000.0
N_GROUPS = 4
EXPERTS_PER_GROUP = 8
RMS_EPS = 1e-6

LANES = 128
Q_ROWS = 4
Q_BLOCK = Q_ROWS * GRID_W
KEY_ROWS = Q_ROWS + WIN_ROWS - 1
HEADS_PER_STEP = 2
MASKED = -1e30
VMEM_LIMIT = 56 * 1024 * 1024

BF16 = jnp.bfloat16
F32 = jnp.float32
U32 = jnp.uint32


def _params(n_grid_dims, vmem=VMEM_LIMIT):
    return pltpu.CompilerParams(dimension_semantics=("arbitrary",) * n_grid_dims, vmem_limit_bytes=vmem)


def _resident(shape):
    nd = len(shape)
    return pl.BlockSpec(shape, lambda *_: (0,) * nd, pipeline_mode=pl.Buffered(1))


def _resident_layer(stacked, layer):
    nd = stacked.ndim
    return pl.BlockSpec((None,) + stacked.shape[1:], lambda *_: (layer,) + (0,) * (nd - 1),
                        pipeline_mode=pl.Buffered(1))


def _pack_bf16_pair(lo, hi):
    lo_bits = lax.bitcast_convert_type(lo.astype(BF16).astype(F32), U32)
    hi_bits = lax.bitcast_convert_type(hi.astype(BF16).astype(F32), U32)
    return (hi_bits & jnp.uint32(0xFFFF0000)) | (lo_bits >> 16)


def _unpack_bf16_pair(w):
    lo = lax.bitcast_convert_type(w << 16, F32)
    hi = lax.bitcast_convert_type(w & jnp.uint32(0xFFFF0000), F32)
    return lo, hi


def _rms_modulate(x, norm_w, shift, scale):
    y = x * lax.rsqrt(jnp.mean(x * x, axis=-1, keepdims=True) + RMS_EPS) * norm_w
    return y * (1.0 + scale) + shift


def _ada_body(ct_ref, w_ref, b_ref, o_ref, *, n_rows):
    s = ct_ref[...]
    s = s * jax.nn.sigmoid(s)
    w = w_ref[0]
    o_ref[...] = jnp.zeros_like(o_ref)
    for r in range(n_rows):
        o_ref[0, r:r + 1, :] = jnp.sum(w * s[:, r:r + 1], axis=0, keepdims=True) + b_ref[0]


def _ada_call(cond, w_ada, b_ada):
    n_rows, d = cond.shape
    depth, _, n = w_ada.shape
    tn = min(n, 512)
    ct = jnp.zeros((d, 8), F32).at[:, :n_rows].set(cond.T)
    return pl.pallas_call(
        functools.partial(_ada_body, n_rows=n_rows),
        grid=(depth, n // tn),
        in_specs=[pl.BlockSpec((d, 8), lambda l, j: (0, 0)),
                  pl.BlockSpec((1, d, tn), lambda l, j: (l, 0, j)),
                  pl.BlockSpec((1, 1, tn), lambda l, j: (l, 0, j))],
        out_specs=pl.BlockSpec((1, 8, tn), lambda l, j: (l, 0, j)),
        out_shape=jax.ShapeDtypeStruct((depth, 8, n), F32),
        compiler_params=_params(2),
    )(ct, w_ada, b_ada.reshape(depth, 1, n))


def _inproj_body(*refs, n_main_tiles, two_src, sgu_w, na_w):
    if two_src:
        xa_ref, xb_ref = refs[:2]
        refs = refs[2:]
    else:
        xa_ref = refs[0]
        refs = refs[1:]
    (mod_ref, nw_ref, w_ref, snw_ref, ws_ref, bst_ref, cos_ref, sa_ref, sb_ref,
     h_ref, a_ref, qp_ref, qr_ref, kr_ref, v_ref) = refs
    if two_src:
        x = jnp.where(pl.program_id(0) < n_main_tiles, xa_ref[...], xb_ref[...])
    else:
        x = xa_ref[...]
    tm = x.shape[0]
    h = _rms_modulate(x, nw_ref[...], mod_ref[0, 0:1, :], mod_ref[0, 1:2, :])
    hb = h.astype(BF16)
    h_ref[...] = hb

    def proj(lo, width):
        return jnp.dot(hb, w_ref[:, lo:lo + width], preferred_element_type=F32)

    u = jax.nn.gelu(proj(0, sgu_w))
    v = jax.nn.gelu(proj(sgu_w, sgu_w))
    vn = v * lax.rsqrt(jnp.mean(v * v, axis=-1, keepdims=True) + RMS_EPS) * snw_ref[...]
    vnb = vn.astype(BF16)
    gch = sgu_w // SGU_GROUPS
    for c in range(tm // SGU_CHUNK):
        rows = slice(c * SGU_CHUNK, (c + 1) * SGU_CHUNK)
        for g in range(SGU_GROUPS):
            cols = slice(g * gch, (g + 1) * gch)
            z = jnp.dot(ws_ref[g], vnb[rows, cols], preferred_element_type=F32) + bst_ref[:, g:g + 1]
            a_ref[rows, cols] = (u[rows, cols] * z).astype(BF16)

    cos, sa, sb = cos_ref[...], sa_ref[...], sb_ref[...]

    def rope_into(p, out_ref):
        for hh in range(NA_HEADS):
            cols = slice(hh * HEAD_DIM, (hh + 1) * HEAD_DIM)
            xh = p[:, cols]
            out_ref[:, cols] = (xh * cos + pltpu.roll(xh, HEAD_DIM - 32, 1) * sa
                                + pltpu.roll(xh, 32, 1) * sb).astype(BF16)

    q = proj(2 * sgu_w, na_w)
    qp_ref[...] = q.astype(BF16)
    rope_into(q, qr_ref)
    rope_into(proj(2 * sgu_w + na_w, na_w), kr_ref)
    v_ref[...] = proj(2 * sgu_w + 2 * na_w, na_w).astype(BF16)


def _inproj_call(x_main, x_ctx, mod, norm_w, w_in_b, sgu_norm_w, w_s_b, b_s_t, cos, sa, sb,
                 *, layer, rows_per_mod, tm):
    d = x_main.shape[1]
    n_main = x_main.shape[0]
    two_src = x_ctx is not None
    tt = n_main + (x_ctx.shape[0] if two_src else 0)
    n_main_tiles = n_main // tm
    n_mod = mod.shape[0]
    sgu_w = sgu_norm_w.shape[-1]
    na_w = NA_HEADS * HEAD_DIM
    tiles_per_mod = rows_per_mod // tm

    def row_map(i):
        return (i, 0)

    x_specs = [pl.BlockSpec((tm, d), lambda i: (jnp.minimum(i, n_main_tiles - 1), 0))]
    x_args = [x_main]
    if two_src:
        x_specs.append(pl.BlockSpec((tm, d), lambda i: (jnp.maximum(i - n_main_tiles, 0), 0)))
        x_args.append(x_ctx)
    in_specs = x_specs + [
        pl.BlockSpec((1, 6, d), lambda i: (jnp.minimum(i // tiles_per_mod, n_mod - 1), 0, 0)),
        _resident((1, d)),
        _resident_layer(w_in_b, layer),
        _resident((1, sgu_w)),
        _resident(w_s_b.shape),
        _resident(b_s_t.shape),
        pl.BlockSpec((tm, HEAD_DIM), row_map),
        pl.BlockSpec((tm, HEAD_DIM), row_map),
        pl.BlockSpec((tm, HEAD_DIM), row_map),
    ]
    out_widths = [d, sgu_w, na_w, na_w, na_w, na_w]
    return pl.pallas_call(
        functools.partial(_inproj_body, n_main_tiles=n_main_tiles, two_src=two_src, sgu_w=sgu_w, na_w=na_w),
        grid=(tt // tm,),
        in_specs=in_specs,
        out_specs=[pl.BlockSpec((tm, w), row_map) for w in out_widths],
        out_shape=[jax.ShapeDtypeStruct((tt, w), BF16) for w in out_widths],
        compiler_params=_params(1),
    )(*x_args, mod, norm_w.reshape(1, d), w_in_b, sgu_norm_w.reshape(1, sgu_w), w_s_b, b_s_t, cos, sa, sb)


def _softmax_pv(scores, values):
    m = functools.reduce(jnp.maximum, [jnp.max(s, axis=-1, keepdims=True) for s in scores])
    ps = [jnp.exp(s - m) for s in scores]
    denom = functools.reduce(jnp.add, [jnp.sum(p, axis=-1, keepdims=True) for p in ps])
    acc = functools.reduce(jnp.add, [jnp.dot(p.astype(BF16), v, preferred_element_type=F32)
                                     for p, v in zip(ps, values)])
    return acc / denom


def _qk(q, k):
    return lax.dot_general(q, k, (((1,), (1,)), ((), ())), preferred_element_type=F32) * (HEAD_DIM ** -0.5)


def _attn_row_structure(blk, grid_rows):
    wr = min(WIN_ROWS, grid_rows)
    k_row0 = int(np.clip(blk * Q_ROWS - WIN_ROWS // 2, 0, grid_rows - KEY_ROWS))
    qr = blk * Q_ROWS + np.arange(Q_ROWS)
    kr = k_row0 + np.arange(KEY_ROWS)
    r0 = np.clip(qr - wr // 2, 0, grid_rows - wr)
    valid = (kr[None, :] >= r0[:, None]) & (kr[None, :] < r0[:, None] + wr)
    ri = np.clip(kr[None, :] - qr[:, None] + (WIN_ROWS - 1), 0, 2 * WIN_ROWS - 2)
    return valid, ri


def _attn_body(qr_ref, qp_ref, k_ref, v_ref, kc_ref, vc_ref, bcol_ref, o_ref, bias_ref, *, n_blocks, grid_rows):
    i = pl.program_id(2)
    n_keys = KEY_ROWS * GRID_W

    heads = [slice(hh * HEAD_DIM, (hh + 1) * HEAD_DIM) for hh in range(HEADS_PER_STEP)]

    def build_bias(blk):
        valid, ri = _attn_row_structure(blk, grid_rows)
        for hh in range(HEADS_PER_STEP):
            for jr in range(Q_ROWS):
                for kl in range(KEY_ROWS):
                    piece = (bcol_ref[hh, int(ri[jr, kl])] if valid[jr, kl]
                             else jnp.full((GRID_W, GRID_W), MASKED, F32))
                    bias_ref[hh, jr * GRID_W:(jr + 1) * GRID_W, kl * GRID_W:(kl + 1) * GRID_W] = piece

    for blk in sorted({0, min(1, n_blocks - 1), n_blocks - 1}):
        pl.when(i == blk)(functools.partial(build_bias, blk))

    @pl.when(i < n_blocks)
    def _():
        k_row0 = jnp.clip(i * Q_ROWS - WIN_ROWS // 2, 0, grid_rows - KEY_ROWS)
        start = pl.multiple_of(k_row0 * GRID_W, GRID_W)
        for hh, cols in enumerate(heads):
            k_loc = k_ref[pl.ds(start, n_keys), cols]
            v_loc = v_ref[pl.ds(start, n_keys), cols]
            s_loc = _qk(qr_ref[:, cols], k_loc) + bias_ref[hh]
            s_ctx = _qk(qp_ref[:, cols], kc_ref[:, cols])
            o_ref[:, cols] = _softmax_pv([s_loc, s_ctx], [v_loc, vc_ref[:, cols]]).astype(BF16)

    @pl.when(i >= n_blocks)
    def _():
        for cols in heads:
            s_ctx = _qk(qp_ref[:, cols], kc_ref[:, cols])
            o_ref[:, cols] = _softmax_pv([s_ctx], [vc_ref[:, cols]]).astype(BF16)


def _attn_bias_by_column(rpb, grid_rows):
    n_blocks = grid_rows // Q_ROWS
    assert min(WIN_ROWS, grid_rows) == WIN_ROWS and grid_rows >= KEY_ROWS and grid_rows % Q_ROWS == 0
    interior = _attn_row_structure(min(1, n_blocks - 1), grid_rows)
    for blk in range(1, n_blocks - 1):
        assert all(np.array_equal(a, b) for a, b in zip(_attn_row_structure(blk, grid_rows), interior))
    n_ci = 2 * WIN_COLS - 1
    qc = np.arange(GRID_W)
    c0 = np.clip(qc - WIN_COLS // 2, 0, GRID_W - WIN_COLS)
    col_valid = (qc[None, :] >= c0[:, None]) & (qc[None, :] < c0[:, None] + WIN_COLS)
    col_sel = (col_valid[:, :, None]
               & ((qc[None, :, None] - qc[:, None, None] + (WIN_COLS - 1)) == np.arange(n_ci))).astype(np.float32)
    by_col = jnp.einsum('hrc,qkc->hrqk', rpb.astype(F32), jnp.asarray(col_sel), precision=lax.Precision.HIGHEST)
    return jnp.where(jnp.asarray(col_valid)[None, None], by_col, MASKED)


def _attn_call(qr, qp, kr, v, bias_by_col, *, batch, seq, ctx_len, with_ctx_queries):
    assert ctx_len == Q_BLOCK and seq % Q_BLOCK == 0
    grid_rows = seq // GRID_W
    n_blocks = grid_rows // Q_ROWS
    n_steps = n_blocks + (1 if with_ctx_queries else 0)
    ctx_block0 = batch * n_blocks
    n_out = batch * seq + (batch * ctx_len if with_ctx_queries else 0)
    n_keys = KEY_ROWS * GRID_W

    def q_map(b, h, i):
        return (jnp.where(i < n_blocks, b * n_blocks + i, ctx_block0 + b), h)

    width = HEADS_PER_STEP * HEAD_DIM
    return pl.pallas_call(
        functools.partial(_attn_body, n_blocks=n_blocks, grid_rows=grid_rows),
        grid=(batch, NA_HEADS // HEADS_PER_STEP, n_steps),
        in_specs=[pl.BlockSpec((Q_BLOCK, width), q_map),
                  pl.BlockSpec((Q_BLOCK, width), q_map),
                  pl.BlockSpec((seq, width), lambda b, h, i: (b, h)),
                  pl.BlockSpec((seq, width), lambda b, h, i: (b, h)),
                  pl.BlockSpec((ctx_len, width), lambda b, h, i: (ctx_block0 + b, h)),
                  pl.BlockSpec((ctx_len, width), lambda b, h, i: (ctx_block0 + b, h)),
                  pl.BlockSpec((HEADS_PER_STEP,) + bias_by_col.shape[1:], lambda b, h, i: (h, 0, 0, 0))],
        out_specs=pl.BlockSpec((Q_BLOCK, width), q_map),
        out_shape=jax.ShapeDtypeStruct((n_out, NA_HEADS * HEAD_DIM), BF16),
        scratch_shapes=[pltpu.VMEM((HEADS_PER_STEP, Q_BLOCK, n_keys), F32)],
        compiler_params=_params(3),
    )(qr, qp, kr, v, kr, v, bias_by_col)


def _merge_body(h_ref, a_ref, o_ref, wg_ref, bg_ref, wa_ref, wb_ref, m_ref, *, tn):
    d = m_ref.shape[1]
    hb, ab, ob = h_ref[...], a_ref[...], o_ref[...]
    for n0 in range(0, d, tn):
        cols = slice(n0, n0 + tn)
        gcols = slice(d + n0, d + n0 + tn)
        g_a = jax.nn.sigmoid(jnp.dot(hb, wg_ref[:, cols], preferred_element_type=F32) + bg_ref[:, cols])
        g_b = jax.nn.sigmoid(jnp.dot(hb, wg_ref[:, gcols], preferred_element_type=F32) + bg_ref[:, gcols])
        pa = jnp.dot(ab, wa_ref[:, cols], preferred_element_type=F32)
        pb = jnp.dot(ob, wb_ref[:, cols], preferred_element_type=F32)
        m_ref[:, cols] = (g_a * pa + g_b * pb).astype(BF16)


def _merge_call(h, a, o, w_gate_b, b_gate, w_a_b, w_b_b, *, layer, n_rows, tm):
    d = h.shape[1]

    def row_map(i):
        return (i, 0)

    return pl.pallas_call(
        functools.partial(_merge_body, tn=min(d, 512)),
        grid=(n_rows // tm,),
        in_specs=[pl.BlockSpec((tm, d), row_map),
                  pl.BlockSpec((tm, a.shape[1]), row_map),
                  pl.BlockSpec((tm, o.shape[1]), row_map),
                  _resident_layer(w_gate_b, layer), _resident((1, 2 * d)),
                  _resident_layer(w_a_b, layer), _resident_layer(w_b_b, layer)],
        out_specs=pl.BlockSpec((tm, d), row_map),
        out_shape=jax.ShapeDtypeStruct((n_rows, d), BF16),
        compiler_params=_params(1),
    )(h, a, o, w_gate_b, b_gate.reshape(1, 2 * d), w_a_b, w_b_b)


def _outproj_body(*refs, n_main_tiles, two_src, n_experts):
    if two_src:
        xa_ref, xb_ref = refs[:2]
        refs = refs[2:]
    else:
        xa_ref = refs[0]
        refs = refs[1:]
    (m_ref, wo_ref, mod_ref, nw_ref, wr_ref, br_ref,
     xo_ref, hp_ref, route_ref, route_t_ref, cnt_ref, carry_ref) = refs
    i = pl.program_id(0)
    if two_src:
        x = jnp.where(i < n_main_tiles, xa_ref[...], xb_ref[...])
    else:
        x = xa_ref[...]
    tm, d = x.shape

    @pl.when(i == 0)
    def _():
        carry_ref[...] = jnp.zeros_like(carry_ref)

    y = jnp.dot(m_ref[...], wo_ref[...], preferred_element_type=F32)
    x_new = x + mod_ref[0, 2:3, :] * y
    xo_ref[...] = x_new
    h = _rms_modulate(x_new, nw_ref[...], mod_ref[0, 3:4, :], mod_ref[0, 4:5, :])
    hp_ref[...] = _to_token_tiles(_pack_bf16_pair(h[:, :d // 2], h[:, d // 2:]))

    h_hi = h.astype(BF16)
    h_lo = (h - h_hi.astype(F32)).astype(BF16)
    by_hi = jnp.dot(h_hi, wr_ref[...], preferred_element_type=F32)
    logits = (by_hi[:, :LANES] + by_hi[:, LANES:]
              + jnp.dot(h_lo, wr_ref[:, :LANES], preferred_element_type=F32) + br_ref[...])
    lane = lax.broadcasted_iota(jnp.int32, logits.shape, 1)
    lane_f = lane.astype(F32)
    far = jnp.float32(4 * LANES)

    def first_argmax(vals):
        top = jnp.max(vals, axis=-1, keepdims=True)
        return top, jnp.min(jnp.where(vals == top, lane_f, far), axis=-1, keepdims=True)

    g_logits = jnp.where((lane >= n_experts) & (lane < n_experts + N_GROUPS), logits, MASKED)
    g_top, g_lane = first_argmax(g_logits)
    g_prob = 1.0 / jnp.sum(jnp.exp(g_logits - g_top), axis=-1, keepdims=True)
    e_lo = (g_lane - n_experts) * EXPERTS_PER_GROUP
    e_logits = jnp.where((lane_f >= e_lo) & (lane_f < e_lo + EXPERTS_PER_GROUP), logits, MASKED)
    top1, e1 = first_argmax(e_logits)
    top2, e2 = first_argmax(jnp.where(lane_f == e1, MASKED, e_logits))
    t = jnp.exp(top2 - top1)
    w1 = g_prob / (1.0 + t)
    w2 = g_prob * t / (1.0 + t)

    sel1, sel2 = lane_f == e1, lane_f == e2
    onehot = jnp.where(sel1 | sel2, 1.0, 0.0)
    r_i = lax.broadcasted_iota(jnp.int32, (tm, tm), 0)
    c_i = lax.broadcasted_iota(jnp.int32, (tm, tm), 1)
    earlier = jnp.where(c_i < r_i, 1.0, 0.0).astype(BF16)
    before = jnp.dot(earlier, onehot.astype(BF16), preferred_element_type=F32) + carry_ref[...]
    rank1 = jnp.sum(jnp.where(sel1, before, 0.0), axis=-1, keepdims=True)
    rank2 = jnp.sum(jnp.where(sel2, before, 0.0), axis=-1, keepdims=True)
    carry_ref[...] += jnp.sum(onehot, axis=0, keepdims=True)
    cnt_ref[...] = carry_ref[...]

    route = jnp.zeros_like(logits)
    for k, val in enumerate((e1, e2, w1, w2, rank1, rank2)):
        route = jnp.where(lane == k, val, route)
    route_ref[...] = route
    route_t_ref[...] = route.T[:8, :]


def _outproj_call(x_main, x_ctx, m, w_out_b, mod, norm_w, w_router, b_router, *, layer, rows_per_mod, tm,
                  n_experts):
    d = x_main.shape[1]
    two_src = x_ctx is not None
    n_main = x_main.shape[0] if two_src else m.shape[0]
    n_rows = m.shape[0]
    n_main_tiles = n_main // tm
    n_mod = mod.shape[0]
    tiles_per_mod = rows_per_mod // tm

    def row_map(i):
        return (i, 0)

    x_specs = [pl.BlockSpec((tm, d), lambda i: (jnp.minimum(i, n_main_tiles - 1), 0))]
    x_args = [x_main]
    if two_src:
        x_specs.append(pl.BlockSpec((tm, d), lambda i: (jnp.maximum(i - n_main_tiles, 0), 0)))
        x_args.append(x_ctx)
    in_specs = x_specs + [
        pl.BlockSpec((tm, d), row_map),
        _resident_layer(w_out_b, layer),
        pl.BlockSpec((1, 6, d), lambda i: (jnp.minimum(i // tiles_per_mod, n_mod - 1), 0, 0)),
        _resident((1, d)),
        _resident(w_router.shape),
        _resident((1, LANES)),
    ]
    return pl.pallas_call(
        functools.partial(_outproj_body, n_main_tiles=n_main_tiles, two_src=two_src, n_experts=n_experts),
        grid=(n_rows // tm,),
        in_specs=in_specs,
        out_specs=[pl.BlockSpec((tm, d), row_map),
                   pl.BlockSpec((tm, d // 2 // LANES, LANES), lambda i: (i, 0, 0)),
                   pl.BlockSpec((tm, LANES), row_map),
                   pl.BlockSpec((8, tm), lambda i: (0, i)),
                   pl.BlockSpec((1, LANES), lambda i: (0, 0))],
        out_shape=[jax.ShapeDtypeStruct((n_rows, d), F32),
                   jax.ShapeDtypeStruct((n_rows, d // 2 // LANES, LANES), U32),
                   jax.ShapeDtypeStruct((n_rows, LANES), F32),
                   jax.ShapeDtypeStruct((8, n_rows), F32),
                   jax.ShapeDtypeStruct((1, LANES), F32)],
        scratch_shapes=[pltpu.VMEM((1, LANES), F32)],
        compiler_params=_params(1),
    )(*x_args, m, w_out_b, mod, norm_w.reshape(1, d), w_router, b_router)


def _row_copy(src_ref, src_row, dst_ref, dst_row, sem):
    return pltpu.make_async_copy(src_ref.at[src_row], dst_ref.at[dst_row], sem)


def _to_token_tiles(rows):
    return rows.reshape(rows.shape[0], rows.shape[1] // LANES, LANES)


def _from_token_tiles(tiles):
    return tiles.reshape(tiles.shape[0], tiles.shape[1] * LANES)


def _slot_rows_body(p1_ref, p2_ref, init_ref, dst_ref, stage_ref, sem, *, tokens_per_step, row_stride):
    i = pl.program_id(0)

    @pl.when(i == 0)
    def _():
        load = pltpu.make_async_copy(init_ref, stage_ref, sem)
        load.start()
        load.wait()

    def put(r, carry):
        t = i * tokens_per_step + r
        stage_ref[p1_ref[t]] = t
        stage_ref[p2_ref[t]] = row_stride + t
        return carry

    lax.fori_loop(0, tokens_per_step, put, 0, unroll=8)

    @pl.when(i == pl.num_programs(0) - 1)
    def _():
        store = pltpu.make_async_copy(stage_ref, dst_ref, sem)
        store.start()
        store.wait()


def _slot_rows_call(pos1, pos2, init_rows, *, row_stride, tokens_per_step):
    n_tok = pos1.shape[0]
    n_map = init_rows.shape[0]
    return pl.pallas_call(
        functools.partial(_slot_rows_body, tokens_per_step=tokens_per_step, row_stride=row_stride),
        grid_spec=pltpu.PrefetchScalarGridSpec(
            num_scalar_prefetch=2,
            grid=(n_tok // tokens_per_step,),
            in_specs=[pl.BlockSpec(memory_space=pl.ANY)],
            out_specs=pl.BlockSpec(memory_space=pl.ANY),
            scratch_shapes=[pltpu.SMEM((n_map,), jnp.int32), pltpu.SemaphoreType.DMA(())],
        ),
        out_shape=jax.ShapeDtypeStruct((n_map,), jnp.int32),
        compiler_params=_params(1),
    )(pos1, pos2, init_rows)


def _ffn_body(te_ref, nu_ref, src_ref, dst_ref, h_ref, wg_ref, wu_ref, wd_ref, yk_ref,
              wgb_ref, wub_ref, wdb_ref, xbuf, ybuf, sem_g, sem_s, *, tm, spare_rows):
    j = pl.program_id(0)
    n_used = nu_ref[0]
    slot = j % 2
    other = 1 - slot
    half = xbuf.shape[2] * LANES

    def gather_row(tile, r, buf):
        return _row_copy(h_ref, src_ref[(tile + 1) * tm + r], xbuf.at[buf], r, sem_g.at[buf])

    def scatter_row(tile, r, buf):
        return _row_copy(ybuf.at[buf], r, yk_ref, dst_ref[(tile + 1) * tm + r], sem_s.at[buf])

    def wait_tile(buf_ref, sem):
        pltpu.make_async_copy(buf_ref, buf_ref, sem).wait()

    @pl.when(j == 0)
    def _():
        ybuf[...] = jnp.zeros_like(ybuf)

        def prime(r, carry):
            _row_copy(ybuf.at[0], r, yk_ref, spare_rows + r, sem_s.at[0]).start()
            gather_row(0, r, 0).start()
            return carry

        lax.fori_loop(0, tm, prime, 0)

    @pl.when((j == 0) | (te_ref[j] != te_ref[jnp.maximum(j - 1, 0)]))
    def _():
        wgb_ref[...] = wg_ref[0, 0].astype(BF16)
        wub_ref[...] = wu_ref[0, 0].astype(BF16)
        wdb_ref[...] = wd_ref[0, 0].astype(BF16)

    @pl.when(j < n_used)
    def _():
        wait_tile(xbuf.at[slot], sem_g.at[slot])
        lo, hi = _unpack_bf16_pair(_from_token_tiles(xbuf[slot]))
        lo, hi = lo.astype(BF16), hi.astype(BF16)
        wait_tile(ybuf.at[slot], sem_s.at[slot])
        for r in range(tm):
            gather_row(j + 1, r, other).start()
            scatter_row(j - 1, r, other).start()

        def up(w_ref):
            return (jnp.dot(lo, w_ref[:half, :], preferred_element_type=F32)
                    + jnp.dot(hi, w_ref[half:, :], preferred_element_type=F32))

        g = up(wgb_ref)
        hid = (g * jax.nn.sigmoid(g) * up(wub_ref)).astype(BF16)
        y = jnp.dot(hid, wdb_ref[...], preferred_element_type=F32)
        ybuf[slot] = _to_token_tiles(_pack_bf16_pair(y[:, :half], y[:, half:]))

    @pl.when(j == n_used - 1)
    def _():
        def flush(r, carry):
            scatter_row(j, r, slot).start()
            return carry

        wait_tile(ybuf.at[other], sem_s.at[other])
        lax.fori_loop(0, tm, flush, 0)
        wait_tile(ybuf.at[slot], sem_s.at[slot])
        wait_tile(xbuf.at[other], sem_g.at[other])


def _ffn_call(tile_expert, n_used, dst_rows, h_packed, w_gate, w_up, w_down, *, layer, tm, row_stride, n_out_rows):
    _, sub, _ = h_packed.shape
    _, _, d, de = w_gate.shape
    n_tiles = tile_expert.shape[0]
    src_rows = jnp.where(dst_rows < 2 * row_stride, dst_rows % row_stride, 0)

    def w_map(j, te, nu, sr, dr):
        return (layer, te[j], 0, 0)

    return pl.pallas_call(
        functools.partial(_ffn_body, tm=tm, spare_rows=2 * row_stride),
        grid_spec=pltpu.PrefetchScalarGridSpec(
            num_scalar_prefetch=4,
            grid=(n_tiles,),
            in_specs=[pl.BlockSpec(memory_space=pl.ANY),
                      pl.BlockSpec((1, 1, d, de), w_map),
                      pl.BlockSpec((1, 1, d, de), w_map),
                      pl.BlockSpec((1, 1, de, d), w_map)],
            out_specs=pl.BlockSpec(memory_space=pl.ANY),
            scratch_shapes=[pltpu.VMEM((d, de), BF16), pltpu.VMEM((d, de), BF16), pltpu.VMEM((de, d), BF16),
                            pltpu.VMEM((2, tm, sub, LANES), U32), pltpu.VMEM((2, tm, sub, LANES), U32),
                            pltpu.SemaphoreType.DMA((2,)), pltpu.SemaphoreType.DMA((2,))],
        ),
        out_shape=jax.ShapeDtypeStruct((n_out_rows, sub, LANES), U32),
        compiler_params=_params(1),
    )(tile_expert, n_used, src_rows, dst_rows, h_packed, w_gate, w_up, w_down)


def _combine_body(x_ref, y1_ref, y2_ref, route_ref, mod_ref, fw_ref, o_ref, *, final_norm):
    w1, w2 = route_ref[:, 2:3], route_ref[:, 3:4]
    lo1, hi1 = _unpack_bf16_pair(_from_token_tiles(y1_ref[...]))
    lo2, hi2 = _unpack_bf16_pair(_from_token_tiles(y2_ref[...]))
    y = jnp.concatenate([w1 * lo1 + w2 * lo2, w1 * hi1 + w2 * hi2], axis=-1)
    x_new = x_ref[...] + mod_ref[0, 5:6, :] * y
    if final_norm:
        x_new = x_new * lax.rsqrt(jnp.mean(x_new * x_new, axis=-1, keepdims=True) + RMS_EPS) * fw_ref[...]
    o_ref[...] = x_new


def _combine_call(x, yk, route, mod, final_w, *, rows_per_mod, tm, row_stride, final_norm):
    n_rows, d = x.shape
    sub = yk.shape[1]
    n_mod = mod.shape[0]
    tiles_per_mod = rows_per_mod // tm
    second = row_stride // tm

    def row_map(i):
        return (i, 0)

    return pl.pallas_call(
        functools.partial(_combine_body, final_norm=final_norm),
        grid=(n_rows // tm,),
        in_specs=[pl.BlockSpec((tm, d), row_map),
                  pl.BlockSpec((tm, sub, LANES), lambda i: (i, 0, 0)),
                  pl.BlockSpec((tm, sub, LANES), lambda i: (second + i, 0, 0)),
                  pl.BlockSpec((tm, LANES), row_map),
                  pl.BlockSpec((1, 6, d), lambda i: (jnp.minimum(i // tiles_per_mod, n_mod - 1), 0, 0)),
                  pl.BlockSpec((1, d), lambda i: (0, 0))],
        out_specs=pl.BlockSpec((tm, d), row_map),
        out_shape=jax.ShapeDtypeStruct((n_rows, d), F32),
        compiler_params=_params(1),
    )(x, yk, yk, route, mod, final_w.reshape(1, d))


def _slot_plan(route_t, counts, n_experts, tm):
    n_rows = route_t.shape[1]
    n_tiles = (2 * n_rows) // tm + n_experts
    cnt = counts[0, :n_experts].astype(jnp.int32)
    padded = ((cnt + tm - 1) // tm) * tm
    ends = jnp.cumsum(padded)
    starts = ends - padded
    fields = route_t.astype(jnp.int32)
    pos1 = starts[fields[0]] + fields[4]
    pos2 = starts[fields[1]] + fields[5]
    n_used = ends[-1] // tm
    tile_start = jnp.arange(n_tiles, dtype=jnp.int32) * tm
    tile_expert = jnp.sum((ends[None, :] <= tile_start[:, None]).astype(jnp.int32), axis=1)
    last_used = tile_expert[jnp.maximum(n_used - 1, 0)]
    tile_expert = jnp.where(tile_start < ends[-1], tile_expert, last_used)
    tile_expert = jnp.minimum(tile_expert, n_experts - 1).astype(jnp.int32)
    row_stride = n_rows
    spare = 2 * row_stride
    n_map = pl.cdiv((n_tiles + 2) * tm, 1024) * 1024
    init_rows = spare + jnp.arange(n_map, dtype=jnp.int32) % tm
    dst_rows = _slot_rows_call(pos1 + tm, pos2 + tm, init_rows, row_stride=row_stride, tokens_per_step=2 * tm)
    n_out_rows = spare + tm
    return (tile_expert, n_used.reshape(1).astype(jnp.int32), dst_rows, row_stride, n_out_rows)


def _rope_tables(batch, seq, ctx_rows):
    pairs = HEAD_DIM // 4
    t = np.arange(seq)
    pos = np.stack([t // GRID_W, t % GRID_W], axis=-1).astype(np.float32)
    inv_freq = (ROPE_THETA ** (-np.arange(pairs, dtype=np.float32) / pairs)).astype(np.float32)
    ang = pos[:, :, None] * inv_freq
    cos = np.repeat(np.cos(ang)[:, :, None, :], 2, axis=2).reshape(seq, HEAD_DIM)
    sin = np.sin(ang)
    zero = np.zeros_like(sin)
    sa = np.stack([-sin, zero], axis=2).reshape(seq, HEAD_DIM)
    sb = np.stack([zero, sin], axis=2).reshape(seq, HEAD_DIM)

    def full(tab, fill):
        return jnp.asarray(np.concatenate([np.tile(tab, (batch, 1)),
                                           np.full((ctx_rows, HEAD_DIM), fill, np.float32)]), F32)

    return full(cos, 1.0), full(sa, 0.0), full(sb, 0.0)


def kernel(x, c, ctx, c_ctx, w_ada, b_ada, norm_mix_w, norm_ffn_w, w_in, sgu_norm_w, sgu_w_s, sgu_b_s, na_rpb,
           w_merge_gate, b_merge_gate, w_branch_a, w_branch_b, w_out, w_router_group, b_router_group,
           w_router_expert, b_router_expert, w_exp_gate, w_exp_up, w_exp_down, final_norm_w):
    batch, seq, d = x.shape
    ctx_len = ctx.shape[1]
    depth = w_ada.shape[0]
    n_experts = w_exp_gate.shape[1]
    n_lat = batch * seq
    n_ctx = batch * ctx_len
    tm = 256
    tm_wide = 512
    assert seq % tm_wide == 0 and n_ctx % tm_wide == 0 and tm % SGU_CHUNK == 0 and ctx_len % SGU_CHUNK == 0

    mods = _ada_call(jnp.concatenate([c, c_ctx[None]], axis=0), w_ada, b_ada)
    cos, sa, sb = _rope_tables(batch, seq, n_ctx)
    grid_rows = seq // GRID_W

    x_cur = x.reshape(n_lat, d)
    ctx_rows = ctx.reshape(n_ctx, d)
    w_in_b, w_gate_b, w_a_b, w_b_b, w_out_b = (w.astype(BF16) for w in
                                               (w_in, w_merge_gate, w_branch_a, w_branch_b, w_out))
    x_all = None
    for l in range(depth):
        last = l == depth - 1
        mod = mods[l, :batch + 1].reshape(batch + 1, 6, d)
        w_s_b = sgu_w_s[l].astype(BF16)
        b_s_t = sgu_b_s[l].T
        if x_all is None:
            src = (x_cur, ctx_rows)
        else:
            src = (x_all, None)
        h, a, qp, qr, kr, v = _inproj_call(src[0], src[1], mod, norm_mix_w[l], w_in_b, sgu_norm_w[l], w_s_b, b_s_t,
                                           cos, sa, sb, layer=l, rows_per_mod=seq, tm=tm)
        o = _attn_call(qr, qp, kr, v, _attn_bias_by_column(na_rpb[l], grid_rows), batch=batch, seq=seq,
                       ctx_len=ctx_len, with_ctx_queries=not last)
        n_rows = n_lat if last else n_lat + n_ctx
        m = _merge_call(h, a, o, w_gate_b, b_merge_gate[l], w_a_b, w_b_b, layer=l, n_rows=n_rows, tm=tm_wide)
        w_router = jnp.zeros((d, LANES), F32)
        w_router = w_router.at[:, :n_experts].set(w_router_expert[l])
        w_router = w_router.at[:, n_experts:n_experts + N_GROUPS].set(w_router_group[l])
        w_router_hi = w_router.astype(BF16)
        w_router_lo = (w_router - w_router_hi.astype(F32)).astype(BF16)
        w_router = jnp.concatenate([w_router_hi, w_router_lo], axis=1)
        b_router = jnp.zeros((1, LANES), F32)
        b_router = b_router.at[0, :n_experts].set(b_router_expert[l])
        b_router = b_router.at[0, n_experts:n_experts + N_GROUPS].set(b_router_group[l])
        x_mid, h_packed, route, route_t, counts = _outproj_call(
            src[0], src[1], m, w_out_b, mod, norm_ffn_w[l], w_router, b_router, layer=l, rows_per_mod=seq,
            tm=tm_wide, n_experts=n_experts)
        tile_expert, n_used, dst_rows, row_stride, n_out_rows = _slot_plan(route_t, counts, n_experts, tm)
        yk = _ffn_call(tile_expert, n_used, dst_rows, h_packed, w_exp_gate, w_exp_up, w_exp_down, layer=l, tm=tm,
                       row_stride=row_stride, n_out_rows=n_out_rows)
        x_all = _combine_call(x_mid, yk, route, mod, final_norm_w, rows_per_mod=seq, tm=tm,
                              row_stride=row_stride, final_norm=last)
    return x_all.reshape(batch, seq, d)
```

```python
import functools

import numpy as np
import jax
import jax.numpy as jnp
from jax import lax
from jax.experimental import pallas as pl
from jax.experimental.pallas import tpu as pltpu

GRID_W = 64
SGU_CHUNK = 128
SGU_GROUPS = 8
NA_HEADS = 8
HEAD_DIM = 128
WIN_ROWS = 8
WIN_COLS = 16
ROPE_THETA = 10000.0
N_GROUPS = 4
EXPERTS_PER_GROUP = 8
RMS_EPS = 1e-6

LANES = 128
Q_ROWS = 4
Q_BLOCK = Q_ROWS * GRID_W
KEY_ROWS = Q_ROWS + WIN_ROWS - 1
HEADS_PER_STEP = 4
MASKED = -1e30
VMEM_LIMIT = 56 * 1024 * 1024

BF16 = jnp.bfloat16
F32 = jnp.float32
U32 = jnp.uint32


def _params(n_grid_dims, vmem=VMEM_LIMIT):
    return pltpu.CompilerParams(dimension_semantics=("arbitrary",) * n_grid_dims, vmem_limit_bytes=vmem)


def _resident(shape):
    nd = len(shape)
    return pl.BlockSpec(shape, lambda *_: (0,) * nd, pipeline_mode=pl.Buffered(1))


def _resident_layer(stacked, layer):
    nd = stacked.ndim
    return pl.BlockSpec((None,) + stacked.shape[1:], lambda *_: (layer,) + (0,) * (nd - 1),
                        pipeline_mode=pl.Buffered(1))


def _pack_bf16_pair(lo, hi):
    lo_bits = lax.bitcast_convert_type(lo.astype(BF16).astype(F32), U32)
    hi_bits = lax.bitcast_convert_type(hi.astype(BF16).astype(F32), U32)
    return (hi_bits & jnp.uint32(0xFFFF0000)) | (lo_bits >> 16)


def _unpack_bf16_pair(w):
    lo = lax.bitcast_convert_type(w << 16, F32)
    hi = lax.bitcast_convert_type(w & jnp.uint32(0xFFFF0000), F32)
    return lo, hi


def _rms_modulate(x, norm_w, shift, scale):
    y = x * lax.rsqrt(jnp.mean(x * x, axis=-1, keepdims=True) + RMS_EPS) * norm_w
    return y * (1.0 + scale) + shift


def _ada_body(ct_ref, w_ref, b_ref, o_ref, *, n_rows):
    s = ct_ref[...]
    s = s * jax.nn.sigmoid(s)
    w = w_ref[0]
    o_ref[...] = jnp.zeros_like(o_ref)
    for r in range(n_rows):
        o_ref[0, r:r + 1, :] = jnp.sum(w * s[:, r:r + 1], axis=0, keepdims=True) + b_ref[0]


def _ada_call(cond, w_ada, b_ada):
    n_rows, d = cond.shape
    depth, _, n = w_ada.shape
    tn = min(n, 512)
    ct = jnp.zeros((d, 8), F32).at[:, :n_rows].set(cond.T)
    return pl.pallas_call(
        functools.partial(_ada_body, n_rows=n_rows),
        grid=(depth, n // tn),
        in_specs=[pl.BlockSpec((d, 8), lambda l, j: (0, 0)),
                  pl.BlockSpec((1, d, tn), lambda l, j: (l, 0, j)),
                  pl.BlockSpec((1, 1, tn), lambda l, j: (l, 0, j))],
        out_specs=pl.BlockSpec((1, 8, tn), lambda l, j: (l, 0, j)),
        out_shape=jax.ShapeDtypeStruct((depth, 8, n), F32),
        compiler_params=_params(2),
    )(ct, w_ada, b_ada.reshape(depth, 1, n))


def _inproj_body(*refs, n_main_tiles, two_src, sgu_w, na_w):
    if two_src:
        xa_ref, xb_ref = refs[:2]
        refs = refs[2:]
    else:
        xa_ref = refs[0]
        refs = refs[1:]
    (mod_ref, nw_ref, w_ref, snw_ref, ws_ref, bst_ref, cos_ref, sa_ref, sb_ref,
     h_ref, a_ref, qp_ref, qr_ref, kr_ref, v_ref) = refs
    if two_src:
        x = jnp.where(pl.program_id(0) < n_main_tiles, xa_ref[...], xb_ref[...])
    else:
        x = xa_ref[...]
    tm = x.shape[0]
    h = _rms_modulate(x, nw_ref[...], mod_ref[0, 0:1, :], mod_ref[0, 1:2, :])
    hb = h.astype(BF16)
    h_ref[...] = hb

    def proj(lo, width):
        return jnp.dot(hb, w_ref[:, lo:lo + width], preferred_element_type=F32)

    u = jax.nn.gelu(proj(0, sgu_w))
    v = jax.nn.gelu(proj(sgu_w, sgu_w))
    vn = v * lax.rsqrt(jnp.mean(v * v, axis=-1, keepdims=True) + RMS_EPS) * snw_ref[...]
    vnb = vn.astype(BF16)
    gch = sgu_w // SGU_GROUPS
    for c in range(tm // SGU_CHUNK):
        rows = slice(c * SGU_CHUNK, (c + 1) * SGU_CHUNK)
        for g in range(SGU_GROUPS):
            cols = slice(g * gch, (g + 1) * gch)
            z = jnp.dot(ws_ref[g], vnb[rows, cols], preferred_element_type=F32) + bst_ref[:, g:g + 1]
            a_ref[rows, cols] = (u[rows, cols] * z).astype(BF16)

    cos, sa, sb = cos_ref[...], sa_ref[...], sb_ref[...]

    def rope_into(p, out_ref):
        for hh in range(NA_HEADS):
            cols = slice(hh * HEAD_DIM, (hh + 1) * HEAD_DIM)
            xh = p[:, cols]
            out_ref[:, cols] = (xh * cos + pltpu.roll(xh, HEAD_DIM - 32, 1) * sa
                                + pltpu.roll(xh, 32, 1) * sb).astype(BF16)

    q = proj(2 * sgu_w, na_w)
    qp_ref[...] = q.astype(BF16)
    rope_into(q, qr_ref)
    rope_into(proj(2 * sgu_w + na_w, na_w), kr_ref)
    v_ref[...] = proj(2 * sgu_w + 2 * na_w, na_w).astype(BF16)


def _inproj_call(x_main, x_ctx, mod, norm_w, w_in_b, sgu_norm_w, w_s_b, b_s_t, cos, sa, sb,
                 *, layer, rows_per_mod, tm):
    d = x_main.shape[1]
    n_main = x_main.shape[0]
    two_src = x_ctx is not None
    tt = n_main + (x_ctx.shape[0] if two_src else 0)
    n_main_tiles = n_main // tm
    n_mod = mod.shape[0]
    sgu_w = sgu_norm_w.shape[-1]
    na_w = NA_HEADS * HEAD_DIM
    tiles_per_mod = rows_per_mod // tm

    def row_map(i):
        return (i, 0)

    x_specs = [pl.BlockSpec((tm, d), lambda i: (jnp.minimum(i, n_main_tiles - 1), 0))]
    x_args = [x_main]
    if two_src:
        x_specs.append(pl.BlockSpec((tm, d), lambda i: (jnp.maximum(i - n_main_tiles, 0), 0)))
        x_args.append(x_ctx)
    in_specs = x_specs + [
        pl.BlockSpec((1, 6, d), lambda i: (jnp.minimum(i // tiles_per_mod, n_mod - 1), 0, 0)),
        _resident((1, d)),
        _resident_layer(w_in_b, layer),
        _resident((1, sgu_w)),
        _resident(w_s_b.shape),
        _resident(b_s_t.shape),
        pl.BlockSpec((tm, HEAD_DIM), row_map),
        pl.BlockSpec((tm, HEAD_DIM), row_map),
        pl.BlockSpec((tm, HEAD_DIM), row_map),
    ]
    out_widths = [d, sgu_w, na_w, na_w, na_w, na_w]
    return pl.pallas_call(
        functools.partial(_inproj_body, n_main_tiles=n_main_tiles, two_src=two_src, sgu_w=sgu_w, na_w=na_w),
        grid=(tt // tm,),
        in_specs=in_specs,
        out_specs=[pl.BlockSpec((tm, w), row_map) for w in out_widths],
        out_shape=[jax.ShapeDtypeStruct((tt, w), BF16) for w in out_widths],
        compiler_params=_params(1),
    )(*x_args, mod, norm_w.reshape(1, d), w_in_b, sgu_norm_w.reshape(1, sgu_w), w_s_b, b_s_t, cos, sa, sb)


def _softmax_pv(scores, values):
    m = functools.reduce(jnp.maximum, [jnp.max(s, axis=-1, keepdims=True) for s in scores])
    ps = [jnp.exp(s - m) for s in scores]
    denom = functools.reduce(jnp.add, [jnp.sum(p, axis=-1, keepdims=True) for p in ps])
    acc = functools.reduce(jnp.add, [jnp.dot(p.astype(BF16), v, preferred_element_type=F32)
                                     for p, v in zip(ps, values)])
    return acc / denom


def _qk(q, k):
    return lax.dot_general(q, k, (((1,), (1,)), ((), ())), preferred_element_type=F32) * (HEAD_DIM ** -0.5)


def _attn_row_structure(blk, grid_rows):
    wr = min(WIN_ROWS, grid_rows)
    k_row0 = int(np.clip(blk * Q_ROWS - WIN_ROWS // 2, 0, grid_rows - KEY_ROWS))
    qr = blk * Q_ROWS + np.arange(Q_ROWS)
    kr = k_row0 + np.arange(KEY_ROWS)
    r0 = np.clip(qr - wr // 2, 0, grid_rows - wr)
    valid = (kr[None, :] >= r0[:, None]) & (kr[None, :] < r0[:, None] + wr)
    ri = np.clip(kr[None, :] - qr[:, None] + (WIN_ROWS - 1), 0, 2 * WIN_ROWS - 2)
    return valid, ri


def _attn_body(qr_ref, qp_ref, k_ref, v_ref, kc_ref, vc_ref, bcol_ref, o_ref, bias_ref, *, n_blocks, grid_rows):
    i = pl.program_id(2)
    n_keys = KEY_ROWS * GRID_W

    heads = [slice(hh * HEAD_DIM, (hh + 1) * HEAD_DIM) for hh in range(HEADS_PER_STEP)]

    def build_bias(blk):
        valid, ri = _attn_row_structure(blk, grid_rows)
        for hh in range(HEADS_PER_STEP):
            for jr in range(Q_ROWS):
                for kl in range(KEY_ROWS):
                    piece = (bcol_ref[hh, int(ri[jr, kl])] if valid[jr, kl]
                             else jnp.full((GRID_W, GRID_W), MASKED, F32))
                    bias_ref[hh, jr * GRID_W:(jr + 1) * GRID_W, kl * GRID_W:(kl + 1) * GRID_W] = piece

    for blk in sorted({0, min(1, n_blocks - 1), n_blocks - 1}):
        pl.when(i == blk)(functools.partial(build_bias, blk))

    @pl.when(i < n_blocks)
    def _():
        k_row0 = jnp.clip(i * Q_ROWS - WIN_ROWS // 2, 0, grid_rows - KEY_ROWS)
        start = pl.multiple_of(k_row0 * GRID_W, GRID_W)
        for hh, cols in enumerate(heads):
            k_loc = k_ref[pl.ds(start, n_keys), cols]
            v_loc = v_ref[pl.ds(start, n_keys), cols]
            s_loc = _qk(qr_ref[:, cols], k_loc) + bias_ref[hh]
            s_ctx = _qk(qp_ref[:, cols], kc_ref[:, cols])
            o_ref[:, cols] = _softmax_pv([s_loc, s_ctx], [v_loc, vc_ref[:, cols]]).astype(BF16)

    @pl.when(i >= n_blocks)
    def _():
        for cols in heads:
            s_ctx = _qk(qp_ref[:, cols], kc_ref[:, cols])
            o_ref[:, cols] = _softmax_pv([s_ctx], [vc_ref[:, cols]]).astype(BF16)


def _attn_bias_by_column(rpb, grid_rows):
    n_blocks = grid_rows // Q_ROWS
    assert min(WIN_ROWS, grid_rows) == WIN_ROWS and grid_rows >= KEY_ROWS and grid_rows % Q_ROWS == 0
    interior = _attn_row_structure(min(1, n_blocks - 1), grid_rows)
    for blk in range(1, n_blocks - 1):
        assert all(np.array_equal(a, b) for a, b in zip(_attn_row_structure(blk, grid_rows), interior))
    n_ci = 2 * WIN_COLS - 1
    qc = np.arange(GRID_W)
    c0 = np.clip(qc - WIN_COLS // 2, 0, GRID_W - WIN_COLS)
    col_valid = (qc[None, :] >= c0[:, None]) & (qc[None, :] < c0[:, None] + WIN_COLS)
    col_sel = (col_valid[:, :, None]
               & ((qc[None, :, None] - qc[:, None, None] + (WIN_COLS - 1)) == np.arange(n_ci))).astype(np.float32)
    by_col = jnp.einsum('hrc,qkc->hrqk', rpb.astype(F32), jnp.asarray(col_sel), precision=lax.Precision.HIGHEST)
    return jnp.where(jnp.asarray(col_valid)[None, None], by_col, MASKED)


def _attn_call(qr, qp, kr, v, bias_by_col, *, batch, seq, ctx_len, with_ctx_queries):
    assert ctx_len == Q_BLOCK and seq % Q_BLOCK == 0
    grid_rows = seq // GRID_W
    n_blocks = grid_rows // Q_ROWS
    n_steps = n_blocks + (1 if with_ctx_queries else 0)
    ctx_block0 = batch * n_blocks
    n_out = batch * seq + (batch * ctx_len if with_ctx_queries else 0)
    n_keys = KEY_ROWS * GRID_W

    def q_map(b, h, i):
        return (jnp.where(i < n_blocks, b * n_blocks + i, ctx_block0 + b), h)

    width = HEADS_PER_STEP * HEAD_DIM
    return pl.pallas_call(
        functools.partial(_attn_body, n_blocks=n_blocks, grid_rows=grid_rows),
        grid=(batch, NA_HEADS // HEADS_PER_STEP, n_steps),
        in_specs=[pl.BlockSpec((Q_BLOCK, width), q_map),
                  pl.BlockSpec((Q_BLOCK, width), q_map),
                  pl.BlockSpec((seq, width), lambda b, h, i: (b, h)),
                  pl.BlockSpec((seq, width), lambda b, h, i: (b, h)),
                  pl.BlockSpec((ctx_len, width), lambda b, h, i: (ctx_block0 + b, h)),
                  pl.BlockSpec((ctx_len, width), lambda b, h, i: (ctx_block0 + b, h)),
                  pl.BlockSpec((HEADS_PER_STEP,) + bias_by_col.shape[1:], lambda b, h, i: (h, 0, 0, 0))],
        out_specs=pl.BlockSpec((Q_BLOCK, width), q_map),
        out_shape=jax.ShapeDtypeStruct((n_out, NA_HEADS * HEAD_DIM), BF16),
        scratch_shapes=[pltpu.VMEM((HEADS_PER_STEP, Q_BLOCK, n_keys), F32)],
        compiler_params=_params(3),
    )(qr, qp, kr, v, kr, v, bias_by_col)


def _merge_body(h_ref, a_ref, o_ref, wg_ref, bg_ref, wa_ref, wb_ref, m_ref, *, tn):
    d = m_ref.shape[1]
    hb, ab, ob = h_ref[...], a_ref[...], o_ref[...]
    for n0 in range(0, d, tn):
        cols = slice(n0, n0 + tn)
        gcols = slice(d + n0, d + n0 + tn)
        g_a = jax.nn.sigmoid(jnp.dot(hb, wg_ref[:, cols], preferred_element_type=F32) + bg_ref[:, cols])
        g_b = jax.nn.sigmoid(jnp.dot(hb, wg_ref[:, gcols], preferred_element_type=F32) + bg_ref[:, gcols])
        pa = jnp.dot(ab, wa_ref[:, cols], preferred_element_type=F32)
        pb = jnp.dot(ob, wb_ref[:, cols], preferred_element_type=F32)
        m_ref[:, cols] = (g_a * pa + g_b * pb).astype(BF16)


def _merge_call(h, a, o, w_gate_b, b_gate, w_a_b, w_b_b, *, layer, n_rows, tm):
    d = h.shape[1]

    def row_map(i):
        return (i, 0)

    return pl.pallas_call(
        functools.partial(_merge_body, tn=min(d, 512)),
        grid=(n_rows // tm,),
        in_specs=[pl.BlockSpec((tm, d), row_map),
                  pl.BlockSpec((tm, a.shape[1]), row_map),
                  pl.BlockSpec((tm, o.shape[1]), row_map),
                  _resident_layer(w_gate_b, layer), _resident((1, 2 * d)),
                  _resident_layer(w_a_b, layer), _resident_layer(w_b_b, layer)],
        out_specs=pl.BlockSpec((tm, d), row_map),
        out_shape=jax.ShapeDtypeStruct((n_rows, d), BF16),
        compiler_params=_params(1),
    )(h, a, o, w_gate_b, b_gate.reshape(1, 2 * d), w_a_b, w_b_b)


def _outproj_body(*refs, n_main_tiles, two_src, n_experts):
    if two_src:
        xa_ref, xb_ref = refs[:2]
        refs = refs[2:]
    else:
        xa_ref = refs[0]
        refs = refs[1:]
    (m_ref, wo_ref, mod_ref, nw_ref, wr_ref, br_ref,
     xo_ref, hp_ref, route_ref, route_t_ref, cnt_ref, carry_ref) = refs
    i = pl.program_id(0)
    if two_src:
        x = jnp.where(i < n_main_tiles, xa_ref[...], xb_ref[...])
    else:
        x = xa_ref[...]
    tm, d = x.shape

    @pl.when(i == 0)
    def _():
        carry_ref[...] = jnp.zeros_like(carry_ref)

    y = jnp.dot(m_ref[...], wo_ref[...], preferred_element_type=F32)
    x_new = x + mod_ref[0, 2:3, :] * y
    xo_ref[...] = x_new
    h = _rms_modulate(x_new, nw_ref[...], mod_ref[0, 3:4, :], mod_ref[0, 4:5, :])
    hp_ref[...] = _to_token_tiles(_pack_bf16_pair(h[:, :d // 2], h[:, d // 2:]))

    h_hi = h.astype(BF16)
    h_lo = (h - h_hi.astype(F32)).astype(BF16)
    by_hi = jnp.dot(h_hi, wr_ref[...], preferred_element_type=F32)
    logits = (by_hi[:, :LANES] + by_hi[:, LANES:]
              + jnp.dot(h_lo, wr_ref[:, :LANES], preferred_element_type=F32) + br_ref[...])
    lane = lax.broadcasted_iota(jnp.int32, logits.shape, 1)
    lane_f = lane.astype(F32)
    far = jnp.float32(4 * LANES)

    def first_argmax(vals):
        top = jnp.max(vals, axis=-1, keepdims=True)
        return top, jnp.min(jnp.where(vals == top, lane_f, far), axis=-1, keepdims=True)

    g_logits = jnp.where((lane >= n_experts) & (lane < n_experts + N_GROUPS), logits, MASKED)
    g_top, g_lane = first_argmax(g_logits)
    g_prob = 1.0 / jnp.sum(jnp.exp(g_logits - g_top), axis=-1, keepdims=True)
    e_lo = (g_lane - n_experts) * EXPERTS_PER_GROUP
    e_logits = jnp.where((lane_f >= e_lo) & (lane_f < e_lo + EXPERTS_PER_GROUP), logits, MASKED)
    top1, e1 = first_argmax(e_logits)
    top2, e2 = first_argmax(jnp.where(lane_f == e1, MASKED, e_logits))
    t = jnp.exp(top2 - top1)
    w1 = g_prob / (1.0 + t)
    w2 = g_prob * t / (1.0 + t)

    sel1, sel2 = lane_f == e1, lane_f == e2
    onehot = jnp.where(sel1 | sel2, 1.0, 0.0)
    r_i = lax.broadcasted_iota(jnp.int32, (tm, tm), 0)
    c_i = lax.broadcasted_iota(jnp.int32, (tm, tm), 1)
    earlier = jnp.where(c_i < r_i, 1.0, 0.0).astype(BF16)
    before = jnp.dot(earlier, onehot.astype(BF16), preferred_element_type=F32) + carry_ref[...]
    rank1 = jnp.sum(jnp.where(sel1, before, 0.0), axis=-1, keepdims=True)
    rank2 = jnp.sum(jnp.where(sel2, before, 0.0), axis=-1, keepdims=True)
    carry_ref[...] += jnp.sum(onehot, axis=0, keepdims=True)
    cnt_ref[...] = carry_ref[...]

    route = jnp.zeros_like(logits)
    for k, val in enumerate((e1, e2, w1, w2, rank1, rank2)):
        route = jnp.where(lane == k, val, route)
    route_ref[...] = route
    route_t_ref[...] = route.T[:8, :]


def _outproj_call(x_main, x_ctx, m, w_out_b, mod, norm_w, w_router, b_router, *, layer, rows_per_mod, tm,
                  n_experts):
    d = x_main.shape[1]
    two_src = x_ctx is not None
    n_main = x_main.shape[0] if two_src else m.shape[0]
    n_rows = m.shape[0]
    n_main_tiles = n_main // tm
    n_mod = mod.shape[0]
    tiles_per_mod = rows_per_mod // tm

    def row_map(i):
        return (i, 0)

    x_specs = [pl.BlockSpec((tm, d), lambda i: (jnp.minimum(i, n_main_tiles - 1), 0))]
    x_args = [x_main]
    if two_src:
        x_specs.append(pl.BlockSpec((tm, d), lambda i: (jnp.maximum(i - n_main_tiles, 0), 0)))
        x_args.append(x_ctx)
    in_specs = x_specs + [
        pl.BlockSpec((tm, d), row_map),
        _resident_layer(w_out_b, layer),
        pl.BlockSpec((1, 6, d), lambda i: (jnp.minimum(i // tiles_per_mod, n_mod - 1), 0, 0)),
        _resident((1, d)),
        _resident(w_router.shape),
        _resident((1, LANES)),
    ]
    return pl.pallas_call(
        functools.partial(_outproj_body, n_main_tiles=n_main_tiles, two_src=two_src, n_experts=n_experts),
        grid=(n_rows // tm,),
        in_specs=in_specs,
        out_specs=[pl.BlockSpec((tm, d), row_map),
                   pl.BlockSpec((tm, d // 2 // LANES, LANES), lambda i: (i, 0, 0)),
                   pl.BlockSpec((tm, LANES), row_map),
                   pl.BlockSpec((8, tm), lambda i: (0, i)),
                   pl.BlockSpec((1, LANES), lambda i: (0, 0))],
        out_shape=[jax.ShapeDtypeStruct((n_rows, d), F32),
                   jax.ShapeDtypeStruct((n_rows, d // 2 // LANES, LANES), U32),
                   jax.ShapeDtypeStruct((n_rows, LANES), F32),
                   jax.ShapeDtypeStruct((8, n_rows), F32),
                   jax.ShapeDtypeStruct((1, LANES), F32)],
        scratch_shapes=[pltpu.VMEM((1, LANES), F32)],
        compiler_params=_params(1),
    )(*x_args, m, w_out_b, mod, norm_w.reshape(1, d), w_router, b_router)


def _row_copy(src_ref, src_row, dst_ref, dst_row, sem):
    return pltpu.make_async_copy(src_ref.at[src_row], dst_ref.at[dst_row], sem)


def _to_token_tiles(rows):
    return rows.reshape(rows.shape[0], rows.shape[1] // LANES, LANES)


def _from_token_tiles(tiles):
    return tiles.reshape(tiles.shape[0], tiles.shape[1] * LANES)


def _slot_rows_body(p1_ref, p2_ref, init_ref, dst_ref, stage_ref, sem, *, tokens_per_step, row_stride):
    i = pl.program_id(0)

    @pl.when(i == 0)
    def _():
        load = pltpu.make_async_copy(init_ref, stage_ref, sem)
        load.start()
        load.wait()

    def put(r, carry):
        t = i * tokens_per_step + r
        stage_ref[p1_ref[t]] = t
        stage_ref[p2_ref[t]] = row_stride + t
        return carry

    lax.fori_loop(0, tokens_per_step, put, 0, unroll=8)

    @pl.when(i == pl.num_programs(0) - 1)
    def _():
        store = pltpu.make_async_copy(stage_ref, dst_ref, sem)
        store.start()
        store.wait()


def _slot_rows_call(pos1, pos2, init_rows, *, row_stride, tokens_per_step):
    n_tok = pos1.shape[0]
    n_map = init_rows.shape[0]
    return pl.pallas_call(
        functools.partial(_slot_rows_body, tokens_per_step=tokens_per_step, row_stride=row_stride),
        grid_spec=pltpu.PrefetchScalarGridSpec(
            num_scalar_prefetch=2,
            grid=(n_tok // tokens_per_step,),
            in_specs=[pl.BlockSpec(memory_space=pl.ANY)],
            out_specs=pl.BlockSpec(memory_space=pl.ANY),
            scratch_shapes=[pltpu.SMEM((n_map,), jnp.int32), pltpu.SemaphoreType.DMA(())],
        ),
        out_shape=jax.ShapeDtypeStruct((n_map,), jnp.int32),
        compiler_params=_params(1),
    )(pos1, pos2, init_rows)


def _ffn_body(first_ref, count_ref, src_ref, dst_ref, h_ref, wg_ref, wu_ref, wd_ref, yk_ref,
              wgb_ref, wub_ref, wdb_ref, xbuf, ybuf, sem_g, sem_s, *, tm, spare_rows):
    e = pl.program_id(0)
    n_experts = pl.num_programs(0)
    half = xbuf.shape[2] * LANES

    def gather_row(tile, r, buf):
        return _row_copy(h_ref, src_ref[(tile + 1) * tm + r], xbuf.at[buf], r, sem_g.at[buf])

    def scatter_row(tile, r, buf):
        return _row_copy(ybuf.at[buf], r, yk_ref, dst_ref[(tile + 1) * tm + r], sem_s.at[buf])

    def wait_tile(buf_ref, sem):
        pltpu.make_async_copy(buf_ref, buf_ref, sem).wait()

    @pl.when(e == 0)
    def _():
        ybuf[...] = jnp.zeros_like(ybuf)

        def prime(r, carry):
            _row_copy(ybuf.at[0], r, yk_ref, spare_rows + r, sem_s.at[0]).start()
            gather_row(0, r, 0).start()
            return carry

        lax.fori_loop(0, tm, prime, 0)

    wgb_ref[...] = wg_ref[0, 0].astype(BF16)
    wub_ref[...] = wu_ref[0, 0].astype(BF16)
    wdb_ref[...] = wd_ref[0, 0].astype(BF16)

    def run_tile(j, carry):
        slot = j % 2
        other = 1 - slot
        wait_tile(xbuf.at[slot], sem_g.at[slot])
        lo, hi = _unpack_bf16_pair(_from_token_tiles(xbuf[slot]))
        lo, hi = lo.astype(BF16), hi.astype(BF16)
        wait_tile(ybuf.at[slot], sem_s.at[slot])
        for r in range(tm):
            gather_row(j + 1, r, other).start()
            scatter_row(j - 1, r, other).start()

        def up(w_ref):
            return (jnp.dot(lo, w_ref[:half, :], preferred_element_type=F32)
                    + jnp.dot(hi, w_ref[half:, :], preferred_element_type=F32))

        g = up(wgb_ref)
        hid = (g * jax.nn.sigmoid(g) * up(wub_ref)).astype(BF16)
        y = jnp.dot(hid, wdb_ref[...], preferred_element_type=F32)
        ybuf[slot] = _to_token_tiles(_pack_bf16_pair(y[:, :half], y[:, half:]))
        return carry

    first = first_ref[e]
    lax.fori_loop(first, first + count_ref[e], run_tile, 0)

    @pl.when(e == n_experts - 1)
    def _():
        j = first + count_ref[e] - 1
        slot = j % 2
        other = 1 - slot

        def flush(r, carry):
            scatter_row(j, r, slot).start()
            return carry

        wait_tile(ybuf.at[other], sem_s.at[other])
        lax.fori_loop(0, tm, flush, 0)
        wait_tile(ybuf.at[slot], sem_s.at[slot])
        wait_tile(xbuf.at[other], sem_g.at[other])


def _ffn_call(first_tile, tile_count, dst_rows, h_packed, w_gate, w_up, w_down, *, layer, tm, row_stride,
              n_out_rows):
    _, sub, _ = h_packed.shape
    _, n_experts, d, de = w_gate.shape
    src_rows = jnp.where(dst_rows < 2 * row_stride, dst_rows % row_stride, 0)

    def w_map(e, ft, tc, sr, dr):
        return (layer, e, 0, 0)

    return pl.pallas_call(
        functools.partial(_ffn_body, tm=tm, spare_rows=2 * row_stride),
        grid_spec=pltpu.PrefetchScalarGridSpec(
            num_scalar_prefetch=4,
            grid=(n_experts,),
            in_specs=[pl.BlockSpec(memory_space=pl.ANY),
                      pl.BlockSpec((1, 1, d, de), w_map),
                      pl.BlockSpec((1, 1, d, de), w_map),
                      pl.BlockSpec((1, 1, de, d), w_map)],
            out_specs=pl.BlockSpec(memory_space=pl.ANY),
            scratch_shapes=[pltpu.VMEM((d, de), BF16), pltpu.VMEM((d, de), BF16), pltpu.VMEM((de, d), BF16),
                            pltpu.VMEM((2, tm, sub, LANES), U32), pltpu.VMEM((2, tm, sub, LANES), U32),
                            pltpu.SemaphoreType.DMA((2,)), pltpu.SemaphoreType.DMA((2,))],
        ),
        out_shape=jax.ShapeDtypeStruct((n_out_rows, sub, LANES), U32),
        compiler_params=_params(1),
    )(first_tile, tile_count, src_rows, dst_rows, h_packed, w_gate, w_up, w_down)


def _combine_body(x_ref, y1_ref, y2_ref, route_ref, mod_ref, fw_ref, o_ref, *, final_norm):
    w1, w2 = route_ref[:, 2:3], route_ref[:, 3:4]
    lo1, hi1 = _unpack_bf16_pair(_from_token_tiles(y1_ref[...]))
    lo2, hi2 = _unpack_bf16_pair(_from_token_tiles(y2_ref[...]))
    y = jnp.concatenate([w1 * lo1 + w2 * lo2, w1 * hi1 + w2 * hi2], axis=-1)
    x_new = x_ref[...] + mod_ref[0, 5:6, :] * y
    if final_norm:
        x_new = x_new * lax.rsqrt(jnp.mean(x_new * x_new, axis=-1, keepdims=True) + RMS_EPS) * fw_ref[...]
    o_ref[...] = x_new


def _combine_call(x, yk, route, mod, final_w, *, rows_per_mod, tm, row_stride, final_norm):
    n_rows, d = x.shape
    sub = yk.shape[1]
    n_mod = mod.shape[0]
    tiles_per_mod = rows_per_mod // tm
    second = row_stride // tm

    def row_map(i):
        return (i, 0)

    return pl.pallas_call(
        functools.partial(_combine_body, final_norm=final_norm),
        grid=(n_rows // tm,),
        in_specs=[pl.BlockSpec((tm, d), row_map),
                  pl.BlockSpec((tm, sub, LANES), lambda i: (i, 0, 0)),
                  pl.BlockSpec((tm, sub, LANES), lambda i: (second + i, 0, 0)),
                  pl.BlockSpec((tm, LANES), row_map),
                  pl.BlockSpec((1, 6, d), lambda i: (jnp.minimum(i // tiles_per_mod, n_mod - 1), 0, 0)),
                  pl.BlockSpec((1, d), lambda i: (0, 0))],
        out_specs=pl.BlockSpec((tm, d), row_map),
        out_shape=jax.ShapeDtypeStruct((n_rows, d), F32),
        compiler_params=_params(1),
    )(x, yk, yk, route, mod, final_w.reshape(1, d))


def _slot_plan(route_t, counts, n_experts, tm):
    n_rows = route_t.shape[1]
    n_tiles = (2 * n_rows) // tm + n_experts
    cnt = counts[0, :n_experts].astype(jnp.int32)
    padded = ((cnt + tm - 1) // tm) * tm
    ends = jnp.cumsum(padded)
    starts = ends - padded
    fields = route_t.astype(jnp.int32)
    pos1 = starts[fields[0]] + fields[4]
    pos2 = starts[fields[1]] + fields[5]
    first_tile = (starts // tm).astype(jnp.int32)
    tile_count = (padded // tm).astype(jnp.int32)
    row_stride = n_rows
    spare = 2 * row_stride
    n_map = pl.cdiv((n_tiles + 2) * tm, 1024) * 1024
    init_rows = spare + jnp.arange(n_map, dtype=jnp.int32) % tm
    dst_rows = _slot_rows_call(pos1 + tm, pos2 + tm, init_rows, row_stride=row_stride, tokens_per_step=2 * tm)
    n_out_rows = spare + tm
    return first_tile, tile_count, dst_rows, row_stride, n_out_rows


def _rope_tables(batch, seq, ctx_rows):
    pairs = HEAD_DIM // 4
    t = np.arange(seq)
    pos = np.stack([t // GRID_W, t % GRID_W], axis=-1).astype(np.float32)
    inv_freq = (ROPE_THETA ** (-np.arange(pairs, dtype=np.float32) / pairs)).astype(np.float32)
    ang = pos[:, :, None] * inv_freq
    cos = np.repeat(np.cos(ang)[:, :, None, :], 2, axis=2).reshape(seq, HEAD_DIM)
    sin = np.sin(ang)
    zero = np.zeros_like(sin)
    sa = np.stack([-sin, zero], axis=2).reshape(seq, HEAD_DIM)
    sb = np.stack([zero, sin], axis=2).reshape(seq, HEAD_DIM)

    def full(tab, fill):
        return jnp.asarray(np.concatenate([np.tile(tab, (batch, 1)),
                                           np.full((ctx_rows, HEAD_DIM), fill, np.float32)]), F32)

    return full(cos, 1.0), full(sa, 0.0), full(sb, 0.0)


def kernel(x, c, ctx, c_ctx, w_ada, b_ada, norm_mix_w, norm_ffn_w, w_in, sgu_norm_w, sgu_w_s, sgu_b_s, na_rpb,
           w_merge_gate, b_merge_gate, w_branch_a, w_branch_b, w_out, w_router_group, b_router_group,
           w_router_expert, b_router_expert, w_exp_gate, w_exp_up, w_exp_down, final_norm_w):
    batch, seq, d = x.shape
    ctx_len = ctx.shape[1]
    depth = w_ada.shape[0]
    n_experts = w_exp_gate.shape[1]
    n_lat = batch * seq
    n_ctx = batch * ctx_len
    tm = 256
    tm_wide = 512
    assert seq % tm_wide == 0 and n_ctx % tm_wide == 0 and tm % SGU_CHUNK == 0 and ctx_len % SGU_CHUNK == 0

    mods = _ada_call(jnp.concatenate([c, c_ctx[None]], axis=0), w_ada, b_ada)
    cos, sa, sb = _rope_tables(batch, seq, n_ctx)
    grid_rows = seq // GRID_W

    x_cur = x.reshape(n_lat, d)
    ctx_rows = ctx.reshape(n_ctx, d)
    w_in_b, w_gate_b, w_a_b, w_b_b, w_out_b = (w.astype(BF16) for w in
                                               (w_in, w_merge_gate, w_branch_a, w_branch_b, w_out))
    x_all = None
    for l in range(depth):
        last = l == depth - 1
        mod = mods[l, :batch + 1].reshape(batch + 1, 6, d)
        w_s_b = sgu_w_s[l].astype(BF16)
        b_s_t = sgu_b_s[l].T
        if x_all is None:
            src = (x_cur, ctx_rows)
        else:
            src = (x_all, None)
        h, a, qp, qr, kr, v = _inproj_call(src[0], src[1], mod, norm_mix_w[l], w_in_b, sgu_norm_w[l], w_s_b, b_s_t,
                                           cos, sa, sb, layer=l, rows_per_mod=seq, tm=tm)
        o = _attn_call(qr, qp, kr, v, _attn_bias_by_column(na_rpb[l], grid_rows), batch=batch, seq=seq,
                       ctx_len=ctx_len, with_ctx_queries=not last)
        n_rows = n_lat if last else n_lat + n_ctx
        m = _merge_call(h, a, o, w_gate_b, b_merge_gate[l], w_a_b, w_b_b, layer=l, n_rows=n_rows, tm=tm_wide)
        w_router = jnp.zeros((d, LANES), F32)
        w_router = w_router.at[:, :n_experts].set(w_router_expert[l])
        w_router = w_router.at[:, n_experts:n_experts + N_GROUPS].set(w_router_group[l])
        w_router_hi = w_router.astype(BF16)
        w_router_lo = (w_router - w_router_hi.astype(F32)).astype(BF16)
        w_router = jnp.concatenate([w_router_hi, w_router_lo], axis=1)
        b_router = jnp.zeros((1, LANES), F32)
        b_router = b_router.at[0, :n_experts].set(b_router_expert[l])
        b_router = b_router.at[0, n_experts:n_experts + N_GROUPS].set(b_router_group[l])
        x_mid, h_packed, route, route_t, counts = _outproj_call(
            src[0], src[1], m, w_out_b, mod, norm_ffn_w[l], w_router, b_router, layer=l, rows_per_mod=seq,
            tm=tm_wide, n_experts=n_experts)
        first_tile, tile_count, dst_rows, row_stride, n_out_rows = _slot_plan(route_t, counts, n_experts, tm)
        yk = _ffn_call(first_tile, tile_count, dst_rows, h_packed, w_exp_gate, w_exp_up, w_exp_down, layer=l, tm=tm,
                       row_stride=row_stride, n_out_rows=n_out_rows)
        x_all = _combine_call(x_mid, yk, route, mod, final_norm_w, rows_per_mod=seq, tm=tm,
                              row_stride=row_stride, final_norm=last)
    return x_all.reshape(batch, seq, d)
```

```python
import functools

import numpy as np
import jax
import jax.numpy as jnp
from jax import lax
from jax.experimental import pallas as pl
from jax.experimental.pallas import tpu as pltpu

GRID_W = 64
SGU_CHUNK = 128
SGU_GROUPS = 8
NA_HEADS = 8
HEAD_DIM = 128
WIN_ROWS = 8
WIN_COLS = 16
ROPE_THETA = 10000.0
N_GROUPS = 4
EXPERTS_PER_GROUP = 8
RMS_EPS = 1e-6

LANES = 128
Q_ROWS = 4
Q_BLOCK = Q_ROWS * GRID_W
KEY_ROWS = Q_ROWS + WIN_ROWS - 1
HEADS_PER_STEP = 4
MASKED = -1e30
LOG2_E = 1.4426950408889634
QK_SCALE = HEAD_DIM ** -0.5 * LOG2_E
VMEM_LIMIT = 56 * 1024 * 1024

BF16 = jnp.bfloat16
F32 = jnp.float32
U32 = jnp.uint32


def _params(n_grid_dims, vmem=VMEM_LIMIT):
    return pltpu.CompilerParams(dimension_semantics=("arbitrary",) * n_grid_dims, vmem_limit_bytes=vmem)


def _resident(shape):
    nd = len(shape)
    return pl.BlockSpec(shape, lambda *_: (0,) * nd, pipeline_mode=pl.Buffered(1))


def _resident_layer(stacked, layer):
    nd = stacked.ndim
    return pl.BlockSpec((None,) + stacked.shape[1:], lambda *_: (layer,) + (0,) * (nd - 1),
                        pipeline_mode=pl.Buffered(1))


def _pack_bf16_pair(lo, hi):
    lo_bits = lax.bitcast_convert_type(lo.astype(BF16).astype(F32), U32)
    hi_bits = lax.bitcast_convert_type(hi.astype(BF16).astype(F32), U32)
    return (hi_bits & jnp.uint32(0xFFFF0000)) | (lo_bits >> 16)


def _unpack_bf16_pair(w):
    lo = lax.bitcast_convert_type(w << 16, F32)
    hi = lax.bitcast_convert_type(w & jnp.uint32(0xFFFF0000), F32)
    return lo, hi


def _rms_modulate(x, norm_w, shift, scale):
    y = x * lax.rsqrt(jnp.mean(x * x, axis=-1, keepdims=True) + RMS_EPS) * norm_w
    return y * (1.0 + scale) + shift


def _ada_body(ct_ref, w_ref, b_ref, o_ref, *, n_rows):
    s = ct_ref[...]
    s = s * jax.nn.sigmoid(s)
    w = w_ref[0]
    o_ref[...] = jnp.zeros_like(o_ref)
    for r in range(n_rows):
        o_ref[0, r:r + 1, :] = jnp.sum(w * s[:, r:r + 1], axis=0, keepdims=True) + b_ref[0]


def _ada_call(cond, w_ada, b_ada):
    n_rows, d = cond.shape
    depth, _, n = w_ada.shape
    tn = min(n, 1024)
    ct = jnp.zeros((d, 8), F32).at[:, :n_rows].set(cond.T)
    return pl.pallas_call(
        functools.partial(_ada_body, n_rows=n_rows),
        grid=(depth, n // tn),
        in_specs=[pl.BlockSpec((d, 8), lambda l, j: (0, 0)),
                  pl.BlockSpec((1, d, tn), lambda l, j: (l, 0, j)),
                  pl.BlockSpec((1, 1, tn), lambda l, j: (l, 0, j))],
        out_specs=pl.BlockSpec((1, 8, tn), lambda l, j: (l, 0, j)),
        out_shape=jax.ShapeDtypeStruct((depth, 8, n), F32),
        compiler_params=_params(2),
    )(ct, w_ada, b_ada.reshape(depth, 1, n))


def _inproj_body(*refs, n_main_tiles, two_src, sgu_w, na_w):
    if two_src:
        xa_ref, xb_ref = refs[:2]
        refs = refs[2:]
    else:
        xa_ref = refs[0]
        refs = refs[1:]
    (mod_ref, nw_ref, w_ref, snw_ref, ws_ref, bst_ref, cos_ref, sa_ref, sb_ref,
     h_ref, a_ref, qp_ref, qr_ref, kr_ref, v_ref) = refs
    if two_src:
        x = jnp.where(pl.program_id(0) < n_main_tiles, xa_ref[...], xb_ref[...])
    else:
        x = xa_ref[...]
    tm = x.shape[0]
    h = _rms_modulate(x, nw_ref[...], mod_ref[0, 0:1, :], mod_ref[0, 1:2, :])
    hb = h.astype(BF16)
    h_ref[...] = hb

    def proj(lo, width):
        return jnp.dot(hb, w_ref[:, lo:lo + width], preferred_element_type=F32)

    u = jax.nn.gelu(proj(0, sgu_w))
    v = jax.nn.gelu(proj(sgu_w, sgu_w))
    vn = v * lax.rsqrt(jnp.mean(v * v, axis=-1, keepdims=True) + RMS_EPS) * snw_ref[...]
    vnb = vn.astype(BF16)
    gch = sgu_w // SGU_GROUPS
    for c in range(tm // SGU_CHUNK):
        rows = slice(c * SGU_CHUNK, (c + 1) * SGU_CHUNK)
        for g in range(SGU_GROUPS):
            cols = slice(g * gch, (g + 1) * gch)
            z = jnp.dot(ws_ref[g], vnb[rows, cols], preferred_element_type=F32) + bst_ref[:, g:g + 1]
            a_ref[rows, cols] = (u[rows, cols] * z).astype(BF16)

    cos, sa, sb = cos_ref[...], sa_ref[...], sb_ref[...]

    def rope_into(p, out_ref):
        for hh in range(NA_HEADS):
            cols = slice(hh * HEAD_DIM, (hh + 1) * HEAD_DIM)
            xh = p[:, cols]
            out_ref[:, cols] = (xh * cos + pltpu.roll(xh, HEAD_DIM - 32, 1) * sa
                                + pltpu.roll(xh, 32, 1) * sb).astype(BF16)

    q = proj(2 * sgu_w, na_w) * QK_SCALE
    qp_ref[...] = q.astype(BF16)
    rope_into(q, qr_ref)
    rope_into(proj(2 * sgu_w + na_w, na_w), kr_ref)
    v_ref[...] = proj(2 * sgu_w + 2 * na_w, na_w).astype(BF16)


def _inproj_call(x_main, x_ctx, mod, norm_w, w_in_b, sgu_norm_w, w_s_b, b_s_t, cos, sa, sb,
                 *, layer, rows_per_mod, tm):
    d = x_main.shape[1]
    n_main = x_main.shape[0]
    two_src = x_ctx is not None
    tt = n_main + (x_ctx.shape[0] if two_src else 0)
    n_main_tiles = n_main // tm
    n_mod = mod.shape[0]
    sgu_w = sgu_norm_w.shape[-1]
    na_w = NA_HEADS * HEAD_DIM
    tiles_per_mod = rows_per_mod // tm

    def row_map(i):
        return (i, 0)

    x_specs = [pl.BlockSpec((tm, d), lambda i: (jnp.minimum(i, n_main_tiles - 1), 0))]
    x_args = [x_main]
    if two_src:
        x_specs.append(pl.BlockSpec((tm, d), lambda i: (jnp.maximum(i - n_main_tiles, 0), 0),
                                    pipeline_mode=pl.Buffered(1)))
        x_args.append(x_ctx)
    in_specs = x_specs + [
        pl.BlockSpec((1, 6, d), lambda i: (jnp.minimum(i // tiles_per_mod, n_mod - 1), 0, 0)),
        _resident((1, d)),
        _resident_layer(w_in_b, layer),
        _resident((1, sgu_w)),
        _resident(w_s_b.shape),
        _resident(b_s_t.shape),
        pl.BlockSpec((tm, HEAD_DIM), row_map),
        pl.BlockSpec((tm, HEAD_DIM), row_map),
        pl.BlockSpec((tm, HEAD_DIM), row_map),
    ]
    out_widths = [d, sgu_w, na_w, na_w, na_w, na_w]
    return pl.pallas_call(
        functools.partial(_inproj_body, n_main_tiles=n_main_tiles, two_src=two_src, sgu_w=sgu_w, na_w=na_w),
        grid=(tt // tm,),
        in_specs=in_specs,
        out_specs=[pl.BlockSpec((tm, w), row_map) for w in out_widths],
        out_shape=[jax.ShapeDtypeStruct((tt, w), BF16) for w in out_widths],
        compiler_params=_params(1),
    )(*x_args, mod, norm_w.reshape(1, d), w_in_b, sgu_norm_w.reshape(1, sgu_w), w_s_b, b_s_t, cos, sa, sb)


def _softmax_pv(scores, values):
    m = functools.reduce(jnp.maximum, [jnp.max(s, axis=-1, keepdims=True) for s in scores])
    ps = [jnp.exp2(s - m) for s in scores]
    denom = functools.reduce(jnp.add, [jnp.sum(p, axis=-1, keepdims=True) for p in ps])
    acc = functools.reduce(jnp.add, [jnp.dot(p.astype(BF16), v, preferred_element_type=F32)
                                     for p, v in zip(ps, values)])
    return acc / denom


def _qk(q, k):
    return lax.dot_general(q, k, (((1,), (1,)), ((), ())), preferred_element_type=F32)


def _attn_row_structure(blk, grid_rows):
    wr = min(WIN_ROWS, grid_rows)
    k_row0 = int(np.clip(blk * Q_ROWS - WIN_ROWS // 2, 0, grid_rows - KEY_ROWS))
    qr = blk * Q_ROWS + np.arange(Q_ROWS)
    kr = k_row0 + np.arange(KEY_ROWS)
    r0 = np.clip(qr - wr // 2, 0, grid_rows - wr)
    valid = (kr[None, :] >= r0[:, None]) & (kr[None, :] < r0[:, None] + wr)
    ri = np.clip(kr[None, :] - qr[:, None] + (WIN_ROWS - 1), 0, 2 * WIN_ROWS - 2)
    return valid, ri


def _attn_body(qr_ref, qp_ref, k_ref, v_ref, kc_ref, vc_ref, bcol_ref, o_ref, bias_ref, *, n_blocks, grid_rows):
    i = pl.program_id(2)
    n_keys = KEY_ROWS * GRID_W

    heads = [slice(hh * HEAD_DIM, (hh + 1) * HEAD_DIM) for hh in range(HEADS_PER_STEP)]

    def build_bias(blk):
        valid, ri = _attn_row_structure(blk, grid_rows)
        for hh in range(HEADS_PER_STEP):
            for jr in range(Q_ROWS):
                for kl in range(KEY_ROWS):
                    piece = (bcol_ref[hh, int(ri[jr, kl])] if valid[jr, kl]
                             else jnp.full((GRID_W, GRID_W), MASKED, F32))
                    bias_ref[hh, jr * GRID_W:(jr + 1) * GRID_W, kl * GRID_W:(kl + 1) * GRID_W] = piece

    for blk in sorted({0, min(1, n_blocks - 1), n_blocks - 1}):
        pl.when(i == blk)(functools.partial(build_bias, blk))

    @pl.when(i < n_blocks)
    def _():
        k_row0 = jnp.clip(i * Q_ROWS - WIN_ROWS // 2, 0, grid_rows - KEY_ROWS)
        start = pl.multiple_of(k_row0 * GRID_W, GRID_W)
        for hh, cols in enumerate(heads):
            k_loc = k_ref[pl.ds(start, n_keys), cols]
            v_loc = v_ref[pl.ds(start, n_keys), cols]
            s_loc = _qk(qr_ref[:, cols], k_loc) + bias_ref[hh]
            s_ctx = _qk(qp_ref[:, cols], kc_ref[:, cols])
            o_ref[:, cols] = _softmax_pv([s_loc, s_ctx], [v_loc, vc_ref[:, cols]]).astype(BF16)

    @pl.when(i >= n_blocks)
    def _():
        for cols in heads:
            s_ctx = _qk(qp_ref[:, cols], kc_ref[:, cols])
            o_ref[:, cols] = _softmax_pv([s_ctx], [vc_ref[:, cols]]).astype(BF16)


def _attn_bias_by_column(rpb, grid_rows):
    n_blocks = grid_rows // Q_ROWS
    assert min(WIN_ROWS, grid_rows) == WIN_ROWS and grid_rows >= KEY_ROWS and grid_rows % Q_ROWS == 0
    interior = _attn_row_structure(min(1, n_blocks - 1), grid_rows)
    for blk in range(1, n_blocks - 1):
        assert all(np.array_equal(a, b) for a, b in zip(_attn_row_structure(blk, grid_rows), interior))
    n_ci = 2 * WIN_COLS - 1
    qc = np.arange(GRID_W)
    c0 = np.clip(qc - WIN_COLS // 2, 0, GRID_W - WIN_COLS)
    col_valid = (qc[None, :] >= c0[:, None]) & (qc[None, :] < c0[:, None] + WIN_COLS)
    col_sel = (col_valid[:, :, None]
               & ((qc[None, :, None] - qc[:, None, None] + (WIN_COLS - 1)) == np.arange(n_ci))).astype(np.float32)
    by_col = jnp.einsum('hrc,qkc->hrqk', rpb.astype(F32), jnp.asarray(col_sel), precision=lax.Precision.HIGHEST)
    return jnp.where(jnp.asarray(col_valid)[None, None], by_col * LOG2_E, MASKED)


def _attn_call(qr, qp, kr, v, bias_by_col, *, batch, seq, ctx_len, with_ctx_queries):
    assert ctx_len == Q_BLOCK and seq % Q_BLOCK == 0
    grid_rows = seq // GRID_W
    n_blocks = grid_rows // Q_ROWS
    n_steps = n_blocks + (1 if with_ctx_queries else 0)
    ctx_block0 = batch * n_blocks
    n_out = batch * seq + (batch * ctx_len if with_ctx_queries else 0)
    n_keys = KEY_ROWS * GRID_W

    def q_map(b, h, i):
        return (jnp.where(i < n_blocks, b * n_blocks + i, ctx_block0 + b), h)

    width = HEADS_PER_STEP * HEAD_DIM
    return pl.pallas_call(
        functools.partial(_attn_body, n_blocks=n_blocks, grid_rows=grid_rows),
        grid=(batch, NA_HEADS // HEADS_PER_STEP, n_steps),
        in_specs=[pl.BlockSpec((Q_BLOCK, width), q_map),
                  pl.BlockSpec((Q_BLOCK, width), q_map),
                  pl.BlockSpec((seq, width), lambda b, h, i: (b, h)),
                  pl.BlockSpec((seq, width), lambda b, h, i: (b, h)),
                  pl.BlockSpec((ctx_len, width), lambda b, h, i: (ctx_block0 + b, h)),
                  pl.BlockSpec((ctx_len, width), lambda b, h, i: (ctx_block0 + b, h)),
                  pl.BlockSpec((HEADS_PER_STEP,) + bias_by_col.shape[1:], lambda b, h, i: (h, 0, 0, 0))],
        out_specs=pl.BlockSpec((Q_BLOCK, width), q_map),
        out_shape=jax.ShapeDtypeStruct((n_out, NA_HEADS * HEAD_DIM), BF16),
        scratch_shapes=[pltpu.VMEM((HEADS_PER_STEP, Q_BLOCK, n_keys), F32)],
        compiler_params=_params(3),
    )(qr, qp, kr, v, kr, v, bias_by_col)


def _merge_body(h_ref, a_ref, o_ref, wg_ref, bg_ref, wa_ref, wb_ref, m_ref, *, tn):
    d = m_ref.shape[1]
    hb, ab, ob = h_ref[...], a_ref[...], o_ref[...]
    for n0 in range(0, d, tn):
        cols = slice(n0, n0 + tn)
        gcols = slice(d + n0, d + n0 + tn)
        g_a = jax.nn.sigmoid(jnp.dot(hb, wg_ref[:, cols], preferred_element_type=F32) + bg_ref[:, cols])
        g_b = jax.nn.sigmoid(jnp.dot(hb, wg_ref[:, gcols], preferred_element_type=F32) + bg_ref[:, gcols])
        pa = jnp.dot(ab, wa_ref[:, cols], preferred_element_type=F32)
        pb = jnp.dot(ob, wb_ref[:, cols], preferred_element_type=F32)
        m_ref[:, cols] = (g_a * pa + g_b * pb).astype(BF16)


def _merge_call(h, a, o, w_gate_b, b_gate, w_a_b, w_b_b, *, layer, n_rows, tm):
    d = h.shape[1]

    def row_map(i):
        return (i, 0)

    return pl.pallas_call(
        functools.partial(_merge_body, tn=min(d, 512)),
        grid=(n_rows // tm,),
        in_specs=[pl.BlockSpec((tm, d), row_map),
                  pl.BlockSpec((tm, a.shape[1]), row_map),
                  pl.BlockSpec((tm, o.shape[1]), row_map),
                  _resident_layer(w_gate_b, layer), _resident((1, 2 * d)),
                  _resident_layer(w_a_b, layer), _resident_layer(w_b_b, layer)],
        out_specs=pl.BlockSpec((tm, d), row_map),
        out_shape=jax.ShapeDtypeStruct((n_rows, d), BF16),
        compiler_params=_params(1),
    )(h, a, o, w_gate_b, b_gate.reshape(1, 2 * d), w_a_b, w_b_b)


def _outproj_body(*refs, n_main_tiles, two_src, n_experts):
    if two_src:
        xa_ref, xb_ref = refs[:2]
        refs = refs[2:]
    else:
        xa_ref = refs[0]
        refs = refs[1:]
    (m_ref, wo_ref, mod_ref, nw_ref, wr_ref, br_ref,
     xo_ref, hp_ref, route_ref, route_t_ref, cnt_ref, carry_ref) = refs
    i = pl.program_id(0)
    if two_src:
        x = jnp.where(i < n_main_tiles, xa_ref[...], xb_ref[...])
    else:
        x = xa_ref[...]
    tm, d = x.shape

    @pl.when(i == 0)
    def _():
        carry_ref[...] = jnp.zeros_like(carry_ref)

    y = jnp.dot(m_ref[...], wo_ref[...], preferred_element_type=F32)
    x_new = x + mod_ref[0, 2:3, :] * y
    xo_ref[...] = x_new
    h = _rms_modulate(x_new, nw_ref[...], mod_ref[0, 3:4, :], mod_ref[0, 4:5, :])
    hp_ref[...] = _to_token_tiles(_pack_bf16_pair(h[:, :d // 2], h[:, d // 2:]))

    h_hi = h.astype(BF16)
    h_lo = (h - h_hi.astype(F32)).astype(BF16)
    by_hi = jnp.dot(h_hi, wr_ref[...], preferred_element_type=F32)
    logits = (by_hi[:, :LANES] + by_hi[:, LANES:]
              + jnp.dot(h_lo, wr_ref[:, :LANES], preferred_element_type=F32) + br_ref[...])
    lane = lax.broadcasted_iota(jnp.int32, logits.shape, 1)
    lane_f = lane.astype(F32)
    far = jnp.float32(4 * LANES)

    def first_argmax(vals):
        top = jnp.max(vals, axis=-1, keepdims=True)
        return top, jnp.min(jnp.where(vals == top, lane_f, far), axis=-1, keepdims=True)

    g_logits = jnp.where((lane >= n_experts) & (lane < n_experts + N_GROUPS), logits, MASKED)
    g_top, g_lane = first_argmax(g_logits)
    g_prob = 1.0 / jnp.sum(jnp.exp(g_logits - g_top), axis=-1, keepdims=True)
    e_lo = (g_lane - n_experts) * EXPERTS_PER_GROUP
    e_logits = jnp.where((lane_f >= e_lo) & (lane_f < e_lo + EXPERTS_PER_GROUP), logits, MASKED)
    top1, e1 = first_argmax(e_logits)
    top2, e2 = first_argmax(jnp.where(lane_f == e1, MASKED, e_logits))
    t = jnp.exp(top2 - top1)
    w1 = g_prob / (1.0 + t)
    w2 = g_prob * t / (1.0 + t)

    sel1, sel2 = lane_f == e1, lane_f == e2
    onehot = jnp.where(sel1 | sel2, 1.0, 0.0)
    r_i = lax.broadcasted_iota(jnp.int32, (tm, tm), 0)
    c_i = lax.broadcasted_iota(jnp.int32, (tm, tm), 1)
    earlier = jnp.where(c_i < r_i, 1.0, 0.0).astype(BF16)
    before = jnp.dot(earlier, onehot.astype(BF16), preferred_element_type=F32) + carry_ref[...]
    rank1 = jnp.sum(jnp.where(sel1, before, 0.0), axis=-1, keepdims=True)
    rank2 = jnp.sum(jnp.where(sel2, before, 0.0), axis=-1, keepdims=True)
    carry_ref[...] += jnp.sum(onehot, axis=0, keepdims=True)
    cnt_ref[...] = carry_ref[...]

    route = jnp.zeros_like(logits)
    for k, val in enumerate((e1, e2, w1, w2, rank1, rank2)):
        route = jnp.where(lane == k, val, route)
    route_ref[...] = route
    route_t_ref[...] = route.T[:8, :]


def _outproj_call(x_main, x_ctx, m, w_out_b, mod, norm_w, w_router, b_router, *, layer, rows_per_mod, tm,
                  n_experts):
    d = x_main.shape[1]
    two_src = x_ctx is not None
    n_main = x_main.shape[0] if two_src else m.shape[0]
    n_rows = m.shape[0]
    n_main_tiles = n_main // tm
    n_mod = mod.shape[0]
    tiles_per_mod = rows_per_mod // tm

    def row_map(i):
        return (i, 0)

    x_specs = [pl.BlockSpec((tm, d), lambda i: (jnp.minimum(i, n_main_tiles - 1), 0))]
    x_args = [x_main]
    if two_src:
        x_specs.append(pl.BlockSpec((tm, d), lambda i: (jnp.maximum(i - n_main_tiles, 0), 0),
                                    pipeline_mode=pl.Buffered(1)))
        x_args.append(x_ctx)
    in_specs = x_specs + [
        pl.BlockSpec((tm, d), row_map),
        _resident_layer(w_out_b, layer),
        pl.BlockSpec((1, 6, d), lambda i: (jnp.minimum(i // tiles_per_mod, n_mod - 1), 0, 0)),
        _resident((1, d)),
        _resident(w_router.shape),
        _resident((1, LANES)),
    ]
    return pl.pallas_call(
        functools.partial(_outproj_body, n_main_tiles=n_main_tiles, two_src=two_src, n_experts=n_experts),
        grid=(n_rows // tm,),
        in_specs=in_specs,
        out_specs=[pl.BlockSpec((tm, d), row_map),
                   pl.BlockSpec((tm, d // 2 // LANES, LANES), lambda i: (i, 0, 0)),
                   pl.BlockSpec((tm, LANES), row_map),
                   pl.BlockSpec((8, tm), lambda i: (0, i)),
                   pl.BlockSpec((1, LANES), lambda i: (0, 0))],
        out_shape=[jax.ShapeDtypeStruct((n_rows, d), F32),
                   jax.ShapeDtypeStruct((n_rows, d // 2 // LANES, LANES), U32),
                   jax.ShapeDtypeStruct((n_rows, LANES), F32),
                   jax.ShapeDtypeStruct((8, n_rows), F32),
                   jax.ShapeDtypeStruct((1, LANES), F32)],
        scratch_shapes=[pltpu.VMEM((1, LANES), F32)],
        compiler_params=_params(1),
    )(*x_args, m, w_out_b, mod, norm_w.reshape(1, d), w_router, b_router)


def _row_copy(src_ref, src_row, dst_ref, dst_row, sem):
    return pltpu.make_async_copy(src_ref.at[src_row], dst_ref.at[dst_row], sem)


def _to_token_tiles(rows):
    return rows.reshape(rows.shape[0], rows.shape[1] // LANES, LANES)


def _from_token_tiles(tiles):
    return tiles.reshape(tiles.shape[0], tiles.shape[1] * LANES)


def _slot_rows_body(p1_ref, p2_ref, init_ref, dst_ref, stage_ref, sem, *, tokens_per_step, row_stride):
    i = pl.program_id(0)

    @pl.when(i == 0)
    def _():
        load = pltpu.make_async_copy(init_ref, stage_ref, sem)
        load.start()
        load.wait()

    def put(r, carry):
        t = i * tokens_per_step + r
        stage_ref[p1_ref[t]] = t
        stage_ref[p2_ref[t]] = row_stride + t
        return carry

    lax.fori_loop(0, tokens_per_step, put, 0, unroll=8)

    @pl.when(i == pl.num_programs(0) - 1)
    def _():
        store = pltpu.make_async_copy(stage_ref, dst_ref, sem)
        store.start()
        store.wait()


def _slot_rows_call(pos1, pos2, init_rows, *, row_stride, tokens_per_step):
    n_tok = pos1.shape[0]
    n_map = init_rows.shape[0]
    return pl.pallas_call(
        functools.partial(_slot_rows_body, tokens_per_step=tokens_per_step, row_stride=row_stride),
        grid_spec=pltpu.PrefetchScalarGridSpec(
            num_scalar_prefetch=2,
            grid=(n_tok // tokens_per_step,),
            in_specs=[pl.BlockSpec(memory_space=pl.ANY)],
            out_specs=pl.BlockSpec(memory_space=pl.ANY),
            scratch_shapes=[pltpu.SMEM((n_map,), jnp.int32), pltpu.SemaphoreType.DMA(())],
        ),
        out_shape=jax.ShapeDtypeStruct((n_map,), jnp.int32),
        compiler_params=_params(1),
    )(pos1, pos2, init_rows)


def _ffn_body(first_ref, count_ref, src_ref, dst_ref, h_ref, wg_ref, wu_ref, wd_ref, yk_ref,
              wgb_ref, wub_ref, wdb_ref, xbuf, ybuf, sem_g, sem_s, *, tm, spare_rows):
    e = pl.program_id(0)
    n_experts = pl.num_programs(0)
    half = xbuf.shape[2] * LANES

    def gather_row(tile, r, buf):
        return _row_copy(h_ref, src_ref[(tile + 1) * tm + r], xbuf.at[buf], r, sem_g.at[buf])

    def scatter_row(tile, r, buf):
        return _row_copy(ybuf.at[buf], r, yk_ref, dst_ref[(tile + 1) * tm + r], sem_s.at[buf])

    def wait_tile(buf_ref, sem):
        pltpu.make_async_copy(buf_ref, buf_ref, sem).wait()

    @pl.when(e == 0)
    def _():
        ybuf[...] = jnp.zeros_like(ybuf)

        def prime(r, carry):
            _row_copy(ybuf.at[0], r, yk_ref, spare_rows + r, sem_s.at[0]).start()
            gather_row(0, r, 0).start()
            return carry

        lax.fori_loop(0, tm, prime, 0)

    wgb_ref[...] = wg_ref[0, 0].astype(BF16)
    wub_ref[...] = wu_ref[0, 0].astype(BF16)
    wdb_ref[...] = wd_ref[0, 0].astype(BF16)

    def run_tile(j, carry):
        slot = j % 2
        other = 1 - slot
        wait_tile(xbuf.at[slot], sem_g.at[slot])
        lo, hi = _unpack_bf16_pair(_from_token_tiles(xbuf[slot]))
        lo, hi = lo.astype(BF16), hi.astype(BF16)
        wait_tile(ybuf.at[slot], sem_s.at[slot])
        for r in range(tm):
            gather_row(j + 1, r, other).start()
            scatter_row(j - 1, r, other).start()

        def up(w_ref):
            return (jnp.dot(lo, w_ref[:half, :], preferred_element_type=F32)
                    + jnp.dot(hi, w_ref[half:, :], preferred_element_type=F32))

        g = up(wgb_ref)
        hid = (g * jax.nn.sigmoid(g) * up(wub_ref)).astype(BF16)
        y = jnp.dot(hid, wdb_ref[...], preferred_element_type=F32)
        ybuf[slot] = _to_token_tiles(_pack_bf16_pair(y[:, :half], y[:, half:]))
        return carry

    first = first_ref[e]
    lax.fori_loop(first, first + count_ref[e], run_tile, 0)

    @pl.when(e == n_experts - 1)
    def _():
        j = first + count_ref[e] - 1
        slot = j % 2
        other = 1 - slot

        def flush(r, carry):
            scatter_row(j, r, slot).start()
            return carry

        wait_tile(ybuf.at[other], sem_s.at[other])
        lax.fori_loop(0, tm, flush, 0)
        wait_tile(ybuf.at[slot], sem_s.at[slot])
        wait_tile(xbuf.at[other], sem_g.at[other])


def _ffn_call(first_tile, tile_count, dst_rows, h_packed, w_gate, w_up, w_down, *, layer, tm, row_stride,
              n_out_rows):
    _, sub, _ = h_packed.shape
    _, n_experts, d, de = w_gate.shape
    src_rows = jnp.where(dst_rows < 2 * row_stride, dst_rows % row_stride, 0)

    def w_map(e, ft, tc, sr, dr):
        return (layer, e, 0, 0)

    return pl.pallas_call(
        functools.partial(_ffn_body, tm=tm, spare_rows=2 * row_stride),
        grid_spec=pltpu.PrefetchScalarGridSpec(
            num_scalar_prefetch=4,
            grid=(n_experts,),
            in_specs=[pl.BlockSpec(memory_space=pl.ANY),
                      pl.BlockSpec((1, 1, d, de), w_map),
                      pl.BlockSpec((1, 1, d, de), w_map),
                      pl.BlockSpec((1, 1, de, d), w_map)],
            out_specs=pl.BlockSpec(memory_space=pl.ANY),
            scratch_shapes=[pltpu.VMEM((d, de), BF16), pltpu.VMEM((d, de), BF16), pltpu.VMEM((de, d), BF16),
                            pltpu.VMEM((2, tm, sub, LANES), U32), pltpu.VMEM((2, tm, sub, LANES), U32),
                            pltpu.SemaphoreType.DMA((2,)), pltpu.SemaphoreType.DMA((2,))],
        ),
        out_shape=jax.ShapeDtypeStruct((n_out_rows, sub, LANES), U32),
        compiler_params=_params(1),
    )(first_tile, tile_count, src_rows, dst_rows, h_packed, w_gate, w_up, w_down)


def _combine_body(x_ref, y1_ref, y2_ref, route_ref, mod_ref, fw_ref, o_ref, *, final_norm):
    w1, w2 = route_ref[:, 2:3], route_ref[:, 3:4]
    lo1, hi1 = _unpack_bf16_pair(_from_token_tiles(y1_ref[...]))
    lo2, hi2 = _unpack_bf16_pair(_from_token_tiles(y2_ref[...]))
    y = jnp.concatenate([w1 * lo1 + w2 * lo2, w1 * hi1 + w2 * hi2], axis=-1)
    x_new = x_ref[...] + mod_ref[0, 5:6, :] * y
    if final_norm:
        x_new = x_new * lax.rsqrt(jnp.mean(x_new * x_new, axis=-1, keepdims=True) + RMS_EPS) * fw_ref[...]
    o_ref[...] = x_new


def _combine_call(x, yk, route, mod, final_w, *, rows_per_mod, tm, row_stride, final_norm):
    n_rows, d = x.shape
    sub = yk.shape[1]
    n_mod = mod.shape[0]
    tiles_per_mod = rows_per_mod // tm
    second = row_stride // tm

    def row_map(i):
        return (i, 0)

    return pl.pallas_call(
        functools.partial(_combine_body, final_norm=final_norm),
        grid=(n_rows // tm,),
        in_specs=[pl.BlockSpec((tm, d), row_map),
                  pl.BlockSpec((tm, sub, LANES), lambda i: (i, 0, 0)),
                  pl.BlockSpec((tm, sub, LANES), lambda i: (second + i, 0, 0)),
                  pl.BlockSpec((tm, LANES), row_map),
                  pl.BlockSpec((1, 6, d), lambda i: (jnp.minimum(i // tiles_per_mod, n_mod - 1), 0, 0)),
                  pl.BlockSpec((1, d), lambda i: (0, 0))],
        out_specs=pl.BlockSpec((tm, d), row_map),
        out_shape=jax.ShapeDtypeStruct((n_rows, d), F32),
        compiler_params=_params(1),
    )(x, yk, yk, route, mod, final_w.reshape(1, d))


def _slot_plan(route_t, counts, n_experts, tm):
    n_rows = route_t.shape[1]
    n_tiles = (2 * n_rows) // tm + n_experts
    cnt = counts[0, :n_experts].astype(jnp.int32)
    padded = ((cnt + tm - 1) // tm) * tm
    ends = jnp.cumsum(padded)
    starts = ends - padded
    fields = route_t.astype(jnp.int32)
    pos1 = starts[fields[0]] + fields[4]
    pos2 = starts[fields[1]] + fields[5]
    first_tile = (starts // tm).astype(jnp.int32)
    tile_count = (padded // tm).astype(jnp.int32)
    row_stride = n_rows
    spare = 2 * row_stride
    n_map = pl.cdiv((n_tiles + 2) * tm, 1024) * 1024
    init_rows = spare + jnp.arange(n_map, dtype=jnp.int32) % tm
    dst_rows = _slot_rows_call(pos1 + tm, pos2 + tm, init_rows, row_stride=row_stride, tokens_per_step=2 * tm)
    n_out_rows = spare + tm
    return first_tile, tile_count, dst_rows, row_stride, n_out_rows


def _rope_tables(batch, seq, ctx_rows):
    pairs = HEAD_DIM // 4
    t = np.arange(seq)
    pos = np.stack([t // GRID_W, t % GRID_W], axis=-1).astype(np.float32)
    inv_freq = (ROPE_THETA ** (-np.arange(pairs, dtype=np.float32) / pairs)).astype(np.float32)
    ang = pos[:, :, None] * inv_freq
    cos = np.repeat(np.cos(ang)[:, :, None, :], 2, axis=2).reshape(seq, HEAD_DIM)
    sin = np.sin(ang)
    zero = np.zeros_like(sin)
    sa = np.stack([-sin, zero], axis=2).reshape(seq, HEAD_DIM)
    sb = np.stack([zero, sin], axis=2).reshape(seq, HEAD_DIM)

    def full(tab, fill):
        return jnp.asarray(np.concatenate([np.tile(tab, (batch, 1)),
                                           np.full((ctx_rows, HEAD_DIM), fill, np.float32)]), F32)

    return full(cos, 1.0), full(sa, 0.0), full(sb, 0.0)


def kernel(x, c, ctx, c_ctx, w_ada, b_ada, norm_mix_w, norm_ffn_w, w_in, sgu_norm_w, sgu_w_s, sgu_b_s, na_rpb,
           w_merge_gate, b_merge_gate, w_branch_a, w_branch_b, w_out, w_router_group, b_router_group,
           w_router_expert, b_router_expert, w_exp_gate, w_exp_up, w_exp_down, final_norm_w):
    batch, seq, d = x.shape
    ctx_len = ctx.shape[1]
    depth = w_ada.shape[0]
    n_experts = w_exp_gate.shape[1]
    n_lat = batch * seq
    n_ctx = batch * ctx_len
    tm = 256
    tm_wide = 512
    assert seq % tm_wide == 0 and n_ctx % tm_wide == 0 and tm % SGU_CHUNK == 0 and ctx_len % SGU_CHUNK == 0

    mods = _ada_call(jnp.concatenate([c, c_ctx[None]], axis=0), w_ada, b_ada)
    cos, sa, sb = _rope_tables(batch, seq, n_ctx)
    grid_rows = seq // GRID_W

    x_cur = x.reshape(n_lat, d)
    ctx_rows = ctx.reshape(n_ctx, d)
    w_in_b, w_gate_b, w_a_b, w_b_b, w_out_b = (w.astype(BF16) for w in
                                               (w_in, w_merge_gate, w_branch_a, w_branch_b, w_out))
    x_all = None
    for l in range(depth):
        last = l == depth - 1
        mod = mods[l, :batch + 1].reshape(batch + 1, 6, d)
        w_s_b = sgu_w_s[l].astype(BF16)
        b_s_t = sgu_b_s[l].T
        if x_all is None:
            src = (x_cur, ctx_rows)
        else:
            src = (x_all, None)
        h, a, qp, qr, kr, v = _inproj_call(src[0], src[1], mod, norm_mix_w[l], w_in_b, sgu_norm_w[l], w_s_b, b_s_t,
                                           cos, sa, sb, layer=l, rows_per_mod=seq, tm=tm_wide)
        o = _attn_call(qr, qp, kr, v, _attn_bias_by_column(na_rpb[l], grid_rows), batch=batch, seq=seq,
                       ctx_len=ctx_len, with_ctx_queries=not last)
        n_rows = n_lat if last else n_lat + n_ctx
        m = _merge_call(h, a, o, w_gate_b, b_merge_gate[l], w_a_b, w_b_b, layer=l, n_rows=n_rows, tm=tm_wide)
        w_router = jnp.zeros((d, LANES), F32)
        w_router = w_router.at[:, :n_experts].set(w_router_expert[l])
        w_router = w_router.at[:, n_experts:n_experts + N_GROUPS].set(w_router_group[l])
        w_router_hi = w_router.astype(BF16)
        w_router_lo = (w_router - w_router_hi.astype(F32)).astype(BF16)
        w_router = jnp.concatenate([w_router_hi, w_router_lo], axis=1)
        b_router = jnp.zeros((1, LANES), F32)
        b_router = b_router.at[0, :n_experts].set(b_router_expert[l])
        b_router = b_router.at[0, n_experts:n_experts + N_GROUPS].set(b_router_group[l])
        x_mid, h_packed, route, route_t, counts = _outproj_call(
            src[0], src[1], m, w_out_b, mod, norm_ffn_w[l], w_router, b_router, layer=l, rows_per_mod=seq,
            tm=tm_wide, n_experts=n_experts)
        first_tile, tile_count, dst_rows, row_stride, n_out_rows = _slot_plan(route_t, counts, n_experts, tm)
        yk = _ffn_call(first_tile, tile_count, dst_rows, h_packed, w_exp_gate, w_exp_up, w_exp_down, layer=l, tm=tm,
                       row_stride=row_stride, n_out_rows=n_out_rows)
        x_all = _combine_call(x_mid, yk, route, mod, final_norm_w, rows_per_mod=seq, tm=tm,
                              row_stride=row_stride, final_norm=last)
    return x_all.reshape(batch, seq, d)
```

```python
import functools

import numpy as np
import jax
import jax.numpy as jnp
from jax import lax
from jax.experimental import pallas as pl
from jax.experimental.pallas import tpu as pltpu

GRID_W = 64
SGU_CHUNK = 128
SGU_GROUPS = 8
NA_HEADS = 8
HEAD_DIM = 128
WIN_ROWS = 8
WIN_COLS = 16
ROPE_THETA = 10000.0
N_GROUPS = 4
EXPERTS_PER_GROUP = 8
RMS_EPS = 1e-6

LANES = 128
Q_ROWS = 4
Q_BLOCK = Q_ROWS * GRID_W
KEY_ROWS = Q_ROWS + WIN_ROWS - 1
HEADS_PER_STEP = 4
MASKED = -1e30
LOG2_E = 1.4426950408889634
QK_SCALE = HEAD_DIM ** -0.5 * LOG2_E
VMEM_LIMIT = 56 * 1024 * 1024

BF16 = jnp.bfloat16
F32 = jnp.float32
U32 = jnp.uint32


def _params(n_grid_dims, vmem=VMEM_LIMIT):
    return pltpu.CompilerParams(dimension_semantics=("arbitrary",) * n_grid_dims, vmem_limit_bytes=vmem)


def _resident(shape):
    nd = len(shape)
    return pl.BlockSpec(shape, lambda *_: (0,) * nd, pipeline_mode=pl.Buffered(1))


def _resident_layer(stacked, layer):
    nd = stacked.ndim
    return pl.BlockSpec((None,) + stacked.shape[1:], lambda *_: (layer,) + (0,) * (nd - 1),
                        pipeline_mode=pl.Buffered(1))


def _pack_bf16_pair(lo, hi):
    lo_bits = lax.bitcast_convert_type(lo.astype(BF16).astype(F32), U32)
    hi_bits = lax.bitcast_convert_type(hi.astype(BF16).astype(F32), U32)
    return (hi_bits & jnp.uint32(0xFFFF0000)) | (lo_bits >> 16)


def _unpack_bf16_pair(w):
    lo = lax.bitcast_convert_type(w << 16, F32)
    hi = lax.bitcast_convert_type(w & jnp.uint32(0xFFFF0000), F32)
    return lo, hi


def _rms_modulate(x, norm_w, shift, scale):
    y = x * lax.rsqrt(jnp.mean(x * x, axis=-1, keepdims=True) + RMS_EPS) * norm_w
    return y * (1.0 + scale) + shift


def _ada_body(ct_ref, w_ref, b_ref, o_ref, *, n_rows):
    s = ct_ref[...]
    s = s * jax.nn.sigmoid(s)
    w = w_ref[0]
    o_ref[...] = jnp.zeros_like(o_ref)
    for r in range(n_rows):
        o_ref[0, r:r + 1, :] = jnp.sum(w * s[:, r:r + 1], axis=0, keepdims=True) + b_ref[0]


def _ada_call(cond, w_ada, b_ada):
    n_rows, d = cond.shape
    depth, _, n = w_ada.shape
    tn = next(t for t in (1024, 512, 256, LANES) if n % t == 0)
    ct = jnp.zeros((d, 8), F32).at[:, :n_rows].set(cond.T)
    return pl.pallas_call(
        functools.partial(_ada_body, n_rows=n_rows),
        grid=(depth, n // tn),
        in_specs=[pl.BlockSpec((d, 8), lambda l, j: (0, 0)),
                  pl.BlockSpec((1, d, tn), lambda l, j: (l, 0, j)),
                  pl.BlockSpec((1, 1, tn), lambda l, j: (l, 0, j))],
        out_specs=pl.BlockSpec((1, 8, tn), lambda l, j: (l, 0, j)),
        out_shape=jax.ShapeDtypeStruct((depth, 8, n), F32),
        compiler_params=_params(2),
    )(ct, w_ada, b_ada.reshape(depth, 1, n))


def _inproj_body(*refs, n_main_tiles, two_src, sgu_w, na_w):
    if two_src:
        xa_ref, xb_ref = refs[:2]
        refs = refs[2:]
    else:
        xa_ref = refs[0]
        refs = refs[1:]
    (mod_ref, nw_ref, w_ref, snw_ref, ws_ref, bst_ref, cos_ref, sa_ref, sb_ref,
     h_ref, a_ref, qp_ref, qr_ref, kr_ref, v_ref) = refs
    if two_src:
        x = jnp.where(pl.program_id(0) < n_main_tiles, xa_ref[...], xb_ref[...])
    else:
        x = xa_ref[...]
    tm = x.shape[0]
    h = _rms_modulate(x, nw_ref[...], mod_ref[0, 0:1, :], mod_ref[0, 1:2, :])
    hb = h.astype(BF16)
    h_ref[...] = hb

    def proj(lo, width):
        return jnp.dot(hb, w_ref[:, lo:lo + width], preferred_element_type=F32)

    u = jax.nn.gelu(proj(0, sgu_w))
    v = jax.nn.gelu(proj(sgu_w, sgu_w))
    vn = v * lax.rsqrt(jnp.mean(v * v, axis=-1, keepdims=True) + RMS_EPS) * snw_ref[...]
    vnb = vn.astype(BF16)
    gch = sgu_w // SGU_GROUPS
    for c in range(tm // SGU_CHUNK):
        rows = slice(c * SGU_CHUNK, (c + 1) * SGU_CHUNK)
        for g in range(SGU_GROUPS):
            cols = slice(g * gch, (g + 1) * gch)
            z = jnp.dot(ws_ref[g], vnb[rows, cols], preferred_element_type=F32) + bst_ref[:, g:g + 1]
            a_ref[rows, cols] = (u[rows, cols] * z).astype(BF16)

    cos, sa, sb = cos_ref[...], sa_ref[...], sb_ref[...]

    def rope_into(p, out_ref):
        for hh in range(NA_HEADS):
            cols = slice(hh * HEAD_DIM, (hh + 1) * HEAD_DIM)
            xh = p[:, cols]
            out_ref[:, cols] = (xh * cos + pltpu.roll(xh, HEAD_DIM - 32, 1) * sa
                                + pltpu.roll(xh, 32, 1) * sb).astype(BF16)

    q = proj(2 * sgu_w, na_w) * QK_SCALE
    qp_ref[...] = q.astype(BF16)
    rope_into(q, qr_ref)
    rope_into(proj(2 * sgu_w + na_w, na_w), kr_ref)
    v_ref[...] = proj(2 * sgu_w + 2 * na_w, na_w).astype(BF16)


def _inproj_call(x_main, x_ctx, mod, norm_w, w_in_b, sgu_norm_w, w_s_b, b_s_t, cos, sa, sb,
                 *, layer, rows_per_mod, tm):
    d = x_main.shape[1]
    n_main = x_main.shape[0]
    two_src = x_ctx is not None
    tt = n_main + (x_ctx.shape[0] if two_src else 0)
    n_main_tiles = n_main // tm
    n_mod = mod.shape[0]
    sgu_w = sgu_norm_w.shape[-1]
    na_w = NA_HEADS * HEAD_DIM
    tiles_per_mod = rows_per_mod // tm

    def row_map(i):
        return (i, 0)

    x_specs = [pl.BlockSpec((tm, d), lambda i: (jnp.minimum(i, n_main_tiles - 1), 0))]
    x_args = [x_main]
    if two_src:
        x_specs.append(pl.BlockSpec((tm, d), lambda i: (jnp.maximum(i - n_main_tiles, 0), 0),
                                    pipeline_mode=pl.Buffered(1)))
        x_args.append(x_ctx)
    in_specs = x_specs + [
        pl.BlockSpec((1, 6, d), lambda i: (jnp.minimum(i // tiles_per_mod, n_mod - 1), 0, 0)),
        _resident((1, d)),
        _resident_layer(w_in_b, layer),
        _resident((1, sgu_w)),
        _resident(w_s_b.shape),
        _resident(b_s_t.shape),
        pl.BlockSpec((tm, HEAD_DIM), row_map),
        pl.BlockSpec((tm, HEAD_DIM), row_map),
        pl.BlockSpec((tm, HEAD_DIM), row_map),
    ]
    out_widths = [d, sgu_w, na_w, na_w, na_w, na_w]
    return pl.pallas_call(
        functools.partial(_inproj_body, n_main_tiles=n_main_tiles, two_src=two_src, sgu_w=sgu_w, na_w=na_w),
        grid=(tt // tm,),
        in_specs=in_specs,
        out_specs=[pl.BlockSpec((tm, w), row_map) for w in out_widths],
        out_shape=[jax.ShapeDtypeStruct((tt, w), BF16) for w in out_widths],
        compiler_params=_params(1),
    )(*x_args, mod, norm_w.reshape(1, d), w_in_b, sgu_norm_w.reshape(1, sgu_w), w_s_b, b_s_t, cos, sa, sb)


def _softmax_pv(scores, values):
    m = functools.reduce(jnp.maximum, [jnp.max(s, axis=-1, keepdims=True) for s in scores])
    ps = [jnp.exp2(s - m) for s in scores]
    denom = functools.reduce(jnp.add, [jnp.sum(p, axis=-1, keepdims=True) for p in ps])
    acc = functools.reduce(jnp.add, [jnp.dot(p.astype(BF16), v, preferred_element_type=F32)
                                     for p, v in zip(ps, values)])
    return acc / denom


def _qk(q, k):
    return lax.dot_general(q, k, (((1,), (1,)), ((), ())), preferred_element_type=F32)


def _attn_row_structure(blk, grid_rows):
    wr = min(WIN_ROWS, grid_rows)
    k_row0 = int(np.clip(blk * Q_ROWS - WIN_ROWS // 2, 0, grid_rows - KEY_ROWS))
    qr = blk * Q_ROWS + np.arange(Q_ROWS)
    kr = k_row0 + np.arange(KEY_ROWS)
    r0 = np.clip(qr - wr // 2, 0, grid_rows - wr)
    valid = (kr[None, :] >= r0[:, None]) & (kr[None, :] < r0[:, None] + wr)
    ri = np.clip(kr[None, :] - qr[:, None] + (WIN_ROWS - 1), 0, 2 * WIN_ROWS - 2)
    return valid, ri


def _attn_body(qr_ref, qp_ref, k_ref, v_ref, kc_ref, vc_ref, bcol_ref, o_ref, bias_ref, *, n_blocks, grid_rows):
    i = pl.program_id(2)
    n_keys = KEY_ROWS * GRID_W

    heads = [slice(hh * HEAD_DIM, (hh + 1) * HEAD_DIM) for hh in range(HEADS_PER_STEP)]

    def build_bias(blk):
        valid, ri = _attn_row_structure(blk, grid_rows)
        for hh in range(HEADS_PER_STEP):
            for jr in range(Q_ROWS):
                for kl in range(KEY_ROWS):
                    piece = (bcol_ref[hh, int(ri[jr, kl])] if valid[jr, kl]
                             else jnp.full((GRID_W, GRID_W), MASKED, F32))
                    bias_ref[hh, jr * GRID_W:(jr + 1) * GRID_W, kl * GRID_W:(kl + 1) * GRID_W] = piece

    for blk in sorted({0, min(1, n_blocks - 1), n_blocks - 1}):
        pl.when(i == blk)(functools.partial(build_bias, blk))

    @pl.when(i < n_blocks)
    def _():
        k_row0 = jnp.clip(i * Q_ROWS - WIN_ROWS // 2, 0, grid_rows - KEY_ROWS)
        start = pl.multiple_of(k_row0 * GRID_W, GRID_W)
        for hh, cols in enumerate(heads):
            k_loc = k_ref[pl.ds(start, n_keys), cols]
            v_loc = v_ref[pl.ds(start, n_keys), cols]
            s_loc = _qk(qr_ref[:, cols], k_loc) + bias_ref[hh]
            s_ctx = _qk(qp_ref[:, cols], kc_ref[:, cols])
            o_ref[:, cols] = _softmax_pv([s_loc, s_ctx], [v_loc, vc_ref[:, cols]]).astype(BF16)

    @pl.when(i >= n_blocks)
    def _():
        for cols in heads:
            s_ctx = _qk(qp_ref[:, cols], kc_ref[:, cols])
            o_ref[:, cols] = _softmax_pv([s_ctx], [vc_ref[:, cols]]).astype(BF16)


def _attn_bias_by_column(rpb, grid_rows):
    n_blocks = grid_rows // Q_ROWS
    assert min(WIN_ROWS, grid_rows) == WIN_ROWS and grid_rows >= KEY_ROWS and grid_rows % Q_ROWS == 0
    interior = _attn_row_structure(min(1, n_blocks - 1), grid_rows)
    for blk in range(1, n_blocks - 1):
        assert all(np.array_equal(a, b) for a, b in zip(_attn_row_structure(blk, grid_rows), interior))
    n_ci = 2 * WIN_COLS - 1
    qc = np.arange(GRID_W)
    c0 = np.clip(qc - WIN_COLS // 2, 0, GRID_W - WIN_COLS)
    col_valid = (qc[None, :] >= c0[:, None]) & (qc[None, :] < c0[:, None] + WIN_COLS)
    col_sel = (col_valid[:, :, None]
               & ((qc[None, :, None] - qc[:, None, None] + (WIN_COLS - 1)) == np.arange(n_ci))).astype(np.float32)
    by_col = jnp.einsum('hrc,qkc->hrqk', rpb.astype(F32), jnp.asarray(col_sel), precision=lax.Precision.HIGHEST)
    return jnp.where(jnp.asarray(col_valid)[None, None], by_col * LOG2_E, MASKED)


def _attn_call(qr, qp, kr, v, bias_by_col, *, batch, seq, ctx_len, with_ctx_queries):
    assert ctx_len == Q_BLOCK and seq % Q_BLOCK == 0
    grid_rows = seq // GRID_W
    n_blocks = grid_rows // Q_ROWS
    n_steps = n_blocks + (1 if with_ctx_queries else 0)
    ctx_block0 = batch * n_blocks
    n_out = batch * seq + (batch * ctx_len if with_ctx_queries else 0)
    n_keys = KEY_ROWS * GRID_W

    def q_map(b, h, i):
        return (jnp.where(i < n_blocks, b * n_blocks + i, ctx_block0 + b), h)

    width = HEADS_PER_STEP * HEAD_DIM
    return pl.pallas_call(
        functools.partial(_attn_body, n_blocks=n_blocks, grid_rows=grid_rows),
        grid=(batch, NA_HEADS // HEADS_PER_STEP, n_steps),
        in_specs=[pl.BlockSpec((Q_BLOCK, width), q_map),
                  pl.BlockSpec((Q_BLOCK, width), q_map),
                  pl.BlockSpec((seq, width), lambda b, h, i: (b, h)),
                  pl.BlockSpec((seq, width), lambda b, h, i: (b, h)),
                  pl.BlockSpec((ctx_len, width), lambda b, h, i: (ctx_block0 + b, h)),
                  pl.BlockSpec((ctx_len, width), lambda b, h, i: (ctx_block0 + b, h)),
                  pl.BlockSpec((HEADS_PER_STEP,) + bias_by_col.shape[1:], lambda b, h, i: (h, 0, 0, 0))],
        out_specs=pl.BlockSpec((Q_BLOCK, width), q_map),
        out_shape=jax.ShapeDtypeStruct((n_out, NA_HEADS * HEAD_DIM), BF16),
        scratch_shapes=[pltpu.VMEM((HEADS_PER_STEP, Q_BLOCK, n_keys), F32)],
        compiler_params=_params(3),
    )(qr, qp, kr, v, kr, v, bias_by_col)


def _merge_body(h_ref, a_ref, o_ref, wg_ref, bg_ref, wa_ref, wb_ref, m_ref, *, tn):
    d = m_ref.shape[1]
    hb, ab, ob = h_ref[...], a_ref[...], o_ref[...]
    for n0 in range(0, d, tn):
        cols = slice(n0, n0 + tn)
        gcols = slice(d + n0, d + n0 + tn)
        g_a = jax.nn.sigmoid(jnp.dot(hb, wg_ref[:, cols], preferred_element_type=F32) + bg_ref[:, cols])
        g_b = jax.nn.sigmoid(jnp.dot(hb, wg_ref[:, gcols], preferred_element_type=F32) + bg_ref[:, gcols])
        pa = jnp.dot(ab, wa_ref[:, cols], preferred_element_type=F32)
        pb = jnp.dot(ob, wb_ref[:, cols], preferred_element_type=F32)
        m_ref[:, cols] = (g_a * pa + g_b * pb).astype(BF16)


def _merge_call(h, a, o, w_gate_b, b_gate, w_a_b, w_b_b, *, layer, n_rows, tm):
    d = h.shape[1]

    def row_map(i):
        return (i, 0)

    return pl.pallas_call(
        functools.partial(_merge_body, tn=min(d, 512)),
        grid=(n_rows // tm,),
        in_specs=[pl.BlockSpec((tm, d), row_map),
                  pl.BlockSpec((tm, a.shape[1]), row_map),
                  pl.BlockSpec((tm, o.shape[1]), row_map),
                  _resident_layer(w_gate_b, layer), _resident((1, 2 * d)),
                  _resident_layer(w_a_b, layer), _resident_layer(w_b_b, layer)],
        out_specs=pl.BlockSpec((tm, d), row_map),
        out_shape=jax.ShapeDtypeStruct((n_rows, d), BF16),
        compiler_params=_params(1),
    )(h, a, o, w_gate_b, b_gate.reshape(1, 2 * d), w_a_b, w_b_b)


def _outproj_body(*refs, n_main_tiles, two_src, n_experts):
    if two_src:
        xa_ref, xb_ref = refs[:2]
        refs = refs[2:]
    else:
        xa_ref = refs[0]
        refs = refs[1:]
    (m_ref, wo_ref, mod_ref, nw_ref, wr_ref, br_ref,
     xo_ref, hp_ref, route_ref, route_t_ref, cnt_ref, carry_ref) = refs
    i = pl.program_id(0)
    if two_src:
        x = jnp.where(i < n_main_tiles, xa_ref[...], xb_ref[...])
    else:
        x = xa_ref[...]
    tm, d = x.shape

    @pl.when(i == 0)
    def _():
        carry_ref[...] = jnp.zeros_like(carry_ref)

    y = jnp.dot(m_ref[...], wo_ref[...], preferred_element_type=F32)
    x_new = x + mod_ref[0, 2:3, :] * y
    xo_ref[...] = x_new
    h = _rms_modulate(x_new, nw_ref[...], mod_ref[0, 3:4, :], mod_ref[0, 4:5, :])
    hp_ref[...] = _to_token_tiles(_pack_bf16_pair(h[:, :d // 2], h[:, d // 2:]))

    h_hi = h.astype(BF16)
    h_lo = (h - h_hi.astype(F32)).astype(BF16)
    by_hi = jnp.dot(h_hi, wr_ref[...], preferred_element_type=F32)
    logits = (by_hi[:, :LANES] + by_hi[:, LANES:]
              + jnp.dot(h_lo, wr_ref[:, :LANES], preferred_element_type=F32) + br_ref[...])
    lane = lax.broadcasted_iota(jnp.int32, logits.shape, 1)
    lane_f = lane.astype(F32)
    far = jnp.float32(4 * LANES)

    def first_argmax(vals):
        top = jnp.max(vals, axis=-1, keepdims=True)
        return top, jnp.min(jnp.where(vals == top, lane_f, far), axis=-1, keepdims=True)

    g_logits = jnp.where((lane >= n_experts) & (lane < n_experts + N_GROUPS), logits, MASKED)
    g_top, g_lane = first_argmax(g_logits)
    g_prob = 1.0 / jnp.sum(jnp.exp(g_logits - g_top), axis=-1, keepdims=True)
    e_lo = (g_lane - n_experts) * EXPERTS_PER_GROUP
    e_logits = jnp.where((lane_f >= e_lo) & (lane_f < e_lo + EXPERTS_PER_GROUP), logits, MASKED)
    top1, e1 = first_argmax(e_logits)
    top2, e2 = first_argmax(jnp.where(lane_f == e1, MASKED, e_logits))
    t = jnp.exp(top2 - top1)
    w1 = g_prob / (1.0 + t)
    w2 = g_prob * t / (1.0 + t)

    sel1, sel2 = lane_f == e1, lane_f == e2
    onehot = jnp.where(sel1 | sel2, 1.0, 0.0)
    r_i = lax.broadcasted_iota(jnp.int32, (tm, tm), 0)
    c_i = lax.broadcasted_iota(jnp.int32, (tm, tm), 1)
    earlier = jnp.where(c_i < r_i, 1.0, 0.0).astype(BF16)
    before = jnp.dot(earlier, onehot.astype(BF16), preferred_element_type=F32) + carry_ref[...]
    rank1 = jnp.sum(jnp.where(sel1, before, 0.0), axis=-1, keepdims=True)
    rank2 = jnp.sum(jnp.where(sel2, before, 0.0), axis=-1, keepdims=True)
    carry_ref[...] += jnp.sum(onehot, axis=0, keepdims=True)
    cnt_ref[...] = carry_ref[...]

    route = jnp.zeros_like(logits)
    for k, val in enumerate((e1, e2, w1, w2, rank1, rank2)):
        route = jnp.where(lane == k, val, route)
    route_ref[...] = route
    route_t_ref[...] = route.T[:8, :]


def _outproj_call(x_main, x_ctx, m, w_out_b, mod, norm_w, w_router, b_router, *, layer, rows_per_mod, tm,
                  n_experts):
    d = x_main.shape[1]
    two_src = x_ctx is not None
    n_main = x_main.shape[0] if two_src else m.shape[0]
    n_rows = m.shape[0]
    n_main_tiles = n_main // tm
    n_mod = mod.shape[0]
    tiles_per_mod = rows_per_mod // tm

    def row_map(i):
        return (i, 0)

    x_specs = [pl.BlockSpec((tm, d), lambda i: (jnp.minimum(i, n_main_tiles - 1), 0))]
    x_args = [x_main]
    if two_src:
        x_specs.append(pl.BlockSpec((tm, d), lambda i: (jnp.maximum(i - n_main_tiles, 0), 0),
                                    pipeline_mode=pl.Buffered(1)))
        x_args.append(x_ctx)
    in_specs = x_specs + [
        pl.BlockSpec((tm, d), row_map),
        _resident_layer(w_out_b, layer),
        pl.BlockSpec((1, 6, d), lambda i: (jnp.minimum(i // tiles_per_mod, n_mod - 1), 0, 0)),
        _resident((1, d)),
        _resident(w_router.shape),
        _resident((1, LANES)),
    ]
    return pl.pallas_call(
        functools.partial(_outproj_body, n_main_tiles=n_main_tiles, two_src=two_src, n_experts=n_experts),
        grid=(n_rows // tm,),
        in_specs=in_specs,
        out_specs=[pl.BlockSpec((tm, d), row_map),
                   pl.BlockSpec((tm, d // 2 // LANES, LANES), lambda i: (i, 0, 0)),
                   pl.BlockSpec((tm, LANES), row_map),
                   pl.BlockSpec((8, tm), lambda i: (0, i)),
                   pl.BlockSpec((1, LANES), lambda i: (0, 0))],
        out_shape=[jax.ShapeDtypeStruct((n_rows, d), F32),
                   jax.ShapeDtypeStruct((n_rows, d // 2 // LANES, LANES), U32),
                   jax.ShapeDtypeStruct((n_rows, LANES), F32),
                   jax.ShapeDtypeStruct((8, n_rows), F32),
                   jax.ShapeDtypeStruct((1, LANES), F32)],
        scratch_shapes=[pltpu.VMEM((1, LANES), F32)],
        compiler_params=_params(1),
    )(*x_args, m, w_out_b, mod, norm_w.reshape(1, d), w_router, b_router)


def _row_copy(src_ref, src_row, dst_ref, dst_row, sem):
    return pltpu.make_async_copy(src_ref.at[src_row], dst_ref.at[dst_row], sem)


def _to_token_tiles(rows):
    return rows.reshape(rows.shape[0], rows.shape[1] // LANES, LANES)


def _from_token_tiles(tiles):
    return tiles.reshape(tiles.shape[0], tiles.shape[1] * LANES)


def _slot_rows_body(p1_ref, p2_ref, init_ref, dst_ref, stage_ref, sem, *, tokens_per_step, row_stride):
    i = pl.program_id(0)

    @pl.when(i == 0)
    def _():
        load = pltpu.make_async_copy(init_ref, stage_ref, sem)
        load.start()
        load.wait()

    def put(r, carry):
        t = i * tokens_per_step + r
        stage_ref[p1_ref[t]] = t
        stage_ref[p2_ref[t]] = row_stride + t
        return carry

    lax.fori_loop(0, tokens_per_step, put, 0, unroll=8)

    @pl.when(i == pl.num_programs(0) - 1)
    def _():
        store = pltpu.make_async_copy(stage_ref, dst_ref, sem)
        store.start()
        store.wait()


def _slot_rows_call(pos1, pos2, init_rows, *, row_stride, tokens_per_step):
    n_tok = pos1.shape[0]
    n_map = init_rows.shape[0]
    return pl.pallas_call(
        functools.partial(_slot_rows_body, tokens_per_step=tokens_per_step, row_stride=row_stride),
        grid_spec=pltpu.PrefetchScalarGridSpec(
            num_scalar_prefetch=2,
            grid=(n_tok // tokens_per_step,),
            in_specs=[pl.BlockSpec(memory_space=pl.ANY)],
            out_specs=pl.BlockSpec(memory_space=pl.ANY),
            scratch_shapes=[pltpu.SMEM((n_map,), jnp.int32), pltpu.SemaphoreType.DMA(())],
        ),
        out_shape=jax.ShapeDtypeStruct((n_map,), jnp.int32),
        compiler_params=_params(1),
    )(pos1, pos2, init_rows)


def _ffn_body(first_ref, count_ref, src_ref, dst_ref, h_ref, wg_ref, wu_ref, wd_ref, yk_ref,
              wgb_ref, wub_ref, wdb_ref, xbuf, ybuf, sem_g, sem_s, *, tm, spare_rows):
    e = pl.program_id(0)
    n_experts = pl.num_programs(0)
    half = xbuf.shape[2] * LANES

    def gather_row(tile, r, buf):
        return _row_copy(h_ref, src_ref[(tile + 1) * tm + r], xbuf.at[buf], r, sem_g.at[buf])

    def scatter_row(tile, r, buf):
        return _row_copy(ybuf.at[buf], r, yk_ref, dst_ref[(tile + 1) * tm + r], sem_s.at[buf])

    def wait_tile(buf_ref, sem):
        pltpu.make_async_copy(buf_ref, buf_ref, sem).wait()

    @pl.when(e == 0)
    def _():
        ybuf[...] = jnp.zeros_like(ybuf)

        def prime(r, carry):
            _row_copy(ybuf.at[0], r, yk_ref, spare_rows + r, sem_s.at[0]).start()
            gather_row(0, r, 0).start()
            return carry

        lax.fori_loop(0, tm, prime, 0)

    wgb_ref[...] = wg_ref[0, 0].astype(BF16)
    wub_ref[...] = wu_ref[0, 0].astype(BF16)
    wdb_ref[...] = wd_ref[0, 0].astype(BF16)

    def run_tile(j, carry):
        slot = j % 2
        other = 1 - slot
        wait_tile(xbuf.at[slot], sem_g.at[slot])
        lo, hi = _unpack_bf16_pair(_from_token_tiles(xbuf[slot]))
        lo, hi = lo.astype(BF16), hi.astype(BF16)
        wait_tile(ybuf.at[slot], sem_s.at[slot])
        for r in range(tm):
            gather_row(j + 1, r, other).start()
            scatter_row(j - 1, r, other).start()

        def up(w_ref):
            return (jnp.dot(lo, w_ref[:half, :], preferred_element_type=F32)
                    + jnp.dot(hi, w_ref[half:, :], preferred_element_type=F32))

        g = up(wgb_ref)
        hid = (g * jax.nn.sigmoid(g) * up(wub_ref)).astype(BF16)
        y = jnp.dot(hid, wdb_ref[...], preferred_element_type=F32)
        ybuf[slot] = _to_token_tiles(_pack_bf16_pair(y[:, :half], y[:, half:]))
        return carry

    first = first_ref[e]
    lax.fori_loop(first, first + count_ref[e], run_tile, 0)

    @pl.when(e == n_experts - 1)
    def _():
        j = first + count_ref[e] - 1
        slot = j % 2
        other = 1 - slot

        def flush(r, carry):
            scatter_row(j, r, slot).start()
            return carry

        wait_tile(ybuf.at[other], sem_s.at[other])
        lax.fori_loop(0, tm, flush, 0)
        wait_tile(ybuf.at[slot], sem_s.at[slot])
        wait_tile(xbuf.at[other], sem_g.at[other])


def _ffn_call(first_tile, tile_count, dst_rows, h_packed, w_gate, w_up, w_down, *, layer, tm, row_stride,
              n_out_rows):
    _, sub, _ = h_packed.shape
    _, n_experts, d, de = w_gate.shape
    src_rows = jnp.where(dst_rows < 2 * row_stride, dst_rows % row_stride, 0)

    def w_map(e, ft, tc, sr, dr):
        return (layer, e, 0, 0)

    return pl.pallas_call(
        functools.partial(_ffn_body, tm=tm, spare_rows=2 * row_stride),
        grid_spec=pltpu.PrefetchScalarGridSpec(
            num_scalar_prefetch=4,
            grid=(n_experts,),
            in_specs=[pl.BlockSpec(memory_space=pl.ANY),
                      pl.BlockSpec((1, 1, d, de), w_map),
                      pl.BlockSpec((1, 1, d, de), w_map),
                      pl.BlockSpec((1, 1, de, d), w_map)],
            out_specs=pl.BlockSpec(memory_space=pl.ANY),
            scratch_shapes=[pltpu.VMEM((d, de), BF16), pltpu.VMEM((d, de), BF16), pltpu.VMEM((de, d), BF16),
                            pltpu.VMEM((2, tm, sub, LANES), U32), pltpu.VMEM((2, tm, sub, LANES), U32),
                            pltpu.SemaphoreType.DMA((2,)), pltpu.SemaphoreType.DMA((2,))],
        ),
        out_shape=jax.ShapeDtypeStruct((n_out_rows, sub, LANES), U32),
        compiler_params=_params(1),
    )(first_tile, tile_count, src_rows, dst_rows, h_packed, w_gate, w_up, w_down)


def _combine_body(x_ref, y1_ref, y2_ref, route_ref, mod_ref, fw_ref, o_ref, *, final_norm):
    w1, w2 = route_ref[:, 2:3], route_ref[:, 3:4]
    lo1, hi1 = _unpack_bf16_pair(_from_token_tiles(y1_ref[...]))
    lo2, hi2 = _unpack_bf16_pair(_from_token_tiles(y2_ref[...]))
    y = jnp.concatenate([w1 * lo1 + w2 * lo2, w1 * hi1 + w2 * hi2], axis=-1)
    x_new = x_ref[...] + mod_ref[0, 5:6, :] * y
    if final_norm:
        x_new = x_new * lax.rsqrt(jnp.mean(x_new * x_new, axis=-1, keepdims=True) + RMS_EPS) * fw_ref[...]
    o_ref[...] = x_new


def _combine_call(x, yk, route, mod, final_w, *, rows_per_mod, tm, row_stride, final_norm):
    n_rows, d = x.shape
    sub = yk.shape[1]
    n_mod = mod.shape[0]
    tiles_per_mod = rows_per_mod // tm
    second = row_stride // tm

    def row_map(i):
        return (i, 0)

    return pl.pallas_call(
        functools.partial(_combine_body, final_norm=final_norm),
        grid=(n_rows // tm,),
        in_specs=[pl.BlockSpec((tm, d), row_map),
                  pl.BlockSpec((tm, sub, LANES), lambda i: (i, 0, 0)),
                  pl.BlockSpec((tm, sub, LANES), lambda i: (second + i, 0, 0)),
                  pl.BlockSpec((tm, LANES), row_map),
                  pl.BlockSpec((1, 6, d), lambda i: (jnp.minimum(i // tiles_per_mod, n_mod - 1), 0, 0)),
                  pl.BlockSpec((1, d), lambda i: (0, 0))],
        out_specs=pl.BlockSpec((tm, d), row_map),
        out_shape=jax.ShapeDtypeStruct((n_rows, d), F32),
        compiler_params=_params(1),
    )(x, yk, yk, route, mod, final_w.reshape(1, d))


def _slot_plan(route_t, counts, n_experts, tm):
    n_rows = route_t.shape[1]
    n_tiles = (2 * n_rows) // tm + n_experts
    cnt = counts[0, :n_experts].astype(jnp.int32)
    padded = ((cnt + tm - 1) // tm) * tm
    ends = jnp.cumsum(padded)
    starts = ends - padded
    fields = route_t.astype(jnp.int32)
    picks = jnp.stack([fields[0], fields[1]])[..., None] == jnp.arange(n_experts, dtype=jnp.int32)
    base = jnp.einsum('kte,e->kt', picks.astype(F32), starts.astype(F32),
                      precision=lax.Precision.HIGHEST).astype(jnp.int32)
    pos1 = base[0] + fields[4]
    pos2 = base[1] + fields[5]
    first_tile = (starts // tm).astype(jnp.int32)
    tile_count = (padded // tm).astype(jnp.int32)
    row_stride = n_rows
    spare = 2 * row_stride
    n_map = pl.cdiv((n_tiles + 2) * tm, 1024) * 1024
    init_rows = spare + jnp.arange(n_map, dtype=jnp.int32) % tm
    dst_rows = _slot_rows_call(pos1 + tm, pos2 + tm, init_rows, row_stride=row_stride, tokens_per_step=2 * tm)
    n_out_rows = spare + tm
    return first_tile, tile_count, dst_rows, row_stride, n_out_rows


def _rope_tables(batch, seq, ctx_rows):
    pairs = HEAD_DIM // 4
    t = np.arange(seq)
    pos = np.stack([t // GRID_W, t % GRID_W], axis=-1).astype(np.float32)
    inv_freq = (ROPE_THETA ** (-np.arange(pairs, dtype=np.float32) / pairs)).astype(np.float32)
    ang = pos[:, :, None] * inv_freq
    cos = np.repeat(np.cos(ang)[:, :, None, :], 2, axis=2).reshape(seq, HEAD_DIM)
    sin = np.sin(ang)
    zero = np.zeros_like(sin)
    sa = np.stack([-sin, zero], axis=2).reshape(seq, HEAD_DIM)
    sb = np.stack([zero, sin], axis=2).reshape(seq, HEAD_DIM)

    def full(tab, fill):
        return jnp.asarray(np.concatenate([np.tile(tab, (batch, 1)),
                                           np.full((ctx_rows, HEAD_DIM), fill, np.float32)]), F32)

    return full(cos, 1.0), full(sa, 0.0), full(sb, 0.0)


def kernel(x, c, ctx, c_ctx, w_ada, b_ada, norm_mix_w, norm_ffn_w, w_in, sgu_norm_w, sgu_w_s, sgu_b_s, na_rpb,
           w_merge_gate, b_merge_gate, w_branch_a, w_branch_b, w_out, w_router_group, b_router_group,
           w_router_expert, b_router_expert, w_exp_gate, w_exp_up, w_exp_down, final_norm_w):
    batch, seq, d = x.shape
    ctx_len = ctx.shape[1]
    depth = w_ada.shape[0]
    n_experts = w_exp_gate.shape[1]
    n_lat = batch * seq
    n_ctx = batch * ctx_len
    tm = 256
    tm_wide = 512
    assert seq % tm_wide == 0 and n_ctx % tm_wide == 0 and tm % SGU_CHUNK == 0 and ctx_len % SGU_CHUNK == 0

    mods = _ada_call(jnp.concatenate([c, c_ctx[None]], axis=0), w_ada, b_ada)
    cos, sa, sb = _rope_tables(batch, seq, n_ctx)
    grid_rows = seq // GRID_W

    x_cur = x.reshape(n_lat, d)
    ctx_rows = ctx.reshape(n_ctx, d)
    w_in_b, w_gate_b, w_a_b, w_b_b, w_out_b = (w.astype(BF16) for w in
                                               (w_in, w_merge_gate, w_branch_a, w_branch_b, w_out))
    x_all = None
    for l in range(depth):
        last = l == depth - 1
        mod = mods[l, :batch + 1].reshape(batch + 1, 6, d)
        w_s_b = sgu_w_s[l].astype(BF16)
        b_s_t = sgu_b_s[l].T
        if x_all is None:
            src = (x_cur, ctx_rows)
        else:
            src = (x_all, None)
        h, a, qp, qr, kr, v = _inproj_call(src[0], src[1], mod, norm_mix_w[l], w_in_b, sgu_norm_w[l], w_s_b, b_s_t,
                                           cos, sa, sb, layer=l, rows_per_mod=seq, tm=tm_wide)
        o = _attn_call(qr, qp, kr, v, _attn_bias_by_column(na_rpb[l], grid_rows), batch=batch, seq=seq,
                       ctx_len=ctx_len, with_ctx_queries=not last)
        n_rows = n_lat if last else n_lat + n_ctx
        m = _merge_call(h, a, o, w_gate_b, b_merge_gate[l], w_a_b, w_b_b, layer=l, n_rows=n_rows, tm=tm_wide)
        w_router = jnp.zeros((d, LANES), F32)
        w_router = w_router.at[:, :n_experts].set(w_router_expert[l])
        w_router = w_router.at[:, n_experts:n_experts + N_GROUPS].set(w_router_group[l])
        w_router_hi = w_router.astype(BF16)
        w_router_lo = (w_router - w_router_hi.astype(F32)).astype(BF16)
        w_router = jnp.concatenate([w_router_hi, w_router_lo], axis=1)
        b_router = jnp.zeros((1, LANES), F32)
        b_router = b_router.at[0, :n_experts].set(b_router_expert[l])
        b_router = b_router.at[0, n_experts:n_experts + N_GROUPS].set(b_router_group[l])
        x_mid, h_packed, route, route_t, counts = _outproj_call(
            src[0], src[1], m, w_out_b, mod, norm_ffn_w[l], w_router, b_router, layer=l, rows_per_mod=seq,
            tm=tm_wide, n_experts=n_experts)
        first_tile, tile_count, dst_rows, row_stride, n_out_rows = _slot_plan(route_t, counts, n_experts, tm)
        yk = _ffn_call(first_tile, tile_count, dst_rows, h_packed, w_exp_gate, w_exp_up, w_exp_down, layer=l, tm=tm,
                       row_stride=row_stride, n_out_rows=n_out_rows)
        x_all = _combine_call(x_mid, yk, route, mod, final_norm_w, rows_per_mod=seq, tm=tm,
                              row_stride=row_stride, final_norm=last)
    return x_all.reshape(batch, seq, d)
```

```python
import functools

import numpy as np
import jax
import jax.numpy as jnp
from jax import lax
from jax.experimental import pallas as pl
from jax.experimental.pallas import tpu as pltpu

GRID_W = 64
SGU_CHUNK = 128
SGU_GROUPS = 8
NA_HEADS = 8
HEAD_DIM = 128
WIN_ROWS = 8
WIN_COLS = 16
ROPE_THETA = 10000.0
N_GROUPS = 4
EXPERTS_PER_GROUP = 8
RMS_EPS = 1e-6

LANES = 128
Q_ROWS = 4
Q_BLOCK = Q_ROWS * GRID_W
KEY_ROWS = Q_ROWS + WIN_ROWS - 1
HEADS_PER_STEP = 4
MASKED = -1e30
LOG2_E = 1.4426950408889634
QK_SCALE = HEAD_DIM ** -0.5 * LOG2_E
VMEM_LIMIT = 56 * 1024 * 1024

BF16 = jnp.bfloat16
F32 = jnp.float32
U32 = jnp.uint32


def _params(n_grid_dims, vmem=VMEM_LIMIT):
    return pltpu.CompilerParams(dimension_semantics=("arbitrary",) * n_grid_dims, vmem_limit_bytes=vmem)


def _resident(shape):
    nd = len(shape)
    return pl.BlockSpec(shape, lambda *_: (0,) * nd, pipeline_mode=pl.Buffered(1))


def _resident_layer(stacked, layer):
    nd = stacked.ndim
    return pl.BlockSpec((None,) + stacked.shape[1:], lambda *_: (layer,) + (0,) * (nd - 1),
                        pipeline_mode=pl.Buffered(1))


def _pack_bf16_pair(lo, hi):
    lo_bits = lax.bitcast_convert_type(lo.astype(BF16).astype(F32), U32)
    hi_bits = lax.bitcast_convert_type(hi.astype(BF16).astype(F32), U32)
    return (hi_bits & jnp.uint32(0xFFFF0000)) | (lo_bits >> 16)


def _unpack_bf16_pair(w):
    lo = lax.bitcast_convert_type(w << 16, F32)
    hi = lax.bitcast_convert_type(w & jnp.uint32(0xFFFF0000), F32)
    return lo, hi


def _rms_modulate(x, norm_w, shift, scale):
    y = x * lax.rsqrt(jnp.mean(x * x, axis=-1, keepdims=True) + RMS_EPS) * norm_w
    return y * (1.0 + scale) + shift


def _ada_body(ct_ref, w_ref, b_ref, o_ref, *, n_rows):
    s = ct_ref[...]
    s = s * jax.nn.sigmoid(s)
    w = w_ref[0]
    o_ref[...] = jnp.zeros_like(o_ref)
    for r in range(n_rows):
        o_ref[0, r:r + 1, :] = jnp.sum(w * s[:, r:r + 1], axis=0, keepdims=True) + b_ref[0]


def _ada_call(cond, w_ada, b_ada):
    n_rows, d = cond.shape
    depth, _, n = w_ada.shape
    tn = next(t for t in (1024, 512, 256, LANES) if n % t == 0)
    ct = jnp.zeros((d, 8), F32).at[:, :n_rows].set(cond.T)
    return pl.pallas_call(
        functools.partial(_ada_body, n_rows=n_rows),
        grid=(depth, n // tn),
        in_specs=[pl.BlockSpec((d, 8), lambda l, j: (0, 0)),
                  pl.BlockSpec((1, d, tn), lambda l, j: (l, 0, j)),
                  pl.BlockSpec((1, 1, tn), lambda l, j: (l, 0, j))],
        out_specs=pl.BlockSpec((1, 8, tn), lambda l, j: (l, 0, j)),
        out_shape=jax.ShapeDtypeStruct((depth, 8, n), F32),
        compiler_params=_params(2),
    )(ct, w_ada, b_ada.reshape(depth, 1, n))


def _inproj_body(*refs, n_main_tiles, two_src, sgu_w, na_w):
    if two_src:
        xa_ref, xb_ref = refs[:2]
        refs = refs[2:]
    else:
        xa_ref = refs[0]
        refs = refs[1:]
    (mod_ref, nw_ref, w_ref, snw_ref, ws_ref, bst_ref, cos_ref, sa_ref, sb_ref,
     h_ref, a_ref, qp_ref, qr_ref, kr_ref, v_ref) = refs
    if two_src:
        x = jnp.where(pl.program_id(0) < n_main_tiles, xa_ref[...], xb_ref[...])
    else:
        x = xa_ref[...]
    tm = x.shape[0]
    h = _rms_modulate(x, nw_ref[...], mod_ref[0, 0:1, :], mod_ref[0, 1:2, :])
    hb = h.astype(BF16)
    h_ref[...] = hb

    def proj(lo, width):
        return jnp.dot(hb, w_ref[:, lo:lo + width], preferred_element_type=F32)

    u = jax.nn.gelu(proj(0, sgu_w))
    v = jax.nn.gelu(proj(sgu_w, sgu_w))
    vn = v * lax.rsqrt(jnp.mean(v * v, axis=-1, keepdims=True) + RMS_EPS) * snw_ref[...]
    vnb = vn.astype(BF16)
    gch = sgu_w // SGU_GROUPS
    for c in range(tm // SGU_CHUNK):
        rows = slice(c * SGU_CHUNK, (c + 1) * SGU_CHUNK)
        for g in range(SGU_GROUPS):
            cols = slice(g * gch, (g + 1) * gch)
            z = jnp.dot(ws_ref[g], vnb[rows, cols], preferred_element_type=F32) + bst_ref[:, g:g + 1]
            a_ref[rows, cols] = (u[rows, cols] * z).astype(BF16)

    cos, sa, sb = cos_ref[...], sa_ref[...], sb_ref[...]

    def rope_into(p, out_ref):
        for hh in range(NA_HEADS):
            cols = slice(hh * HEAD_DIM, (hh + 1) * HEAD_DIM)
            xh = p[:, cols]
            out_ref[:, cols] = (xh * cos + pltpu.roll(xh, HEAD_DIM - 32, 1) * sa
                                + pltpu.roll(xh, 32, 1) * sb).astype(BF16)

    q = proj(2 * sgu_w, na_w) * QK_SCALE
    qp_ref[...] = q.astype(BF16)
    rope_into(q, qr_ref)
    rope_into(proj(2 * sgu_w + na_w, na_w), kr_ref)
    v_ref[...] = proj(2 * sgu_w + 2 * na_w, na_w).astype(BF16)


def _inproj_call(x_main, x_ctx, mod, norm_w, w_in_b, sgu_norm_w, w_s_b, b_s_t, cos, sa, sb,
                 *, layer, rows_per_mod, tm):
    d = x_main.shape[1]
    n_main = x_main.shape[0]
    two_src = x_ctx is not None
    tt = n_main + (x_ctx.shape[0] if two_src else 0)
    n_main_tiles = n_main // tm
    n_mod = mod.shape[0]
    sgu_w = sgu_norm_w.shape[-1]
    na_w = NA_HEADS * HEAD_DIM
    tiles_per_mod = rows_per_mod // tm

    def row_map(i):
        return (i, 0)

    x_specs = [pl.BlockSpec((tm, d), lambda i: (jnp.minimum(i, n_main_tiles - 1), 0))]
    x_args = [x_main]
    if two_src:
        x_specs.append(pl.BlockSpec((tm, d), lambda i: (jnp.maximum(i - n_main_tiles, 0), 0),
                                    pipeline_mode=pl.Buffered(1)))
        x_args.append(x_ctx)
    in_specs = x_specs + [
        pl.BlockSpec((1, 6, d), lambda i: (jnp.minimum(i // tiles_per_mod, n_mod - 1), 0, 0)),
        _resident((1, d)),
        _resident_layer(w_in_b, layer),
        _resident((1, sgu_w)),
        _resident(w_s_b.shape),
        _resident(b_s_t.shape),
        pl.BlockSpec((tm, HEAD_DIM), row_map),
        pl.BlockSpec((tm, HEAD_DIM), row_map),
        pl.BlockSpec((tm, HEAD_DIM), row_map),
    ]
    out_widths = [d, sgu_w, na_w, na_w, na_w, na_w]
    return pl.pallas_call(
        functools.partial(_inproj_body, n_main_tiles=n_main_tiles, two_src=two_src, sgu_w=sgu_w, na_w=na_w),
        grid=(tt // tm,),
        in_specs=in_specs,
        out_specs=[pl.BlockSpec((tm, w), row_map) for w in out_widths],
        out_shape=[jax.ShapeDtypeStruct((tt, w), BF16) for w in out_widths],
        compiler_params=_params(1),
    )(*x_args, mod, norm_w.reshape(1, d), w_in_b, sgu_norm_w.reshape(1, sgu_w), w_s_b, b_s_t, cos, sa, sb)


def _softmax_pv(scores, values):
    m = functools.reduce(jnp.maximum, [jnp.max(s, axis=-1, keepdims=True) for s in scores])
    ps = [jnp.exp2(s - m) for s in scores]
    denom = functools.reduce(jnp.add, [jnp.sum(p, axis=-1, keepdims=True) for p in ps])
    acc = functools.reduce(jnp.add, [jnp.dot(p.astype(BF16), v, preferred_element_type=F32)
                                     for p, v in zip(ps, values)])
    return acc / denom


def _qk(q, k):
    return lax.dot_general(q, k, (((1,), (1,)), ((), ())), preferred_element_type=F32)


def _attn_row_structure(blk, grid_rows):
    wr = min(WIN_ROWS, grid_rows)
    k_row0 = int(np.clip(blk * Q_ROWS - WIN_ROWS // 2, 0, grid_rows - KEY_ROWS))
    qr = blk * Q_ROWS + np.arange(Q_ROWS)
    kr = k_row0 + np.arange(KEY_ROWS)
    r0 = np.clip(qr - wr // 2, 0, grid_rows - wr)
    valid = (kr[None, :] >= r0[:, None]) & (kr[None, :] < r0[:, None] + wr)
    ri = np.clip(kr[None, :] - qr[:, None] + (WIN_ROWS - 1), 0, 2 * WIN_ROWS - 2)
    return valid, ri


def _attn_body(qr_ref, qp_ref, k_ref, v_ref, kc_ref, vc_ref, bcol_ref, o_ref, bias_ref, *, n_blocks, grid_rows):
    i = pl.program_id(2)
    n_keys = KEY_ROWS * GRID_W

    heads = [slice(hh * HEAD_DIM, (hh + 1) * HEAD_DIM) for hh in range(HEADS_PER_STEP)]

    def build_bias(blk):
        valid, ri = _attn_row_structure(blk, grid_rows)
        for hh in range(HEADS_PER_STEP):
            for jr in range(Q_ROWS):
                for kl in range(KEY_ROWS):
                    piece = (bcol_ref[hh, int(ri[jr, kl])] if valid[jr, kl]
                             else jnp.full((GRID_W, GRID_W), MASKED, F32))
                    bias_ref[hh, jr * GRID_W:(jr + 1) * GRID_W, kl * GRID_W:(kl + 1) * GRID_W] = piece

    for blk in sorted({0, min(1, n_blocks - 1), n_blocks - 1}):
        pl.when(i == blk)(functools.partial(build_bias, blk))

    @pl.when(i < n_blocks)
    def _():
        k_row0 = jnp.clip(i * Q_ROWS - WIN_ROWS // 2, 0, grid_rows - KEY_ROWS)
        start = pl.multiple_of(k_row0 * GRID_W, GRID_W)
        for hh, cols in enumerate(heads):
            k_loc = k_ref[pl.ds(start, n_keys), cols]
            v_loc = v_ref[pl.ds(start, n_keys), cols]
            s_loc = _qk(qr_ref[:, cols], k_loc) + bias_ref[hh]
            s_ctx = _qk(qp_ref[:, cols], kc_ref[:, cols])
            o_ref[:, cols] = _softmax_pv([s_loc, s_ctx], [v_loc, vc_ref[:, cols]]).astype(BF16)

    @pl.when(i >= n_blocks)
    def _():
        for cols in heads:
            s_ctx = _qk(qp_ref[:, cols], kc_ref[:, cols])
            o_ref[:, cols] = _softmax_pv([s_ctx], [vc_ref[:, cols]]).astype(BF16)


def _attn_bias_by_column(rpb, grid_rows):
    n_blocks = grid_rows // Q_ROWS
    assert min(WIN_ROWS, grid_rows) == WIN_ROWS and grid_rows >= KEY_ROWS and grid_rows % Q_ROWS == 0
    interior = _attn_row_structure(min(1, n_blocks - 1), grid_rows)
    for blk in range(1, n_blocks - 1):
        assert all(np.array_equal(a, b) for a, b in zip(_attn_row_structure(blk, grid_rows), interior))
    n_ci = 2 * WIN_COLS - 1
    qc = np.arange(GRID_W)
    c0 = np.clip(qc - WIN_COLS // 2, 0, GRID_W - WIN_COLS)
    col_valid = (qc[None, :] >= c0[:, None]) & (qc[None, :] < c0[:, None] + WIN_COLS)
    col_sel = (col_valid[:, :, None]
               & ((qc[None, :, None] - qc[:, None, None] + (WIN_COLS - 1)) == np.arange(n_ci))).astype(np.float32)
    by_col = jnp.einsum('hrc,qkc->hrqk', rpb.astype(F32), jnp.asarray(col_sel), precision=lax.Precision.HIGHEST)
    return jnp.where(jnp.asarray(col_valid)[None, None], by_col * LOG2_E, MASKED)


def _attn_call(qr, qp, kr, v, bias_by_col, *, batch, seq, ctx_len, with_ctx_queries):
    assert ctx_len == Q_BLOCK and seq % Q_BLOCK == 0
    grid_rows = seq // GRID_W
    n_blocks = grid_rows // Q_ROWS
    n_steps = n_blocks + (1 if with_ctx_queries else 0)
    ctx_block0 = batch * n_blocks
    n_out = batch * seq + (batch * ctx_len if with_ctx_queries else 0)
    n_keys = KEY_ROWS * GRID_W

    def q_map(b, h, i):
        return (jnp.where(i < n_blocks, b * n_blocks + i, ctx_block0 + b), h)

    width = HEADS_PER_STEP * HEAD_DIM
    return pl.pallas_call(
        functools.partial(_attn_body, n_blocks=n_blocks, grid_rows=grid_rows),
        grid=(batch, NA_HEADS // HEADS_PER_STEP, n_steps),
        in_specs=[pl.BlockSpec((Q_BLOCK, width), q_map),
                  pl.BlockSpec((Q_BLOCK, width), q_map),
                  pl.BlockSpec((seq, width), lambda b, h, i: (b, h)),
                  pl.BlockSpec((seq, width), lambda b, h, i: (b, h)),
                  pl.BlockSpec((ctx_len, width), lambda b, h, i: (ctx_block0 + b, h)),
                  pl.BlockSpec((ctx_len, width), lambda b, h, i: (ctx_block0 + b, h)),
                  pl.BlockSpec((HEADS_PER_STEP,) + bias_by_col.shape[1:], lambda b, h, i: (h, 0, 0, 0))],
        out_specs=pl.BlockSpec((Q_BLOCK, width), q_map),
        out_shape=jax.ShapeDtypeStruct((n_out, NA_HEADS * HEAD_DIM), BF16),
        scratch_shapes=[pltpu.VMEM((HEADS_PER_STEP, Q_BLOCK, n_keys), F32)],
        compiler_params=_params(3),
    )(qr, qp, kr, v, kr, v, bias_by_col)


def _merge_body(h_ref, a_ref, o_ref, wg_ref, bg_ref, wa_ref, wb_ref, m_ref, *, tn):
    d = m_ref.shape[1]
    hb, ab, ob = h_ref[...], a_ref[...], o_ref[...]
    for n0 in range(0, d, tn):
        cols = slice(n0, n0 + tn)
        gcols = slice(d + n0, d + n0 + tn)
        g_a = jax.nn.sigmoid(jnp.dot(hb, wg_ref[:, cols], preferred_element_type=F32) + bg_ref[:, cols])
        g_b = jax.nn.sigmoid(jnp.dot(hb, wg_ref[:, gcols], preferred_element_type=F32) + bg_ref[:, gcols])
        pa = jnp.dot(ab, wa_ref[:, cols], preferred_element_type=F32)
        pb = jnp.dot(ob, wb_ref[:, cols], preferred_element_type=F32)
        m_ref[:, cols] = (g_a * pa + g_b * pb).astype(BF16)


def _merge_call(h, a, o, w_gate_b, b_gate, w_a_b, w_b_b, *, layer, n_rows, tm):
    d = h.shape[1]

    def row_map(i):
        return (i, 0)

    return pl.pallas_call(
        functools.partial(_merge_body, tn=min(d, 512)),
        grid=(n_rows // tm,),
        in_specs=[pl.BlockSpec((tm, d), row_map),
                  pl.BlockSpec((tm, a.shape[1]), row_map),
                  pl.BlockSpec((tm, o.shape[1]), row_map),
                  _resident_layer(w_gate_b, layer), _resident((1, 2 * d)),
                  _resident_layer(w_a_b, layer), _resident_layer(w_b_b, layer)],
        out_specs=pl.BlockSpec((tm, d), row_map),
        out_shape=jax.ShapeDtypeStruct((n_rows, d), BF16),
        compiler_params=_params(1),
    )(h, a, o, w_gate_b, b_gate.reshape(1, 2 * d), w_a_b, w_b_b)


def _outproj_body(*refs, n_main_tiles, two_src, n_experts):
    if two_src:
        xa_ref, xb_ref = refs[:2]
        refs = refs[2:]
    else:
        xa_ref = refs[0]
        refs = refs[1:]
    (m_ref, wo_ref, mod_ref, nw_ref, wr_ref, br_ref,
     xo_ref, hp_ref, route_ref, route_t_ref, cnt_ref, carry_ref) = refs
    i = pl.program_id(0)
    if two_src:
        x = jnp.where(i < n_main_tiles, xa_ref[...], xb_ref[...])
    else:
        x = xa_ref[...]
    tm, d = x.shape

    @pl.when(i == 0)
    def _():
        carry_ref[...] = jnp.zeros_like(carry_ref)

    y = jnp.dot(m_ref[...], wo_ref[...], preferred_element_type=F32)
    x_new = x + mod_ref[0, 2:3, :] * y
    xo_ref[...] = x_new
    h = _rms_modulate(x_new, nw_ref[...], mod_ref[0, 3:4, :], mod_ref[0, 4:5, :])
    hp_ref[...] = _to_token_tiles(_pack_bf16_pair(h[:, :d // 2], h[:, d // 2:]))

    h_hi = h.astype(BF16)
    h_lo = (h - h_hi.astype(F32)).astype(BF16)
    by_hi = jnp.dot(h_hi, wr_ref[...], preferred_element_type=F32)
    logits = (by_hi[:, :LANES] + by_hi[:, LANES:]
              + jnp.dot(h_lo, wr_ref[:, :LANES], preferred_element_type=F32) + br_ref[...])
    lane = lax.broadcasted_iota(jnp.int32, logits.shape, 1)
    lane_f = lane.astype(F32)
    far = jnp.float32(4 * LANES)

    def first_argmax(vals):
        top = jnp.max(vals, axis=-1, keepdims=True)
        return top, jnp.min(jnp.where(vals == top, lane_f, far), axis=-1, keepdims=True)

    g_logits = jnp.where((lane >= n_experts) & (lane < n_experts + N_GROUPS), logits, MASKED)
    g_top, g_lane = first_argmax(g_logits)
    g_prob = 1.0 / jnp.sum(jnp.exp(g_logits - g_top), axis=-1, keepdims=True)
    e_lo = (g_lane - n_experts) * EXPERTS_PER_GROUP
    e_logits = jnp.where((lane_f >= e_lo) & (lane_f < e_lo + EXPERTS_PER_GROUP), logits, MASKED)
    top1, e1 = first_argmax(e_logits)
    top2, e2 = first_argmax(jnp.where(lane_f == e1, MASKED, e_logits))
    t = jnp.exp(top2 - top1)
    w1 = g_prob / (1.0 + t)
    w2 = g_prob * t / (1.0 + t)

    sel1, sel2 = lane_f == e1, lane_f == e2
    onehot = jnp.where(sel1 | sel2, 1.0, 0.0)
    r_i = lax.broadcasted_iota(jnp.int32, (tm, tm), 0)
    c_i = lax.broadcasted_iota(jnp.int32, (tm, tm), 1)
    earlier = jnp.where(c_i < r_i, 1.0, 0.0).astype(BF16)
    before = jnp.dot(earlier, onehot.astype(BF16), preferred_element_type=F32) + carry_ref[...]
    rank1 = jnp.sum(jnp.where(sel1, before, 0.0), axis=-1, keepdims=True)
    rank2 = jnp.sum(jnp.where(sel2, before, 0.0), axis=-1, keepdims=True)
    carry_ref[...] += jnp.sum(onehot, axis=0, keepdims=True)
    cnt_ref[...] = carry_ref[...]

    route = jnp.zeros_like(logits)
    for k, val in enumerate((e1, e2, w1, w2, rank1, rank2)):
        route = jnp.where(lane == k, val, route)
    route_ref[...] = route
    route_t_ref[...] = route.T[:8, :]


def _outproj_call(x_main, x_ctx, m, w_out_b, mod, norm_w, w_router, b_router, *, layer, rows_per_mod, tm,
                  n_experts):
    d = x_main.shape[1]
    two_src = x_ctx is not None
    n_main = x_main.shape[0] if two_src else m.shape[0]
    n_rows = m.shape[0]
    n_main_tiles = n_main // tm
    n_mod = mod.shape[0]
    tiles_per_mod = rows_per_mod // tm

    def row_map(i):
        return (i, 0)

    x_specs = [pl.BlockSpec((tm, d), lambda i: (jnp.minimum(i, n_main_tiles - 1), 0))]
    x_args = [x_main]
    if two_src:
        x_specs.append(pl.BlockSpec((tm, d), lambda i: (jnp.maximum(i - n_main_tiles, 0), 0),
                                    pipeline_mode=pl.Buffered(1)))
        x_args.append(x_ctx)
    in_specs = x_specs + [
        pl.BlockSpec((tm, d), row_map),
        _resident_layer(w_out_b, layer),
        pl.BlockSpec((1, 6, d), lambda i: (jnp.minimum(i // tiles_per_mod, n_mod - 1), 0, 0)),
        _resident((1, d)),
        _resident(w_router.shape),
        _resident((1, LANES)),
    ]
    return pl.pallas_call(
        functools.partial(_outproj_body, n_main_tiles=n_main_tiles, two_src=two_src, n_experts=n_experts),
        grid=(n_rows // tm,),
        in_specs=in_specs,
        out_specs=[pl.BlockSpec((tm, d), row_map),
                   pl.BlockSpec((tm, d // 2 // LANES, LANES), lambda i: (i, 0, 0)),
                   pl.BlockSpec((tm, LANES), row_map),
                   pl.BlockSpec((8, tm), lambda i: (0, i)),
                   pl.BlockSpec((1, LANES), lambda i: (0, 0))],
        out_shape=[jax.ShapeDtypeStruct((n_rows, d), F32),
                   jax.ShapeDtypeStruct((n_rows, d // 2 // LANES, LANES), U32),
                   jax.ShapeDtypeStruct((n_rows, LANES), F32),
                   jax.ShapeDtypeStruct((8, n_rows), F32),
                   jax.ShapeDtypeStruct((1, LANES), F32)],
        scratch_shapes=[pltpu.VMEM((1, LANES), F32)],
        compiler_params=_params(1),
    )(*x_args, m, w_out_b, mod, norm_w.reshape(1, d), w_router, b_router)


def _row_copy(src_ref, src_row, dst_ref, dst_row, sem):
    return pltpu.make_async_copy(src_ref.at[src_row], dst_ref.at[dst_row], sem)


def _to_token_tiles(rows):
    return rows.reshape(rows.shape[0], rows.shape[1] // LANES, LANES)


def _from_token_tiles(tiles):
    return tiles.reshape(tiles.shape[0], tiles.shape[1] * LANES)


def _slot_rows_body(p1_ref, p2_ref, init_ref, dst_ref, stage_ref, sem, *, tokens_per_step):
    i = pl.program_id(0)

    @pl.when(i == 0)
    def _():
        load = pltpu.make_async_copy(init_ref, stage_ref, sem)
        load.start()
        load.wait()

    def put(r, carry):
        t = i * tokens_per_step + r
        stage_ref[p1_ref[t]] = t
        stage_ref[p2_ref[t]] = t
        return carry

    lax.fori_loop(0, tokens_per_step, put, 0, unroll=8)

    @pl.when(i == pl.num_programs(0) - 1)
    def _():
        store = pltpu.make_async_copy(stage_ref, dst_ref, sem)
        store.start()
        store.wait()


def _slot_rows_call(pos1, pos2, init_rows, *, tokens_per_step):
    n_tok = pos1.shape[0]
    n_map = init_rows.shape[0]
    return pl.pallas_call(
        functools.partial(_slot_rows_body, tokens_per_step=tokens_per_step),
        grid_spec=pltpu.PrefetchScalarGridSpec(
            num_scalar_prefetch=2,
            grid=(n_tok // tokens_per_step,),
            in_specs=[pl.BlockSpec(memory_space=pl.ANY)],
            out_specs=pl.BlockSpec(memory_space=pl.ANY),
            scratch_shapes=[pltpu.SMEM((n_map,), jnp.int32), pltpu.SemaphoreType.DMA(())],
        ),
        out_shape=jax.ShapeDtypeStruct((n_map,), jnp.int32),
        compiler_params=_params(1),
    )(pos1, pos2, init_rows)


def _ffn_body(first_ref, count_ref, src_ref, h_ref, wg_ref, wu_ref, wd_ref, ys_ref,
              wgb_ref, wub_ref, wdb_ref, xbuf, ybuf, sem_g, sem_s, *, tm, n_tiles):
    e = pl.program_id(0)
    n_experts = pl.num_programs(0)
    half = xbuf.shape[2] * LANES

    def gather_row(tile, r, buf):
        return _row_copy(h_ref, src_ref[tile * tm + r], xbuf.at[buf], r, sem_g.at[buf])

    def store_tile(tile, buf):
        return pltpu.make_async_copy(ybuf.at[buf], ys_ref.at[pl.ds(tile * tm, tm)], sem_s.at[buf])

    def wait_tile(buf_ref, sem):
        pltpu.make_async_copy(buf_ref, buf_ref, sem).wait()

    @pl.when(e == 0)
    def _():
        ybuf[...] = jnp.zeros_like(ybuf)
        store_tile(n_tiles, 0).start()

        def prime(r, carry):
            gather_row(0, r, 0).start()
            return carry

        lax.fori_loop(0, tm, prime, 0)

    wgb_ref[...] = wg_ref[0, 0].astype(BF16)
    wub_ref[...] = wu_ref[0, 0].astype(BF16)
    wdb_ref[...] = wd_ref[0, 0].astype(BF16)

    def run_tile(j, carry):
        slot = j % 2
        other = 1 - slot
        wait_tile(xbuf.at[slot], sem_g.at[slot])
        lo, hi = _unpack_bf16_pair(_from_token_tiles(xbuf[slot]))
        lo, hi = lo.astype(BF16), hi.astype(BF16)
        wait_tile(ybuf.at[slot], sem_s.at[slot])
        store_tile(jnp.where(j == 0, n_tiles + 1, j - 1), other).start()
        for r in range(tm):
            gather_row(j + 1, r, other).start()

        def up(w_ref):
            return (jnp.dot(lo, w_ref[:half, :], preferred_element_type=F32)
                    + jnp.dot(hi, w_ref[half:, :], preferred_element_type=F32))

        g = up(wgb_ref)
        hid = (g * jax.nn.sigmoid(g) * up(wub_ref)).astype(BF16)
        y = jnp.dot(hid, wdb_ref[...], preferred_element_type=F32)
        ybuf[slot] = _to_token_tiles(_pack_bf16_pair(y[:, :half], y[:, half:]))
        return carry

    first = first_ref[e]
    lax.fori_loop(first, first + count_ref[e], run_tile, 0)

    @pl.when(e == n_experts - 1)
    def _():
        n_used = first + count_ref[e]
        slot = (n_used - 1) % 2
        other = 1 - slot
        wait_tile(ybuf.at[other], sem_s.at[other])
        store_tile(n_used - 1, slot).start()
        ybuf[other] = jnp.zeros_like(ybuf[other])

        def clear(k, carry):
            store_tile(k, other).start()
            wait_tile(ybuf.at[other], sem_s.at[other])
            return carry

        lax.fori_loop(n_used, n_tiles, clear, 0)
        wait_tile(ybuf.at[slot], sem_s.at[slot])
        wait_tile(xbuf.at[other], sem_g.at[other])


def _ffn_call(first_tile, tile_count, src_rows, h_packed, w_gate, w_up, w_down, *, layer, tm, n_tiles):
    _, sub, _ = h_packed.shape
    _, n_experts, d, de = w_gate.shape

    def w_map(e, ft, tc, sr):
        return (layer, e, 0, 0)

    return pl.pallas_call(
        functools.partial(_ffn_body, tm=tm, n_tiles=n_tiles),
        grid_spec=pltpu.PrefetchScalarGridSpec(
            num_scalar_prefetch=3,
            grid=(n_experts,),
            in_specs=[pl.BlockSpec(memory_space=pl.ANY),
                      pl.BlockSpec((1, 1, d, de), w_map),
                      pl.BlockSpec((1, 1, d, de), w_map),
                      pl.BlockSpec((1, 1, de, d), w_map)],
            out_specs=pl.BlockSpec(memory_space=pl.ANY),
            scratch_shapes=[pltpu.VMEM((d, de), BF16), pltpu.VMEM((d, de), BF16), pltpu.VMEM((de, d), BF16),
                            pltpu.VMEM((2, tm, sub, LANES), U32), pltpu.VMEM((2, tm, sub, LANES), U32),
                            pltpu.SemaphoreType.DMA((2,)), pltpu.SemaphoreType.DMA((2,))],
        ),
        out_shape=jax.ShapeDtypeStruct(((n_tiles + 2) * tm, sub, LANES), U32),
        compiler_params=_params(1),
    )(first_tile, tile_count, src_rows, h_packed, w_gate, w_up, w_down)


def _combine_body(p1_ref, p2_ref, x_ref, route_ref, mod_ref, fw_ref, ys_ref, o_ref, y1buf, y2buf, sem,
                  *, tm, final_norm):
    i = pl.program_id(0)
    slot = i % 2
    other = 1 - slot

    def gather_pair(tile, r, buf):
        t = tile * tm + r
        _row_copy(ys_ref, p1_ref[t], y1buf.at[buf], r, sem.at[buf]).start()
        _row_copy(ys_ref, p2_ref[t], y2buf.at[buf], r, sem.at[buf]).start()

    def wait_pair(buf):
        pltpu.make_async_copy(y1buf.at[buf], y1buf.at[buf], sem.at[buf]).wait()
        pltpu.make_async_copy(y2buf.at[buf], y2buf.at[buf], sem.at[buf]).wait()

    @pl.when(i == 0)
    def _():
        def prime(r, carry):
            gather_pair(0, r, 0)
            return carry

        lax.fori_loop(0, tm, prime, 0)

    wait_pair(slot)
    lo1, hi1 = _unpack_bf16_pair(_from_token_tiles(y1buf[slot]))
    lo2, hi2 = _unpack_bf16_pair(_from_token_tiles(y2buf[slot]))
    for r in range(tm):
        gather_pair(i + 1, r, other)
    w1, w2 = route_ref[:, 2:3], route_ref[:, 3:4]
    y = jnp.concatenate([w1 * lo1 + w2 * lo2, w1 * hi1 + w2 * hi2], axis=-1)
    x_new = x_ref[...] + mod_ref[0, 5:6, :] * y
    if final_norm:
        x_new = x_new * lax.rsqrt(jnp.mean(x_new * x_new, axis=-1, keepdims=True) + RMS_EPS) * fw_ref[...]
    o_ref[...] = x_new

    @pl.when(i == pl.num_programs(0) - 1)
    def _():
        wait_pair(other)


def _combine_call(pos1, pos2, x, ys, route, mod, final_w, *, rows_per_mod, tm, final_norm):
    n_rows, d = x.shape
    sub = ys.shape[1]
    n_mod = mod.shape[0]
    tiles_per_mod = rows_per_mod // tm
    pad = jnp.zeros((tm,), jnp.int32)
    return pl.pallas_call(
        functools.partial(_combine_body, tm=tm, final_norm=final_norm),
        grid_spec=pltpu.PrefetchScalarGridSpec(
            num_scalar_prefetch=2,
            grid=(n_rows // tm,),
            in_specs=[pl.BlockSpec((tm, d), lambda i, p1, p2: (i, 0)),
                      pl.BlockSpec((tm, LANES), lambda i, p1, p2: (i, 0)),
                      pl.BlockSpec((1, 6, d), lambda i, p1, p2: (jnp.minimum(i // tiles_per_mod, n_mod - 1), 0, 0)),
                      pl.BlockSpec((1, d), lambda i, p1, p2: (0, 0)),
                      pl.BlockSpec(memory_space=pl.ANY)],
            out_specs=pl.BlockSpec((tm, d), lambda i, p1, p2: (i, 0)),
            scratch_shapes=[pltpu.VMEM((2, tm, sub, LANES), U32), pltpu.VMEM((2, tm, sub, LANES), U32),
                            pltpu.SemaphoreType.DMA((2,))],
        ),
        out_shape=jax.ShapeDtypeStruct((n_rows, d), F32),
        compiler_params=_params(1),
    )(jnp.concatenate([pos1, pad]), jnp.concatenate([pos2, pad]), x, route, mod, final_w.reshape(1, d), ys)


def _slot_plan(route_t, counts, n_experts, tm):
    n_rows = route_t.shape[1]
    n_tiles = (2 * n_rows) // tm + n_experts
    cnt = counts[0, :n_experts].astype(jnp.int32)
    padded = ((cnt + tm - 1) // tm) * tm
    ends = jnp.cumsum(padded)
    starts = ends - padded
    fields = route_t.astype(jnp.int32)
    picks = jnp.stack([fields[0], fields[1]])[..., None] == jnp.arange(n_experts, dtype=jnp.int32)
    base = jnp.einsum('kte,e->kt', picks.astype(F32), starts.astype(F32),
                      precision=lax.Precision.HIGHEST).astype(jnp.int32)
    pos1 = base[0] + fields[4]
    pos2 = base[1] + fields[5]
    first_tile = (starts // tm).astype(jnp.int32)
    tile_count = (padded // tm).astype(jnp.int32)
    n_map = pl.cdiv((n_tiles + 1) * tm, 1024) * 1024
    src_rows = _slot_rows_call(pos1, pos2, jnp.zeros((n_map,), jnp.int32), tokens_per_step=2 * tm)
    return first_tile, tile_count, src_rows, pos1, pos2, n_tiles


def _rope_tables(batch, seq, ctx_rows):
    pairs = HEAD_DIM // 4
    t = np.arange(seq)
    pos = np.stack([t // GRID_W, t % GRID_W], axis=-1).astype(np.float32)
    inv_freq = (ROPE_THETA ** (-np.arange(pairs, dtype=np.float32) / pairs)).astype(np.float32)
    ang = pos[:, :, None] * inv_freq
    cos = np.repeat(np.cos(ang)[:, :, None, :], 2, axis=2).reshape(seq, HEAD_DIM)
    sin = np.sin(ang)
    zero = np.zeros_like(sin)
    sa = np.stack([-sin, zero], axis=2).reshape(seq, HEAD_DIM)
    sb = np.stack([zero, sin], axis=2).reshape(seq, HEAD_DIM)

    def full(tab, fill):
        return jnp.asarray(np.concatenate([np.tile(tab, (batch, 1)),
                                           np.full((ctx_rows, HEAD_DIM), fill, np.float32)]), F32)

    return full(cos, 1.0), full(sa, 0.0), full(sb, 0.0)


def kernel(x, c, ctx, c_ctx, w_ada, b_ada, norm_mix_w, norm_ffn_w, w_in, sgu_norm_w, sgu_w_s, sgu_b_s, na_rpb,
           w_merge_gate, b_merge_gate, w_branch_a, w_branch_b, w_out, w_router_group, b_router_group,
           w_router_expert, b_router_expert, w_exp_gate, w_exp_up, w_exp_down, final_norm_w):
    batch, seq, d = x.shape
    ctx_len = ctx.shape[1]
    depth = w_ada.shape[0]
    n_experts = w_exp_gate.shape[1]
    n_lat = batch * seq
    n_ctx = batch * ctx_len
    tm = 256
    tm_wide = 512
    assert seq % tm_wide == 0 and n_ctx % tm_wide == 0 and tm % SGU_CHUNK == 0 and ctx_len % SGU_CHUNK == 0

    mods = _ada_call(jnp.concatenate([c, c_ctx[None]], axis=0), w_ada, b_ada)
    cos, sa, sb = _rope_tables(batch, seq, n_ctx)
    grid_rows = seq // GRID_W

    x_cur = x.reshape(n_lat, d)
    ctx_rows = ctx.reshape(n_ctx, d)
    w_in_b, w_gate_b, w_a_b, w_b_b, w_out_b = (w.astype(BF16) for w in
                                               (w_in, w_merge_gate, w_branch_a, w_branch_b, w_out))
    x_all = None
    for l in range(depth):
        last = l == depth - 1
        mod = mods[l, :batch + 1].reshape(batch + 1, 6, d)
        w_s_b = sgu_w_s[l].astype(BF16)
        b_s_t = sgu_b_s[l].T
        if x_all is None:
            src = (x_cur, ctx_rows)
        else:
            src = (x_all, None)
        h, a, qp, qr, kr, v = _inproj_call(src[0], src[1], mod, norm_mix_w[l], w_in_b, sgu_norm_w[l], w_s_b, b_s_t,
                                           cos, sa, sb, layer=l, rows_per_mod=seq, tm=tm_wide)
        o = _attn_call(qr, qp, kr, v, _attn_bias_by_column(na_rpb[l], grid_rows), batch=batch, seq=seq,
                       ctx_len=ctx_len, with_ctx_queries=not last)
        n_rows = n_lat if last else n_lat + n_ctx
        m = _merge_call(h, a, o, w_gate_b, b_merge_gate[l], w_a_b, w_b_b, layer=l, n_rows=n_rows, tm=tm_wide)
        w_router = jnp.zeros((d, LANES), F32)
        w_router = w_router.at[:, :n_experts].set(w_router_expert[l])
        w_router = w_router.at[:, n_experts:n_experts + N_GROUPS].set(w_router_group[l])
        w_router_hi = w_router.astype(BF16)
        w_router_lo = (w_router - w_router_hi.astype(F32)).astype(BF16)
        w_router = jnp.concatenate([w_router_hi, w_router_lo], axis=1)
        b_router = jnp.zeros((1, LANES), F32)
        b_router = b_router.at[0, :n_experts].set(b_router_expert[l])
        b_router = b_router.at[0, n_experts:n_experts + N_GROUPS].set(b_router_group[l])
        x_mid, h_packed, route, route_t, counts = _outproj_call(
            src[0], src[1], m, w_out_b, mod, norm_ffn_w[l], w_router, b_router, layer=l, rows_per_mod=seq,
            tm=tm_wide, n_experts=n_experts)
        first_tile, tile_count, src_rows, pos1, pos2, n_tiles = _slot_plan(route_t, counts, n_experts, tm)
        ys = _ffn_call(first_tile, tile_count, src_rows, h_packed, w_exp_gate, w_exp_up, w_exp_down, layer=l, tm=tm,
                       n_tiles=n_tiles)
        x_all = _combine_call(pos1, pos2, x_mid, ys, route, mod, final_norm_w, rows_per_mod=seq, tm=tm,
                              final_norm=last)
    return x_all.reshape(batch, seq, d)
```

```python
import functools

import numpy as np
import jax
import jax.numpy as jnp
from jax import lax
from jax.experimental import pallas as pl
from jax.experimental.pallas import tpu as pltpu

GRID_W = 64
SGU_CHUNK = 128
SGU_GROUPS = 8
NA_HEADS = 8
HEAD_DIM = 128
WIN_ROWS = 8
WIN_COLS = 16
ROPE_THETA = 10000.0
N_GROUPS = 4
EXPERTS_PER_GROUP = 8
RMS_EPS = 1e-6

LANES = 128
Q_ROWS = 4
Q_BLOCK = Q_ROWS * GRID_W
KEY_ROWS = Q_ROWS + WIN_ROWS - 1
HEADS_PER_STEP = 4
GATHER_DEPTH = 3
MASKED = -1e30
LOG2_E = 1.4426950408889634
QK_SCALE = HEAD_DIM ** -0.5 * LOG2_E
VMEM_LIMIT = 56 * 1024 * 1024

BF16 = jnp.bfloat16
F32 = jnp.float32
U32 = jnp.uint32


def _params(n_grid_dims, vmem=VMEM_LIMIT):
    return pltpu.CompilerParams(dimension_semantics=("arbitrary",) * n_grid_dims, vmem_limit_bytes=vmem)


def _resident(shape):
    nd = len(shape)
    return pl.BlockSpec(shape, lambda *_: (0,) * nd, pipeline_mode=pl.Buffered(1))


def _resident_layer(stacked, layer):
    nd = stacked.ndim
    return pl.BlockSpec((None,) + stacked.shape[1:], lambda *_: (layer,) + (0,) * (nd - 1),
                        pipeline_mode=pl.Buffered(1))


def _pack_bf16_pair(lo, hi):
    lo_bits = lax.bitcast_convert_type(lo.astype(BF16).astype(F32), U32)
    hi_bits = lax.bitcast_convert_type(hi.astype(BF16).astype(F32), U32)
    return (hi_bits & jnp.uint32(0xFFFF0000)) | (lo_bits >> 16)


def _unpack_bf16_pair(w):
    lo = lax.bitcast_convert_type(w << 16, F32)
    hi = lax.bitcast_convert_type(w & jnp.uint32(0xFFFF0000), F32)
    return lo, hi


def _rms_modulate(x, norm_w, shift, scale):
    y = x * lax.rsqrt(jnp.mean(x * x, axis=-1, keepdims=True) + RMS_EPS) * norm_w
    return y * (1.0 + scale) + shift


def _ada_body(ct_ref, w_ref, b_ref, o_ref, *, n_rows):
    s = ct_ref[...]
    s = s * jax.nn.sigmoid(s)
    w = w_ref[0]
    o_ref[...] = jnp.zeros_like(o_ref)
    for r in range(n_rows):
        o_ref[0, r:r + 1, :] = jnp.sum(w * s[:, r:r + 1], axis=0, keepdims=True) + b_ref[0]


def _ada_call(cond, w_ada, b_ada):
    n_rows, d = cond.shape
    depth, _, n = w_ada.shape
    tn = next(t for t in (1024, 512, 256, LANES) if n % t == 0)
    ct = jnp.zeros((d, 8), F32).at[:, :n_rows].set(cond.T)
    return pl.pallas_call(
        functools.partial(_ada_body, n_rows=n_rows),
        grid=(depth, n // tn),
        in_specs=[pl.BlockSpec((d, 8), lambda l, j: (0, 0)),
                  pl.BlockSpec((1, d, tn), lambda l, j: (l, 0, j)),
                  pl.BlockSpec((1, 1, tn), lambda l, j: (l, 0, j))],
        out_specs=pl.BlockSpec((1, 8, tn), lambda l, j: (l, 0, j)),
        out_shape=jax.ShapeDtypeStruct((depth, 8, n), F32),
        compiler_params=_params(2),
    )(ct, w_ada, b_ada.reshape(depth, 1, n))


def _inproj_body(*refs, n_main_tiles, two_src, sgu_w, na_w):
    if two_src:
        xa_ref, xb_ref = refs[:2]
        refs = refs[2:]
    else:
        xa_ref = refs[0]
        refs = refs[1:]
    (mod_ref, nw_ref, w_ref, snw_ref, ws_ref, bst_ref, cos_ref, sa_ref, sb_ref,
     h_ref, a_ref, qp_ref, qr_ref, kr_ref, v_ref) = refs
    if two_src:
        x = jnp.where(pl.program_id(0) < n_main_tiles, xa_ref[...], xb_ref[...])
    else:
        x = xa_ref[...]
    tm = x.shape[0]
    h = _rms_modulate(x, nw_ref[...], mod_ref[0, 0:1, :], mod_ref[0, 1:2, :])
    hb = h.astype(BF16)
    h_ref[...] = hb

    def proj(lo, width):
        return jnp.dot(hb, w_ref[:, lo:lo + width], preferred_element_type=F32)

    u = jax.nn.gelu(proj(0, sgu_w))
    v = jax.nn.gelu(proj(sgu_w, sgu_w))
    vn = v * lax.rsqrt(jnp.mean(v * v, axis=-1, keepdims=True) + RMS_EPS) * snw_ref[...]
    vnb = vn.astype(BF16)
    gch = sgu_w // SGU_GROUPS
    for c in range(tm // SGU_CHUNK):
        rows = slice(c * SGU_CHUNK, (c + 1) * SGU_CHUNK)
        for g in range(SGU_GROUPS):
            cols = slice(g * gch, (g + 1) * gch)
            z = jnp.dot(ws_ref[g], vnb[rows, cols], preferred_element_type=F32) + bst_ref[:, g:g + 1]
            a_ref[rows, cols] = (u[rows, cols] * z).astype(BF16)

    cos, sa, sb = cos_ref[...], sa_ref[...], sb_ref[...]

    def rope_into(p, out_ref):
        for hh in range(NA_HEADS):
            cols = slice(hh * HEAD_DIM, (hh + 1) * HEAD_DIM)
            xh = p[:, cols]
            out_ref[:, cols] = (xh * cos + pltpu.roll(xh, HEAD_DIM - 32, 1) * sa
                                + pltpu.roll(xh, 32, 1) * sb).astype(BF16)

    q = proj(2 * sgu_w, na_w) * QK_SCALE
    qp_ref[...] = q.astype(BF16)
    rope_into(q, qr_ref)
    rope_into(proj(2 * sgu_w + na_w, na_w), kr_ref)
    v_ref[...] = proj(2 * sgu_w + 2 * na_w, na_w).astype(BF16)


def _inproj_call(x_main, x_ctx, mod, norm_w, w_in_b, sgu_norm_w, w_s_b, b_s_t, cos, sa, sb,
                 *, layer, rows_per_mod, tm):
    d = x_main.shape[1]
    n_main = x_main.shape[0]
    two_src = x_ctx is not None
    tt = n_main + (x_ctx.shape[0] if two_src else 0)
    n_main_tiles = n_main // tm
    n_mod = mod.shape[0]
    sgu_w = sgu_norm_w.shape[-1]
    na_w = NA_HEADS * HEAD_DIM
    tiles_per_mod = rows_per_mod // tm

    def row_map(i):
        return (i, 0)

    x_specs = [pl.BlockSpec((tm, d), lambda i: (jnp.minimum(i, n_main_tiles - 1), 0))]
    x_args = [x_main]
    if two_src:
        x_specs.append(pl.BlockSpec((tm, d), lambda i: (jnp.maximum(i - n_main_tiles, 0), 0),
                                    pipeline_mode=pl.Buffered(1)))
        x_args.append(x_ctx)
    in_specs = x_specs + [
        pl.BlockSpec((1, 6, d), lambda i: (jnp.minimum(i // tiles_per_mod, n_mod - 1), 0, 0)),
        _resident((1, d)),
        _resident_layer(w_in_b, layer),
        _resident((1, sgu_w)),
        _resident(w_s_b.shape),
        _resident(b_s_t.shape),
        pl.BlockSpec((tm, HEAD_DIM), row_map),
        pl.BlockSpec((tm, HEAD_DIM), row_map),
        pl.BlockSpec((tm, HEAD_DIM), row_map),
    ]
    out_widths = [d, sgu_w, na_w, na_w, na_w, na_w]
    return pl.pallas_call(
        functools.partial(_inproj_body, n_main_tiles=n_main_tiles, two_src=two_src, sgu_w=sgu_w, na_w=na_w),
        grid=(tt // tm,),
        in_specs=in_specs,
        out_specs=[pl.BlockSpec((tm, w), row_map) for w in out_widths],
        out_shape=[jax.ShapeDtypeStruct((tt, w), BF16) for w in out_widths],
        compiler_params=_params(1),
    )(*x_args, mod, norm_w.reshape(1, d), w_in_b, sgu_norm_w.reshape(1, sgu_w), w_s_b, b_s_t, cos, sa, sb)


def _softmax_pv(scores, values):
    m = functools.reduce(jnp.maximum, [jnp.max(s, axis=-1, keepdims=True) for s in scores])
    ps = [jnp.exp2(s - m) for s in scores]
    denom = functools.reduce(jnp.add, [jnp.sum(p, axis=-1, keepdims=True) for p in ps])
    acc = functools.reduce(jnp.add, [jnp.dot(p.astype(BF16), v, preferred_element_type=F32)
                                     for p, v in zip(ps, values)])
    return acc / denom


def _qk(q, k):
    return lax.dot_general(q, k, (((1,), (1,)), ((), ())), preferred_element_type=F32)


def _attn_row_structure(blk, grid_rows):
    wr = min(WIN_ROWS, grid_rows)
    k_row0 = int(np.clip(blk * Q_ROWS - WIN_ROWS // 2, 0, grid_rows - KEY_ROWS))
    qr = blk * Q_ROWS + np.arange(Q_ROWS)
    kr = k_row0 + np.arange(KEY_ROWS)
    r0 = np.clip(qr - wr // 2, 0, grid_rows - wr)
    valid = (kr[None, :] >= r0[:, None]) & (kr[None, :] < r0[:, None] + wr)
    ri = np.clip(kr[None, :] - qr[:, None] + (WIN_ROWS - 1), 0, 2 * WIN_ROWS - 2)
    return valid, ri


def _attn_body(qr_ref, qp_ref, k_ref, v_ref, kc_ref, vc_ref, bcol_ref, o_ref, bias_ref, *, n_blocks, grid_rows):
    i = pl.program_id(2)
    n_keys = KEY_ROWS * GRID_W

    heads = [slice(hh * HEAD_DIM, (hh + 1) * HEAD_DIM) for hh in range(HEADS_PER_STEP)]

    def build_bias(blk):
        valid, ri = _attn_row_structure(blk, grid_rows)
        for hh in range(HEADS_PER_STEP):
            for jr in range(Q_ROWS):
                for kl in range(KEY_ROWS):
                    piece = (bcol_ref[hh, int(ri[jr, kl])] if valid[jr, kl]
                             else jnp.full((GRID_W, GRID_W), MASKED, F32))
                    bias_ref[hh, jr * GRID_W:(jr + 1) * GRID_W, kl * GRID_W:(kl + 1) * GRID_W] = piece

    for blk in sorted({0, min(1, n_blocks - 1), n_blocks - 1}):
        pl.when(i == blk)(functools.partial(build_bias, blk))

    @pl.when(i < n_blocks)
    def _():
        k_row0 = jnp.clip(i * Q_ROWS - WIN_ROWS // 2, 0, grid_rows - KEY_ROWS)
        start = pl.multiple_of(k_row0 * GRID_W, GRID_W)
        for hh, cols in enumerate(heads):
            k_loc = k_ref[pl.ds(start, n_keys), cols]
            v_loc = v_ref[pl.ds(start, n_keys), cols]
            s_loc = _qk(qr_ref[:, cols], k_loc) + bias_ref[hh]
            s_ctx = _qk(qp_ref[:, cols], kc_ref[:, cols])
            o_ref[:, cols] = _softmax_pv([s_loc, s_ctx], [v_loc, vc_ref[:, cols]]).astype(BF16)

    @pl.when(i >= n_blocks)
    def _():
        for cols in heads:
            s_ctx = _qk(qp_ref[:, cols], kc_ref[:, cols])
            o_ref[:, cols] = _softmax_pv([s_ctx], [vc_ref[:, cols]]).astype(BF16)


def _attn_bias_by_column(rpb, grid_rows):
    n_blocks = grid_rows // Q_ROWS
    assert min(WIN_ROWS, grid_rows) == WIN_ROWS and grid_rows >= KEY_ROWS and grid_rows % Q_ROWS == 0
    interior = _attn_row_structure(min(1, n_blocks - 1), grid_rows)
    for blk in range(1, n_blocks - 1):
        assert all(np.array_equal(a, b) for a, b in zip(_attn_row_structure(blk, grid_rows), interior))
    n_ci = 2 * WIN_COLS - 1
    qc = np.arange(GRID_W)
    c0 = np.clip(qc - WIN_COLS // 2, 0, GRID_W - WIN_COLS)
    col_valid = (qc[None, :] >= c0[:, None]) & (qc[None, :] < c0[:, None] + WIN_COLS)
    col_sel = (col_valid[:, :, None]
               & ((qc[None, :, None] - qc[:, None, None] + (WIN_COLS - 1)) == np.arange(n_ci))).astype(np.float32)
    by_col = jnp.einsum('hrc,qkc->hrqk', rpb.astype(F32), jnp.asarray(col_sel), precision=lax.Precision.HIGHEST)
    return jnp.where(jnp.asarray(col_valid)[None, None], by_col * LOG2_E, MASKED)


def _attn_call(qr, qp, kr, v, bias_by_col, *, batch, seq, ctx_len, with_ctx_queries):
    assert ctx_len == Q_BLOCK and seq % Q_BLOCK == 0
    grid_rows = seq // GRID_W
    n_blocks = grid_rows // Q_ROWS
    n_steps = n_blocks + (1 if with_ctx_queries else 0)
    ctx_block0 = batch * n_blocks
    n_out = batch * seq + (batch * ctx_len if with_ctx_queries else 0)
    n_keys = KEY_ROWS * GRID_W

    def q_map(b, h, i):
        return (jnp.where(i < n_blocks, b * n_blocks + i, ctx_block0 + b), h)

    width = HEADS_PER_STEP * HEAD_DIM
    return pl.pallas_call(
        functools.partial(_attn_body, n_blocks=n_blocks, grid_rows=grid_rows),
        grid=(batch, NA_HEADS // HEADS_PER_STEP, n_steps),
        in_specs=[pl.BlockSpec((Q_BLOCK, width), q_map),
                  pl.BlockSpec((Q_BLOCK, width), q_map),
                  pl.BlockSpec((seq, width), lambda b, h, i: (b, h)),
                  pl.BlockSpec((seq, width), lambda b, h, i: (b, h)),
                  pl.BlockSpec((ctx_len, width), lambda b, h, i: (ctx_block0 + b, h)),
                  pl.BlockSpec((ctx_len, width), lambda b, h, i: (ctx_block0 + b, h)),
                  pl.BlockSpec((HEADS_PER_STEP,) + bias_by_col.shape[1:], lambda b, h, i: (h, 0, 0, 0))],
        out_specs=pl.BlockSpec((Q_BLOCK, width), q_map),
        out_shape=jax.ShapeDtypeStruct((n_out, NA_HEADS * HEAD_DIM), BF16),
        scratch_shapes=[pltpu.VMEM((HEADS_PER_STEP, Q_BLOCK, n_keys), F32)],
        compiler_params=_params(3),
    )(qr, qp, kr, v, kr, v, bias_by_col)


def _merge_body(h_ref, a_ref, o_ref, wg_ref, bg_ref, wa_ref, wb_ref, m_ref, *, tn):
    d = m_ref.shape[1]
    hb, ab, ob = h_ref[...], a_ref[...], o_ref[...]
    for n0 in range(0, d, tn):
        cols = slice(n0, n0 + tn)
        gcols = slice(d + n0, d + n0 + tn)
        g_a = jax.nn.sigmoid(jnp.dot(hb, wg_ref[:, cols], preferred_element_type=F32) + bg_ref[:, cols])
        g_b = jax.nn.sigmoid(jnp.dot(hb, wg_ref[:, gcols], preferred_element_type=F32) + bg_ref[:, gcols])
        pa = jnp.dot(ab, wa_ref[:, cols], preferred_element_type=F32)
        pb = jnp.dot(ob, wb_ref[:, cols], preferred_element_type=F32)
        m_ref[:, cols] = (g_a * pa + g_b * pb).astype(BF16)


def _merge_call(h, a, o, w_gate_b, b_gate, w_a_b, w_b_b, *, layer, n_rows, tm):
    d = h.shape[1]

    def row_map(i):
        return (i, 0)

    return pl.pallas_call(
        functools.partial(_merge_body, tn=min(d, 512)),
        grid=(n_rows // tm,),
        in_specs=[pl.BlockSpec((tm, d), row_map),
                  pl.BlockSpec((tm, a.shape[1]), row_map),
                  pl.BlockSpec((tm, o.shape[1]), row_map),
                  _resident_layer(w_gate_b, layer), _resident((1, 2 * d)),
                  _resident_layer(w_a_b, layer), _resident_layer(w_b_b, layer)],
        out_specs=pl.BlockSpec((tm, d), row_map),
        out_shape=jax.ShapeDtypeStruct((n_rows, d), BF16),
        compiler_params=_params(1),
    )(h, a, o, w_gate_b, b_gate.reshape(1, 2 * d), w_a_b, w_b_b)


def _outproj_body(*refs, n_main_tiles, two_src, n_experts):
    if two_src:
        xa_ref, xb_ref = refs[:2]
        refs = refs[2:]
    else:
        xa_ref = refs[0]
        refs = refs[1:]
    (m_ref, wo_ref, mod_ref, nw_ref, wr_ref, br_ref,
     xo_ref, hp_ref, route_ref, route_t_ref, cnt_ref, carry_ref) = refs
    i = pl.program_id(0)
    if two_src:
        x = jnp.where(i < n_main_tiles, xa_ref[...], xb_ref[...])
    else:
        x = xa_ref[...]
    tm, d = x.shape

    @pl.when(i == 0)
    def _():
        carry_ref[...] = jnp.zeros_like(carry_ref)

    y = jnp.dot(m_ref[...], wo_ref[...], preferred_element_type=F32)
    x_new = x + mod_ref[0, 2:3, :] * y
    xo_ref[...] = x_new
    h = _rms_modulate(x_new, nw_ref[...], mod_ref[0, 3:4, :], mod_ref[0, 4:5, :])
    hp_ref[...] = _to_token_tiles(_pack_bf16_pair(h[:, :d // 2], h[:, d // 2:]))

    h_hi = h.astype(BF16)
    h_lo = (h - h_hi.astype(F32)).astype(BF16)
    by_hi = jnp.dot(h_hi, wr_ref[...], preferred_element_type=F32)
    logits = (by_hi[:, :LANES] + by_hi[:, LANES:]
              + jnp.dot(h_lo, wr_ref[:, :LANES], preferred_element_type=F32) + br_ref[...])
    lane = lax.broadcasted_iota(jnp.int32, logits.shape, 1)
    lane_f = lane.astype(F32)
    far = jnp.float32(4 * LANES)

    def first_argmax(vals):
        top = jnp.max(vals, axis=-1, keepdims=True)
        return top, jnp.min(jnp.where(vals == top, lane_f, far), axis=-1, keepdims=True)

    g_logits = jnp.where((lane >= n_experts) & (lane < n_experts + N_GROUPS), logits, MASKED)
    g_top, g_lane = first_argmax(g_logits)
    g_prob = 1.0 / jnp.sum(jnp.exp(g_logits - g_top), axis=-1, keepdims=True)
    e_lo = (g_lane - n_experts) * EXPERTS_PER_GROUP
    e_logits = jnp.where((lane_f >= e_lo) & (lane_f < e_lo + EXPERTS_PER_GROUP), logits, MASKED)
    top1, e1 = first_argmax(e_logits)
    top2, e2 = first_argmax(jnp.where(lane_f == e1, MASKED, e_logits))
    t = jnp.exp(top2 - top1)
    w1 = g_prob / (1.0 + t)
    w2 = g_prob * t / (1.0 + t)

    sel1, sel2 = lane_f == e1, lane_f == e2
    onehot = jnp.where(sel1 | sel2, 1.0, 0.0)
    r_i = lax.broadcasted_iota(jnp.int32, (tm, tm), 0)
    c_i = lax.broadcasted_iota(jnp.int32, (tm, tm), 1)
    earlier = jnp.where(c_i < r_i, 1.0, 0.0).astype(BF16)
    before = jnp.dot(earlier, onehot.astype(BF16), preferred_element_type=F32) + carry_ref[...]
    rank1 = jnp.sum(jnp.where(sel1, before, 0.0), axis=-1, keepdims=True)
    rank2 = jnp.sum(jnp.where(sel2, before, 0.0), axis=-1, keepdims=True)
    carry_ref[...] += jnp.sum(onehot, axis=0, keepdims=True)
    cnt_ref[...] = carry_ref[...]

    route = jnp.zeros_like(logits)
    for k, val in enumerate((e1, e2, w1, w2, rank1, rank2)):
        route = jnp.where(lane == k, val, route)
    route_ref[...] = route
    route_t_ref[...] = route.T[:8, :]


def _outproj_call(x_main, x_ctx, m, w_out_b, mod, norm_w, w_router, b_router, *, layer, rows_per_mod, tm,
                  n_experts):
    d = x_main.shape[1]
    two_src = x_ctx is not None
    n_main = x_main.shape[0] if two_src else m.shape[0]
    n_rows = m.shape[0]
    n_main_tiles = n_main // tm
    n_mod = mod.shape[0]
    tiles_per_mod = rows_per_mod // tm

    def row_map(i):
        return (i, 0)

    x_specs = [pl.BlockSpec((tm, d), lambda i: (jnp.minimum(i, n_main_tiles - 1), 0))]
    x_args = [x_main]
    if two_src:
        x_specs.append(pl.BlockSpec((tm, d), lambda i: (jnp.maximum(i - n_main_tiles, 0), 0),
                                    pipeline_mode=pl.Buffered(1)))
        x_args.append(x_ctx)
    in_specs = x_specs + [
        pl.BlockSpec((tm, d), row_map),
        _resident_layer(w_out_b, layer),
        pl.BlockSpec((1, 6, d), lambda i: (jnp.minimum(i // tiles_per_mod, n_mod - 1), 0, 0)),
        _resident((1, d)),
        _resident(w_router.shape),
        _resident((1, LANES)),
    ]
    return pl.pallas_call(
        functools.partial(_outproj_body, n_main_tiles=n_main_tiles, two_src=two_src, n_experts=n_experts),
        grid=(n_rows // tm,),
        in_specs=in_specs,
        out_specs=[pl.BlockSpec((tm, d), row_map),
                   pl.BlockSpec((tm, d // 2 // LANES, LANES), lambda i: (i, 0, 0)),
                   pl.BlockSpec((tm, LANES), row_map),
                   pl.BlockSpec((8, tm), lambda i: (0, i)),
                   pl.BlockSpec((1, LANES), lambda i: (0, 0))],
        out_shape=[jax.ShapeDtypeStruct((n_rows, d), F32),
                   jax.ShapeDtypeStruct((n_rows, d // 2 // LANES, LANES), U32),
                   jax.ShapeDtypeStruct((n_rows, LANES), F32),
                   jax.ShapeDtypeStruct((8, n_rows), F32),
                   jax.ShapeDtypeStruct((1, LANES), F32)],
        scratch_shapes=[pltpu.VMEM((1, LANES), F32)],
        compiler_params=_params(1),
    )(*x_args, m, w_out_b, mod, norm_w.reshape(1, d), w_router, b_router)


def _row_copy(src_ref, src_row, dst_ref, dst_row, sem):
    return pltpu.make_async_copy(src_ref.at[src_row], dst_ref.at[dst_row], sem)


def _to_token_tiles(rows):
    return rows.reshape(rows.shape[0], rows.shape[1] // LANES, LANES)


def _from_token_tiles(tiles):
    return tiles.reshape(tiles.shape[0], tiles.shape[1] * LANES)


def _slot_rows_body(p1_ref, p2_ref, init_ref, dst_ref, stage_ref, sem, *, tokens_per_step, row_stride):
    i = pl.program_id(0)

    @pl.when(i == 0)
    def _():
        load = pltpu.make_async_copy(init_ref, stage_ref, sem)
        load.start()
        load.wait()

    def put(r, carry):
        t = i * tokens_per_step + r
        stage_ref[p1_ref[t]] = t
        stage_ref[p2_ref[t]] = row_stride + t
        return carry

    lax.fori_loop(0, tokens_per_step, put, 0, unroll=8)

    @pl.when(i == pl.num_programs(0) - 1)
    def _():
        store = pltpu.make_async_copy(stage_ref, dst_ref, sem)
        store.start()
        store.wait()


def _slot_rows_call(pos1, pos2, init_rows, *, row_stride, tokens_per_step):
    n_tok = pos1.shape[0]
    n_map = init_rows.shape[0]
    return pl.pallas_call(
        functools.partial(_slot_rows_body, tokens_per_step=tokens_per_step, row_stride=row_stride),
        grid_spec=pltpu.PrefetchScalarGridSpec(
            num_scalar_prefetch=2,
            grid=(n_tok // tokens_per_step,),
            in_specs=[pl.BlockSpec(memory_space=pl.ANY)],
            out_specs=pl.BlockSpec(memory_space=pl.ANY),
            scratch_shapes=[pltpu.SMEM((n_map,), jnp.int32), pltpu.SemaphoreType.DMA(())],
        ),
        out_shape=jax.ShapeDtypeStruct((n_map,), jnp.int32),
        compiler_params=_params(1),
    )(pos1, pos2, init_rows)


def _ffn_body(first_ref, count_ref, src_ref, dst_ref, h_ref, wg_ref, wu_ref, wd_ref, yk_ref,
              wgb_ref, wub_ref, wdb_ref, xbuf, ybuf, sem_g, sem_s, *, tm, spare_rows):
    e = pl.program_id(0)
    n_experts = pl.num_programs(0)
    half = xbuf.shape[2] * LANES

    def gather_row(tile, r):
        buf = tile % GATHER_DEPTH
        return _row_copy(h_ref, src_ref[(tile + 1) * tm + r], xbuf.at[buf], r, sem_g.at[buf])

    def scatter_row(tile, r, buf):
        return _row_copy(ybuf.at[buf], r, yk_ref, dst_ref[(tile + 1) * tm + r], sem_s.at[buf])

    def wait_tile(buf_ref, sem):
        pltpu.make_async_copy(buf_ref, buf_ref, sem).wait()

    @pl.when(e == 0)
    def _():
        ybuf[...] = jnp.zeros_like(ybuf)

        def prime(r, carry):
            _row_copy(ybuf.at[0], r, yk_ref, spare_rows + r, sem_s.at[0]).start()
            for tile in range(GATHER_DEPTH - 1):
                gather_row(tile, r).start()
            return carry

        lax.fori_loop(0, tm, prime, 0)

    wgb_ref[...] = wg_ref[0, 0].astype(BF16)
    wub_ref[...] = wu_ref[0, 0].astype(BF16)
    wdb_ref[...] = wd_ref[0, 0].astype(BF16)

    def run_tile(j, carry):
        slot = j % 2
        other = 1 - slot
        xslot = j % GATHER_DEPTH
        wait_tile(xbuf.at[xslot], sem_g.at[xslot])
        lo, hi = _unpack_bf16_pair(_from_token_tiles(xbuf[xslot]))
        lo, hi = lo.astype(BF16), hi.astype(BF16)
        wait_tile(ybuf.at[slot], sem_s.at[slot])
        for r in range(tm):
            gather_row(j + GATHER_DEPTH - 1, r).start()
            scatter_row(j - 1, r, other).start()

        def up(w_ref):
            return (jnp.dot(lo, w_ref[:half, :], preferred_element_type=F32)
                    + jnp.dot(hi, w_ref[half:, :], preferred_element_type=F32))

        g = up(wgb_ref)
        hid = (g * jax.nn.sigmoid(g) * up(wub_ref)).astype(BF16)
        y = jnp.dot(hid, wdb_ref[...], preferred_element_type=F32)
        ybuf[slot] = _to_token_tiles(_pack_bf16_pair(y[:, :half], y[:, half:]))
        return carry

    first = first_ref[e]
    lax.fori_loop(first, first + count_ref[e], run_tile, 0)

    @pl.when(e == n_experts - 1)
    def _():
        n_used = first + count_ref[e]
        j = n_used - 1
        slot = j % 2
        other = 1 - slot

        def flush(r, carry):
            scatter_row(j, r, slot).start()
            return carry

        wait_tile(ybuf.at[other], sem_s.at[other])
        lax.fori_loop(0, tm, flush, 0)
        wait_tile(ybuf.at[slot], sem_s.at[slot])
        for ahead in range(GATHER_DEPTH - 1):
            xslot = (n_used + ahead) % GATHER_DEPTH
            wait_tile(xbuf.at[xslot], sem_g.at[xslot])


def _ffn_call(first_tile, tile_count, dst_rows, h_packed, w_gate, w_up, w_down, *, layer, tm, row_stride,
              n_out_rows):
    _, sub, _ = h_packed.shape
    _, n_experts, d, de = w_gate.shape
    src_rows = jnp.where(dst_rows < 2 * row_stride, dst_rows % row_stride, 0)

    def w_map(e, ft, tc, sr, dr):
        return (layer, e, 0, 0)

    return pl.pallas_call(
        functools.partial(_ffn_body, tm=tm, spare_rows=2 * row_stride),
        grid_spec=pltpu.PrefetchScalarGridSpec(
            num_scalar_prefetch=4,
            grid=(n_experts,),
            in_specs=[pl.BlockSpec(memory_space=pl.ANY),
                      pl.BlockSpec((1, 1, d, de), w_map),
                      pl.BlockSpec((1, 1, d, de), w_map),
                      pl.BlockSpec((1, 1, de, d), w_map)],
            out_specs=pl.BlockSpec(memory_space=pl.ANY),
            scratch_shapes=[pltpu.VMEM((d, de), BF16), pltpu.VMEM((d, de), BF16), pltpu.VMEM((de, d), BF16),
                            pltpu.VMEM((GATHER_DEPTH, tm, sub, LANES), U32), pltpu.VMEM((2, tm, sub, LANES), U32),
                            pltpu.SemaphoreType.DMA((GATHER_DEPTH,)), pltpu.SemaphoreType.DMA((2,))],
        ),
        out_shape=jax.ShapeDtypeStruct((n_out_rows, sub, LANES), U32),
        compiler_params=_params(1),
    )(first_tile, tile_count, src_rows, dst_rows, h_packed, w_gate, w_up, w_down)


def _combine_body(x_ref, y1_ref, y2_ref, route_ref, mod_ref, fw_ref, o_ref, *, final_norm):
    w1, w2 = route_ref[:, 2:3], route_ref[:, 3:4]
    lo1, hi1 = _unpack_bf16_pair(_from_token_tiles(y1_ref[...]))
    lo2, hi2 = _unpack_bf16_pair(_from_token_tiles(y2_ref[...]))
    y = jnp.concatenate([w1 * lo1 + w2 * lo2, w1 * hi1 + w2 * hi2], axis=-1)
    x_new = x_ref[...] + mod_ref[0, 5:6, :] * y
    if final_norm:
        x_new = x_new * lax.rsqrt(jnp.mean(x_new * x_new, axis=-1, keepdims=True) + RMS_EPS) * fw_ref[...]
    o_ref[...] = x_new


def _combine_call(x, yk, route, mod, final_w, *, rows_per_mod, tm, row_stride, final_norm):
    n_rows, d = x.shape
    sub = yk.shape[1]
    n_mod = mod.shape[0]
    tiles_per_mod = rows_per_mod // tm
    second = row_stride // tm

    def row_map(i):
        return (i, 0)

    return pl.pallas_call(
        functools.partial(_combine_body, final_norm=final_norm),
        grid=(n_rows // tm,),
        in_specs=[pl.BlockSpec((tm, d), row_map),
                  pl.BlockSpec((tm, sub, LANES), lambda i: (i, 0, 0)),
                  pl.BlockSpec((tm, sub, LANES), lambda i: (second + i, 0, 0)),
                  pl.BlockSpec((tm, LANES), row_map),
                  pl.BlockSpec((1, 6, d), lambda i: (jnp.minimum(i // tiles_per_mod, n_mod - 1), 0, 0)),
                  pl.BlockSpec((1, d), lambda i: (0, 0))],
        out_specs=pl.BlockSpec((tm, d), row_map),
        out_shape=jax.ShapeDtypeStruct((n_rows, d), F32),
        compiler_params=_params(1),
    )(x, yk, yk, route, mod, final_w.reshape(1, d))


def _slot_plan(route_t, counts, n_experts, tm):
    n_rows = route_t.shape[1]
    n_tiles = (2 * n_rows) // tm + n_experts
    cnt = counts[0, :n_experts].astype(jnp.int32)
    padded = ((cnt + tm - 1) // tm) * tm
    ends = jnp.cumsum(padded)
    starts = ends - padded
    fields = route_t.astype(jnp.int32)
    picks = jnp.stack([fields[0], fields[1]])[..., None] == jnp.arange(n_experts, dtype=jnp.int32)
    base = jnp.einsum('kte,e->kt', picks.astype(F32), starts.astype(F32),
                      precision=lax.Precision.HIGHEST).astype(jnp.int32)
    pos1 = base[0] + fields[4]
    pos2 = base[1] + fields[5]
    first_tile = (starts // tm).astype(jnp.int32)
    tile_count = (padded // tm).astype(jnp.int32)
    row_stride = n_rows
    spare = 2 * row_stride
    n_map = pl.cdiv((n_tiles + 1 + GATHER_DEPTH) * tm, 1024) * 1024
    init_rows = spare + jnp.arange(n_map, dtype=jnp.int32) % tm
    dst_rows = _slot_rows_call(pos1 + tm, pos2 + tm, init_rows, row_stride=row_stride, tokens_per_step=2 * tm)
    n_out_rows = spare + tm
    return first_tile, tile_count, dst_rows, row_stride, n_out_rows


def _rope_tables(batch, seq, ctx_rows):
    pairs = HEAD_DIM // 4
    t = np.arange(seq)
    pos = np.stack([t // GRID_W, t % GRID_W], axis=-1).astype(np.float32)
    inv_freq = (ROPE_THETA ** (-np.arange(pairs, dtype=np.float32) / pairs)).astype(np.float32)
    ang = pos[:, :, None] * inv_freq
    cos = np.repeat(np.cos(ang)[:, :, None, :], 2, axis=2).reshape(seq, HEAD_DIM)
    sin = np.sin(ang)
    zero = np.zeros_like(sin)
    sa = np.stack([-sin, zero], axis=2).reshape(seq, HEAD_DIM)
    sb = np.stack([zero, sin], axis=2).reshape(seq, HEAD_DIM)

    def full(tab, fill):
        return jnp.asarray(np.concatenate([np.tile(tab, (batch, 1)),
                                           np.full((ctx_rows, HEAD_DIM), fill, np.float32)]), F32)

    return full(cos, 1.0), full(sa, 0.0), full(sb, 0.0)


def kernel(x, c, ctx, c_ctx, w_ada, b_ada, norm_mix_w, norm_ffn_w, w_in, sgu_norm_w, sgu_w_s, sgu_b_s, na_rpb,
           w_merge_gate, b_merge_gate, w_branch_a, w_branch_b, w_out, w_router_group, b_router_group,
           w_router_expert, b_router_expert, w_exp_gate, w_exp_up, w_exp_down, final_norm_w):
    batch, seq, d = x.shape
    ctx_len = ctx.shape[1]
    depth = w_ada.shape[0]
    n_experts = w_exp_gate.shape[1]
    n_lat = batch * seq
    n_ctx = batch * ctx_len
    tm = 256
    tm_wide = 512
    assert seq % tm_wide == 0 and n_ctx % tm_wide == 0 and tm % SGU_CHUNK == 0 and ctx_len % SGU_CHUNK == 0

    mods = _ada_call(jnp.concatenate([c, c_ctx[None]], axis=0), w_ada, b_ada)
    cos, sa, sb = _rope_tables(batch, seq, n_ctx)
    grid_rows = seq // GRID_W

    x_cur = x.reshape(n_lat, d)
    ctx_rows = ctx.reshape(n_ctx, d)
    w_in_b, w_gate_b, w_a_b, w_b_b, w_out_b = (w.astype(BF16) for w in
                                               (w_in, w_merge_gate, w_branch_a, w_branch_b, w_out))
    x_all = None
    for l in range(depth):
        last = l == depth - 1
        mod = mods[l, :batch + 1].reshape(batch + 1, 6, d)
        w_s_b = sgu_w_s[l].astype(BF16)
        b_s_t = sgu_b_s[l].T
        if x_all is None:
            src = (x_cur, ctx_rows)
        else:
            src = (x_all, None)
        h, a, qp, qr, kr, v = _inproj_call(src[0], src[1], mod, norm_mix_w[l], w_in_b, sgu_norm_w[l], w_s_b, b_s_t,
                                           cos, sa, sb, layer=l, rows_per_mod=seq, tm=tm_wide)
        o = _attn_call(qr, qp, kr, v, _attn_bias_by_column(na_rpb[l], grid_rows), batch=batch, seq=seq,
                       ctx_len=ctx_len, with_ctx_queries=not last)
        n_rows = n_lat if last else n_lat + n_ctx
        m = _merge_call(h, a, o, w_gate_b, b_merge_gate[l], w_a_b, w_b_b, layer=l, n_rows=n_rows, tm=tm_wide)
        w_router = jnp.zeros((d, LANES), F32)
        w_router = w_router.at[:, :n_experts].set(w_router_expert[l])
        w_router = w_router.at[:, n_experts:n_experts + N_GROUPS].set(w_router_group[l])
        w_router_hi = w_router.astype(BF16)
        w_router_lo = (w_router - w_router_hi.astype(F32)).astype(BF16)
        w_router = jnp.concatenate([w_router_hi, w_router_lo], axis=1)
        b_router = jnp.zeros((1, LANES), F32)
        b_router = b_router.at[0, :n_experts].set(b_router_expert[l])
        b_router = b_router.at[0, n_experts:n_experts + N_GROUPS].set(b_router_group[l])
        x_mid, h_packed, route, route_t, counts = _outproj_call(
            src[0], src[1], m, w_out_b, mod, norm_ffn_w[l], w_router, b_router, layer=l, rows_per_mod=seq,
            tm=tm_wide, n_experts=n_experts)
        first_tile, tile_count, dst_rows, row_stride, n_out_rows = _slot_plan(route_t, counts, n_experts, tm)
        yk = _ffn_call(first_tile, tile_count, dst_rows, h_packed, w_exp_gate, w_exp_up, w_exp_down, layer=l, tm=tm,
                       row_stride=row_stride, n_out_rows=n_out_rows)
        x_all = _combine_call(x_mid, yk, route, mod, final_norm_w, rows_per_mod=seq, tm=tm,
                              row_stride=row_stride, final_norm=last)
    return x_all.reshape(batch, seq, d)
```

```python
import functools

import numpy as np
import jax
import jax.numpy as jnp
from jax import lax
from jax.experimental import pallas as pl
from jax.experimental.pallas import tpu as pltpu

GRID_W = 64
SGU_CHUNK = 128
SGU_GROUPS = 8
NA_HEADS = 8
HEAD_DIM = 128
WIN_ROWS = 8
WIN_COLS = 16
ROPE_THETA = 10000.0
N_GROUPS = 4
EXPERTS_PER_GROUP = 8
RMS_EPS = 1e-6

LANES = 128
Q_ROWS = 4
Q_BLOCK = Q_ROWS * GRID_W
KEY_ROWS = Q_ROWS + WIN_ROWS - 1
HEADS_PER_STEP = 4
GATHER_DEPTH = 4
MASKED = -1e30
LOG2_E = 1.4426950408889634
QK_SCALE = HEAD_DIM ** -0.5 * LOG2_E
VMEM_LIMIT = 56 * 1024 * 1024

BF16 = jnp.bfloat16
F32 = jnp.float32
U32 = jnp.uint32


def _params(n_grid_dims, vmem=VMEM_LIMIT):
    return pltpu.CompilerParams(dimension_semantics=("arbitrary",) * n_grid_dims, vmem_limit_bytes=vmem)


def _resident(shape):
    nd = len(shape)
    return pl.BlockSpec(shape, lambda *_: (0,) * nd, pipeline_mode=pl.Buffered(1))


def _resident_layer(stacked, layer):
    nd = stacked.ndim
    return pl.BlockSpec((None,) + stacked.shape[1:], lambda *_: (layer,) + (0,) * (nd - 1),
                        pipeline_mode=pl.Buffered(1))


def _pack_bf16_pair(lo, hi):
    lo_bits = lax.bitcast_convert_type(lo.astype(BF16).astype(F32), U32)
    hi_bits = lax.bitcast_convert_type(hi.astype(BF16).astype(F32), U32)
    return (hi_bits & jnp.uint32(0xFFFF0000)) | (lo_bits >> 16)


def _unpack_bf16_pair(w):
    lo = lax.bitcast_convert_type(w << 16, F32)
    hi = lax.bitcast_convert_type(w & jnp.uint32(0xFFFF0000), F32)
    return lo, hi


def _rms_modulate(x, norm_w, shift, scale):
    y = x * lax.rsqrt(jnp.mean(x * x, axis=-1, keepdims=True) + RMS_EPS) * norm_w
    return y * (1.0 + scale) + shift


def _ada_body(ct_ref, w_ref, b_ref, o_ref, *, n_rows):
    s = ct_ref[...]
    s = s * jax.nn.sigmoid(s)
    w = w_ref[0]
    o_ref[...] = jnp.zeros_like(o_ref)
    for r in range(n_rows):
        o_ref[0, r:r + 1, :] = jnp.sum(w * s[:, r:r + 1], axis=0, keepdims=True) + b_ref[0]


def _ada_call(cond, w_ada, b_ada):
    n_rows, d = cond.shape
    depth, _, n = w_ada.shape
    tn = next(t for t in (1024, 512, 256, LANES) if n % t == 0)
    ct = jnp.zeros((d, 8), F32).at[:, :n_rows].set(cond.T)
    return pl.pallas_call(
        functools.partial(_ada_body, n_rows=n_rows),
        grid=(depth, n // tn),
        in_specs=[pl.BlockSpec((d, 8), lambda l, j: (0, 0)),
                  pl.BlockSpec((1, d, tn), lambda l, j: (l, 0, j)),
                  pl.BlockSpec((1, 1, tn), lambda l, j: (l, 0, j))],
        out_specs=pl.BlockSpec((1, 8, tn), lambda l, j: (l, 0, j)),
        out_shape=jax.ShapeDtypeStruct((depth, 8, n), F32),
        compiler_params=_params(2),
    )(ct, w_ada, b_ada.reshape(depth, 1, n))


def _inproj_body(*refs, n_main_tiles, two_src, sgu_w, na_w):
    if two_src:
        xa_ref, xb_ref = refs[:2]
        refs = refs[2:]
    else:
        xa_ref = refs[0]
        refs = refs[1:]
    (mod_ref, nw_ref, w_ref, snw_ref, ws_ref, bst_ref, cos_ref, sa_ref, sb_ref,
     h_ref, a_ref, qp_ref, qr_ref, kr_ref, v_ref) = refs
    if two_src:
        x = jnp.where(pl.program_id(0) < n_main_tiles, xa_ref[...], xb_ref[...])
    else:
        x = xa_ref[...]
    tm = x.shape[0]
    h = _rms_modulate(x, nw_ref[...], mod_ref[0, 0:1, :], mod_ref[0, 1:2, :])
    hb = h.astype(BF16)
    h_ref[...] = hb

    def proj(lo, width):
        return jnp.dot(hb, w_ref[:, lo:lo + width], preferred_element_type=F32)

    u = jax.nn.gelu(proj(0, sgu_w))
    v = jax.nn.gelu(proj(sgu_w, sgu_w))
    vn = v * lax.rsqrt(jnp.mean(v * v, axis=-1, keepdims=True) + RMS_EPS) * snw_ref[...]
    vnb = vn.astype(BF16)
    gch = sgu_w // SGU_GROUPS
    for c in range(tm // SGU_CHUNK):
        rows = slice(c * SGU_CHUNK, (c + 1) * SGU_CHUNK)
        for g in range(SGU_GROUPS):
            cols = slice(g * gch, (g + 1) * gch)
            z = jnp.dot(ws_ref[g], vnb[rows, cols], preferred_element_type=F32) + bst_ref[:, g:g + 1]
            a_ref[rows, cols] = (u[rows, cols] * z).astype(BF16)

    cos, sa, sb = cos_ref[...], sa_ref[...], sb_ref[...]

    def rope_into(p, out_ref):
        for hh in range(NA_HEADS):
            cols = slice(hh * HEAD_DIM, (hh + 1) * HEAD_DIM)
            xh = p[:, cols]
            out_ref[:, cols] = (xh * cos + pltpu.roll(xh, HEAD_DIM - 32, 1) * sa
                                + pltpu.roll(xh, 32, 1) * sb).astype(BF16)

    q = proj(2 * sgu_w, na_w) * QK_SCALE
    qp_ref[...] = q.astype(BF16)
    rope_into(q, qr_ref)
    rope_into(proj(2 * sgu_w + na_w, na_w), kr_ref)
    v_ref[...] = proj(2 * sgu_w + 2 * na_w, na_w).astype(BF16)


def _inproj_call(x_main, x_ctx, mod, norm_w, w_in_b, sgu_norm_w, w_s_b, b_s_t, cos, sa, sb,
                 *, layer, rows_per_mod, tm):
    d = x_main.shape[1]
    n_main = x_main.shape[0]
    two_src = x_ctx is not None
    tt = n_main + (x_ctx.shape[0] if two_src else 0)
    n_main_tiles = n_main // tm
    n_mod = mod.shape[0]
    sgu_w = sgu_norm_w.shape[-1]
    na_w = NA_HEADS * HEAD_DIM
    tiles_per_mod = rows_per_mod // tm

    def row_map(i):
        return (i, 0)

    x_specs = [pl.BlockSpec((tm, d), lambda i: (jnp.minimum(i, n_main_tiles - 1), 0))]
    x_args = [x_main]
    if two_src:
        x_specs.append(pl.BlockSpec((tm, d), lambda i: (jnp.maximum(i - n_main_tiles, 0), 0),
                                    pipeline_mode=pl.Buffered(1)))
        x_args.append(x_ctx)
    in_specs = x_specs + [
        pl.BlockSpec((1, 6, d), lambda i: (jnp.minimum(i // tiles_per_mod, n_mod - 1), 0, 0)),
        _resident((1, d)),
        _resident_layer(w_in_b, layer),
        _resident((1, sgu_w)),
        _resident(w_s_b.shape),
        _resident(b_s_t.shape),
        pl.BlockSpec((tm, HEAD_DIM), row_map),
        pl.BlockSpec((tm, HEAD_DIM), row_map),
        pl.BlockSpec((tm, HEAD_DIM), row_map),
    ]
    out_widths = [d, sgu_w, na_w, na_w, na_w, na_w]
    return pl.pallas_call(
        functools.partial(_inproj_body, n_main_tiles=n_main_tiles, two_src=two_src, sgu_w=sgu_w, na_w=na_w),
        grid=(tt // tm,),
        in_specs=in_specs,
        out_specs=[pl.BlockSpec((tm, w), row_map) for w in out_widths],
        out_shape=[jax.ShapeDtypeStruct((tt, w), BF16) for w in out_widths],
        compiler_params=_params(1),
    )(*x_args, mod, norm_w.reshape(1, d), w_in_b, sgu_norm_w.reshape(1, sgu_w), w_s_b, b_s_t, cos, sa, sb)


def _softmax_pv(scores, values):
    m = functools.reduce(jnp.maximum, [jnp.max(s, axis=-1, keepdims=True) for s in scores])
    ps = [jnp.exp2(s - m) for s in scores]
    denom = functools.reduce(jnp.add, [jnp.sum(p, axis=-1, keepdims=True) for p in ps])
    acc = functools.reduce(jnp.add, [jnp.dot(p.astype(BF16), v, preferred_element_type=F32)
                                     for p, v in zip(ps, values)])
    return acc / denom


def _qk(q, k):
    return lax.dot_general(q, k, (((1,), (1,)), ((), ())), preferred_element_type=F32)


def _attn_row_structure(blk, grid_rows):
    wr = min(WIN_ROWS, grid_rows)
    k_row0 = int(np.clip(blk * Q_ROWS - WIN_ROWS // 2, 0, grid_rows - KEY_ROWS))
    qr = blk * Q_ROWS + np.arange(Q_ROWS)
    kr = k_row0 + np.arange(KEY_ROWS)
    r0 = np.clip(qr - wr // 2, 0, grid_rows - wr)
    valid = (kr[None, :] >= r0[:, None]) & (kr[None, :] < r0[:, None] + wr)
    ri = np.clip(kr[None, :] - qr[:, None] + (WIN_ROWS - 1), 0, 2 * WIN_ROWS - 2)
    return valid, ri


def _attn_body(qr_ref, qp_ref, k_ref, v_ref, kc_ref, vc_ref, bcol_ref, o_ref, bias_ref, *, n_blocks, grid_rows):
    i = pl.program_id(2)
    n_keys = KEY_ROWS * GRID_W

    heads = [slice(hh * HEAD_DIM, (hh + 1) * HEAD_DIM) for hh in range(HEADS_PER_STEP)]

    def build_bias(blk):
        valid, ri = _attn_row_structure(blk, grid_rows)
        for hh in range(HEADS_PER_STEP):
            for jr in range(Q_ROWS):
                for kl in range(KEY_ROWS):
                    piece = (bcol_ref[hh, int(ri[jr, kl])] if valid[jr, kl]
                             else jnp.full((GRID_W, GRID_W), MASKED, F32))
                    bias_ref[hh, jr * GRID_W:(jr + 1) * GRID_W, kl * GRID_W:(kl + 1) * GRID_W] = piece

    for blk in sorted({0, min(1, n_blocks - 1), n_blocks - 1}):
        pl.when(i == blk)(functools.partial(build_bias, blk))

    @pl.when(i < n_blocks)
    def _():
        k_row0 = jnp.clip(i * Q_ROWS - WIN_ROWS // 2, 0, grid_rows - KEY_ROWS)
        start = pl.multiple_of(k_row0 * GRID_W, GRID_W)
        for hh, cols in enumerate(heads):
            k_loc = k_ref[pl.ds(start, n_keys), cols]
            v_loc = v_ref[pl.ds(start, n_keys), cols]
            s_loc = _qk(qr_ref[:, cols], k_loc) + bias_ref[hh]
            s_ctx = _qk(qp_ref[:, cols], kc_ref[:, cols])
            o_ref[:, cols] = _softmax_pv([s_loc, s_ctx], [v_loc, vc_ref[:, cols]]).astype(BF16)

    @pl.when(i >= n_blocks)
    def _():
        for cols in heads:
            s_ctx = _qk(qp_ref[:, cols], kc_ref[:, cols])
            o_ref[:, cols] = _softmax_pv([s_ctx], [vc_ref[:, cols]]).astype(BF16)


def _attn_bias_by_column(rpb, grid_rows):
    n_blocks = grid_rows // Q_ROWS
    assert min(WIN_ROWS, grid_rows) == WIN_ROWS and grid_rows >= KEY_ROWS and grid_rows % Q_ROWS == 0
    interior = _attn_row_structure(min(1, n_blocks - 1), grid_rows)
    for blk in range(1, n_blocks - 1):
        assert all(np.array_equal(a, b) for a, b in zip(_attn_row_structure(blk, grid_rows), interior))
    n_ci = 2 * WIN_COLS - 1
    qc = np.arange(GRID_W)
    c0 = np.clip(qc - WIN_COLS // 2, 0, GRID_W - WIN_COLS)
    col_valid = (qc[None, :] >= c0[:, None]) & (qc[None, :] < c0[:, None] + WIN_COLS)
    col_sel = (col_valid[:, :, None]
               & ((qc[None, :, None] - qc[:, None, None] + (WIN_COLS - 1)) == np.arange(n_ci))).astype(np.float32)
    by_col = jnp.einsum('hrc,qkc->hrqk', rpb.astype(F32), jnp.asarray(col_sel), precision=lax.Precision.HIGHEST)
    return jnp.where(jnp.asarray(col_valid)[None, None], by_col * LOG2_E, MASKED)


def _attn_call(qr, qp, kr, v, bias_by_col, *, batch, seq, ctx_len, with_ctx_queries):
    assert ctx_len == Q_BLOCK and seq % Q_BLOCK == 0
    grid_rows = seq // GRID_W
    n_blocks = grid_rows // Q_ROWS
    n_steps = n_blocks + (1 if with_ctx_queries else 0)
    ctx_block0 = batch * n_blocks
    n_out = batch * seq + (batch * ctx_len if with_ctx_queries else 0)
    n_keys = KEY_ROWS * GRID_W

    def q_map(b, h, i):
        return (jnp.where(i < n_blocks, b * n_blocks + i, ctx_block0 + b), h)

    width = HEADS_PER_STEP * HEAD_DIM
    return pl.pallas_call(
        functools.partial(_attn_body, n_blocks=n_blocks, grid_rows=grid_rows),
        grid=(batch, NA_HEADS // HEADS_PER_STEP, n_steps),
        in_specs=[pl.BlockSpec((Q_BLOCK, width), q_map),
                  pl.BlockSpec((Q_BLOCK, width), q_map),
                  pl.BlockSpec((seq, width), lambda b, h, i: (b, h)),
                  pl.BlockSpec((seq, width), lambda b, h, i: (b, h)),
                  pl.BlockSpec((ctx_len, width), lambda b, h, i: (ctx_block0 + b, h)),
                  pl.BlockSpec((ctx_len, width), lambda b, h, i: (ctx_block0 + b, h)),
                  pl.BlockSpec((HEADS_PER_STEP,) + bias_by_col.shape[1:], lambda b, h, i: (h, 0, 0, 0))],
        out_specs=pl.BlockSpec((Q_BLOCK, width), q_map),
        out_shape=jax.ShapeDtypeStruct((n_out, NA_HEADS * HEAD_DIM), BF16),
        scratch_shapes=[pltpu.VMEM((HEADS_PER_STEP, Q_BLOCK, n_keys), F32)],
        compiler_params=_params(3),
    )(qr, qp, kr, v, kr, v, bias_by_col)


def _merge_body(h_ref, a_ref, o_ref, wg_ref, bg_ref, wa_ref, wb_ref, m_ref, *, tn):
    d = m_ref.shape[1]
    hb, ab, ob = h_ref[...], a_ref[...], o_ref[...]
    for n0 in range(0, d, tn):
        cols = slice(n0, n0 + tn)
        gcols = slice(d + n0, d + n0 + tn)
        g_a = jax.nn.sigmoid(jnp.dot(hb, wg_ref[:, cols], preferred_element_type=F32) + bg_ref[:, cols])
        g_b = jax.nn.sigmoid(jnp.dot(hb, wg_ref[:, gcols], preferred_element_type=F32) + bg_ref[:, gcols])
        pa = jnp.dot(ab, wa_ref[:, cols], preferred_element_type=F32)
        pb = jnp.dot(ob, wb_ref[:, cols], preferred_element_type=F32)
        m_ref[:, cols] = (g_a * pa + g_b * pb).astype(BF16)


def _merge_call(h, a, o, w_gate_b, b_gate, w_a_b, w_b_b, *, layer, n_rows, tm):
    d = h.shape[1]

    def row_map(i):
        return (i, 0)

    return pl.pallas_call(
        functools.partial(_merge_body, tn=min(d, 512)),
        grid=(n_rows // tm,),
        in_specs=[pl.BlockSpec((tm, d), row_map),
                  pl.BlockSpec((tm, a.shape[1]), row_map),
                  pl.BlockSpec((tm, o.shape[1]), row_map),
                  _resident_layer(w_gate_b, layer), _resident((1, 2 * d)),
                  _resident_layer(w_a_b, layer), _resident_layer(w_b_b, layer)],
        out_specs=pl.BlockSpec((tm, d), row_map),
        out_shape=jax.ShapeDtypeStruct((n_rows, d), BF16),
        compiler_params=_params(1),
    )(h, a, o, w_gate_b, b_gate.reshape(1, 2 * d), w_a_b, w_b_b)


def _outproj_body(*refs, n_main_tiles, two_src, n_experts):
    if two_src:
        xa_ref, xb_ref = refs[:2]
        refs = refs[2:]
    else:
        xa_ref = refs[0]
        refs = refs[1:]
    (m_ref, wo_ref, mod_ref, nw_ref, wr_ref, br_ref,
     xo_ref, hp_ref, route_ref, route_t_ref, cnt_ref, carry_ref) = refs
    i = pl.program_id(0)
    if two_src:
        x = jnp.where(i < n_main_tiles, xa_ref[...], xb_ref[...])
    else:
        x = xa_ref[...]
    tm, d = x.shape

    @pl.when(i == 0)
    def _():
        carry_ref[...] = jnp.zeros_like(carry_ref)

    y = jnp.dot(m_ref[...], wo_ref[...], preferred_element_type=F32)
    x_new = x + mod_ref[0, 2:3, :] * y
    xo_ref[...] = x_new
    h = _rms_modulate(x_new, nw_ref[...], mod_ref[0, 3:4, :], mod_ref[0, 4:5, :])
    hp_ref[...] = _to_token_tiles(_pack_bf16_pair(h[:, :d // 2], h[:, d // 2:]))

    h_hi = h.astype(BF16)
    h_lo = (h - h_hi.astype(F32)).astype(BF16)
    by_hi = jnp.dot(h_hi, wr_ref[...], preferred_element_type=F32)
    logits = (by_hi[:, :LANES] + by_hi[:, LANES:]
              + jnp.dot(h_lo, wr_ref[:, :LANES], preferred_element_type=F32) + br_ref[...])
    lane = lax.broadcasted_iota(jnp.int32, logits.shape, 1)
    lane_f = lane.astype(F32)
    far = jnp.float32(4 * LANES)

    def first_argmax(vals):
        top = jnp.max(vals, axis=-1, keepdims=True)
        return top, jnp.min(jnp.where(vals == top, lane_f, far), axis=-1, keepdims=True)

    g_logits = jnp.where((lane >= n_experts) & (lane < n_experts + N_GROUPS), logits, MASKED)
    g_top, g_lane = first_argmax(g_logits)
    g_prob = 1.0 / jnp.sum(jnp.exp(g_logits - g_top), axis=-1, keepdims=True)
    e_lo = (g_lane - n_experts) * EXPERTS_PER_GROUP
    e_logits = jnp.where((lane_f >= e_lo) & (lane_f < e_lo + EXPERTS_PER_GROUP), logits, MASKED)
    top1, e1 = first_argmax(e_logits)
    top2, e2 = first_argmax(jnp.where(lane_f == e1, MASKED, e_logits))
    t = jnp.exp(top2 - top1)
    w1 = g_prob / (1.0 + t)
    w2 = g_prob * t / (1.0 + t)

    sel1, sel2 = lane_f == e1, lane_f == e2
    onehot = jnp.where(sel1 | sel2, 1.0, 0.0)
    r_i = lax.broadcasted_iota(jnp.int32, (tm, tm), 0)
    c_i = lax.broadcasted_iota(jnp.int32, (tm, tm), 1)
    earlier = jnp.where(c_i < r_i, 1.0, 0.0).astype(BF16)
    before = jnp.dot(earlier, onehot.astype(BF16), preferred_element_type=F32) + carry_ref[...]
    rank1 = jnp.sum(jnp.where(sel1, before, 0.0), axis=-1, keepdims=True)
    rank2 = jnp.sum(jnp.where(sel2, before, 0.0), axis=-1, keepdims=True)
    carry_ref[...] += jnp.sum(onehot, axis=0, keepdims=True)
    cnt_ref[...] = carry_ref[...]

    route = jnp.zeros_like(logits)
    for k, val in enumerate((e1, e2, w1, w2, rank1, rank2)):
        route = jnp.where(lane == k, val, route)
    route_ref[...] = route
    route_t_ref[...] = route.T[:8, :]


def _outproj_call(x_main, x_ctx, m, w_out_b, mod, norm_w, w_router, b_router, *, layer, rows_per_mod, tm,
                  n_experts):
    d = x_main.shape[1]
    two_src = x_ctx is not None
    n_main = x_main.shape[0] if two_src else m.shape[0]
    n_rows = m.shape[0]
    n_main_tiles = n_main // tm
    n_mod = mod.shape[0]
    tiles_per_mod = rows_per_mod // tm

    def row_map(i):
        return (i, 0)

    x_specs = [pl.BlockSpec((tm, d), lambda i: (jnp.minimum(i, n_main_tiles - 1), 0))]
    x_args = [x_main]
    if two_src:
        x_specs.append(pl.BlockSpec((tm, d), lambda i: (jnp.maximum(i - n_main_tiles, 0), 0),
                                    pipeline_mode=pl.Buffered(1)))
        x_args.append(x_ctx)
    in_specs = x_specs + [
        pl.BlockSpec((tm, d), row_map),
        _resident_layer(w_out_b, layer),
        pl.BlockSpec((1, 6, d), lambda i: (jnp.minimum(i // tiles_per_mod, n_mod - 1), 0, 0)),
        _resident((1, d)),
        _resident(w_router.shape),
        _resident((1, LANES)),
    ]
    return pl.pallas_call(
        functools.partial(_outproj_body, n_main_tiles=n_main_tiles, two_src=two_src, n_experts=n_experts),
        grid=(n_rows // tm,),
        in_specs=in_specs,
        out_specs=[pl.BlockSpec((tm, d), row_map),
                   pl.BlockSpec((tm, d // 2 // LANES, LANES), lambda i: (i, 0, 0)),
                   pl.BlockSpec((tm, LANES), row_map),
                   pl.BlockSpec((8, tm), lambda i: (0, i)),
                   pl.BlockSpec((1, LANES), lambda i: (0, 0))],
        out_shape=[jax.ShapeDtypeStruct((n_rows, d), F32),
                   jax.ShapeDtypeStruct((n_rows, d // 2 // LANES, LANES), U32),
                   jax.ShapeDtypeStruct((n_rows, LANES), F32),
                   jax.ShapeDtypeStruct((8, n_rows), F32),
                   jax.ShapeDtypeStruct((1, LANES), F32)],
        scratch_shapes=[pltpu.VMEM((1, LANES), F32)],
        compiler_params=_params(1),
    )(*x_args, m, w_out_b, mod, norm_w.reshape(1, d), w_router, b_router)


def _row_copy(src_ref, src_row, dst_ref, dst_row, sem):
    return pltpu.make_async_copy(src_ref.at[src_row], dst_ref.at[dst_row], sem)


def _to_token_tiles(rows):
    return rows.reshape(rows.shape[0], rows.shape[1] // LANES, LANES)


def _from_token_tiles(tiles):
    return tiles.reshape(tiles.shape[0], tiles.shape[1] * LANES)


def _slot_rows_body(p1_ref, p2_ref, init_ref, dst_ref, stage_ref, sem, *, tokens_per_step, row_stride):
    i = pl.program_id(0)

    @pl.when(i == 0)
    def _():
        load = pltpu.make_async_copy(init_ref, stage_ref, sem)
        load.start()
        load.wait()

    def put(r, carry):
        t = i * tokens_per_step + r
        stage_ref[p1_ref[t]] = t
        stage_ref[p2_ref[t]] = row_stride + t
        return carry

    lax.fori_loop(0, tokens_per_step, put, 0, unroll=8)

    @pl.when(i == pl.num_programs(0) - 1)
    def _():
        store = pltpu.make_async_copy(stage_ref, dst_ref, sem)
        store.start()
        store.wait()


def _slot_rows_call(pos1, pos2, init_rows, *, row_stride, tokens_per_step):
    n_tok = pos1.shape[0]
    n_map = init_rows.shape[0]
    return pl.pallas_call(
        functools.partial(_slot_rows_body, tokens_per_step=tokens_per_step, row_stride=row_stride),
        grid_spec=pltpu.PrefetchScalarGridSpec(
            num_scalar_prefetch=2,
            grid=(n_tok // tokens_per_step,),
            in_specs=[pl.BlockSpec(memory_space=pl.ANY)],
            out_specs=pl.BlockSpec(memory_space=pl.ANY),
            scratch_shapes=[pltpu.SMEM((n_map,), jnp.int32), pltpu.SemaphoreType.DMA(())],
        ),
        out_shape=jax.ShapeDtypeStruct((n_map,), jnp.int32),
        compiler_params=_params(1),
    )(pos1, pos2, init_rows)


def _ffn_body(first_ref, count_ref, src_ref, dst_ref, h_ref, wg_ref, wu_ref, wd_ref, yk_ref,
              wgb_ref, wub_ref, wdb_ref, xbuf, ybuf, sem_g, sem_s, *, tm, spare_rows):
    e = pl.program_id(0)
    n_experts = pl.num_programs(0)
    half = xbuf.shape[2] * LANES

    def gather_row(tile, r):
        buf = tile % GATHER_DEPTH
        return _row_copy(h_ref, src_ref[(tile + 1) * tm + r], xbuf.at[buf], r, sem_g.at[buf])

    def scatter_row(tile, r, buf):
        return _row_copy(ybuf.at[buf], r, yk_ref, dst_ref[(tile + 1) * tm + r], sem_s.at[buf])

    def wait_tile(buf_ref, sem):
        pltpu.make_async_copy(buf_ref, buf_ref, sem).wait()

    @pl.when(e == 0)
    def _():
        ybuf[...] = jnp.zeros_like(ybuf)

        def prime(r, carry):
            _row_copy(ybuf.at[0], r, yk_ref, spare_rows + r, sem_s.at[0]).start()
            for tile in range(GATHER_DEPTH - 1):
                gather_row(tile, r).start()
            return carry

        lax.fori_loop(0, tm, prime, 0)

    wgb_ref[...] = wg_ref[0, 0].astype(BF16)
    wub_ref[...] = wu_ref[0, 0].astype(BF16)
    wdb_ref[...] = wd_ref[0, 0].astype(BF16)

    def run_tile(j, carry):
        slot = j % 2
        other = 1 - slot
        xslot = j % GATHER_DEPTH
        wait_tile(xbuf.at[xslot], sem_g.at[xslot])
        lo, hi = _unpack_bf16_pair(_from_token_tiles(xbuf[xslot]))
        lo, hi = lo.astype(BF16), hi.astype(BF16)
        wait_tile(ybuf.at[slot], sem_s.at[slot])
        for r in range(tm):
            gather_row(j + GATHER_DEPTH - 1, r).start()
            scatter_row(j - 1, r, other).start()

        def up(w_ref):
            return (jnp.dot(lo, w_ref[:half, :], preferred_element_type=F32)
                    + jnp.dot(hi, w_ref[half:, :], preferred_element_type=F32))

        g = up(wgb_ref)
        hid = (g * jax.nn.sigmoid(g) * up(wub_ref)).astype(BF16)
        y = jnp.dot(hid, wdb_ref[...], preferred_element_type=F32)
        ybuf[slot] = _to_token_tiles(_pack_bf16_pair(y[:, :half], y[:, half:]))
        return carry

    first = first_ref[e]
    lax.fori_loop(first, first + count_ref[e], run_tile, 0)

    @pl.when(e == n_experts - 1)
    def _():
        n_used = first + count_ref[e]
        j = n_used - 1
        slot = j % 2
        other = 1 - slot

        def flush(r, carry):
            scatter_row(j, r, slot).start()
            return carry

        wait_tile(ybuf.at[other], sem_s.at[other])
        lax.fori_loop(0, tm, flush, 0)
        wait_tile(ybuf.at[slot], sem_s.at[slot])
        for ahead in range(GATHER_DEPTH - 1):
            xslot = (n_used + ahead) % GATHER_DEPTH
            wait_tile(xbuf.at[xslot], sem_g.at[xslot])


def _ffn_call(first_tile, tile_count, dst_rows, h_packed, w_gate, w_up, w_down, *, layer, tm, row_stride,
              n_out_rows):
    _, sub, _ = h_packed.shape
    _, n_experts, d, de = w_gate.shape
    src_rows = jnp.where(dst_rows < 2 * row_stride, dst_rows % row_stride, 0)

    def w_map(e, ft, tc, sr, dr):
        return (layer, e, 0, 0)

    return pl.pallas_call(
        functools.partial(_ffn_body, tm=tm, spare_rows=2 * row_stride),
        grid_spec=pltpu.PrefetchScalarGridSpec(
            num_scalar_prefetch=4,
            grid=(n_experts,),
            in_specs=[pl.BlockSpec(memory_space=pl.ANY),
                      pl.BlockSpec((1, 1, d, de), w_map),
                      pl.BlockSpec((1, 1, d, de), w_map),
                      pl.BlockSpec((1, 1, de, d), w_map)],
            out_specs=pl.BlockSpec(memory_space=pl.ANY),
            scratch_shapes=[pltpu.VMEM((d, de), BF16), pltpu.VMEM((d, de), BF16), pltpu.VMEM((de, d), BF16),
                            pltpu.VMEM((GATHER_DEPTH, tm, sub, LANES), U32), pltpu.VMEM((2, tm, sub, LANES), U32),
                            pltpu.SemaphoreType.DMA((GATHER_DEPTH,)), pltpu.SemaphoreType.DMA((2,))],
        ),
        out_shape=jax.ShapeDtypeStruct((n_out_rows, sub, LANES), U32),
        compiler_params=_params(1),
    )(first_tile, tile_count, src_rows, dst_rows, h_packed, w_gate, w_up, w_down)


def _combine_body(x_ref, y1_ref, y2_ref, route_ref, mod_ref, fw_ref, o_ref, *, final_norm):
    w1, w2 = route_ref[:, 2:3], route_ref[:, 3:4]
    lo1, hi1 = _unpack_bf16_pair(_from_token_tiles(y1_ref[...]))
    lo2, hi2 = _unpack_bf16_pair(_from_token_tiles(y2_ref[...]))
    y = jnp.concatenate([w1 * lo1 + w2 * lo2, w1 * hi1 + w2 * hi2], axis=-1)
    x_new = x_ref[...] + mod_ref[0, 5:6, :] * y
    if final_norm:
        x_new = x_new * lax.rsqrt(jnp.mean(x_new * x_new, axis=-1, keepdims=True) + RMS_EPS) * fw_ref[...]
    o_ref[...] = x_new


def _combine_call(x, yk, route, mod, final_w, *, rows_per_mod, tm, row_stride, final_norm):
    n_rows, d = x.shape
    sub = yk.shape[1]
    n_mod = mod.shape[0]
    tiles_per_mod = rows_per_mod // tm
    second = row_stride // tm

    def row_map(i):
        return (i, 0)

    return pl.pallas_call(
        functools.partial(_combine_body, final_norm=final_norm),
        grid=(n_rows // tm,),
        in_specs=[pl.BlockSpec((tm, d), row_map),
                  pl.BlockSpec((tm, sub, LANES), lambda i: (i, 0, 0)),
                  pl.BlockSpec((tm, sub, LANES), lambda i: (second + i, 0, 0)),
                  pl.BlockSpec((tm, LANES), row_map),
                  pl.BlockSpec((1, 6, d), lambda i: (jnp.minimum(i // tiles_per_mod, n_mod - 1), 0, 0)),
                  pl.BlockSpec((1, d), lambda i: (0, 0))],
        out_specs=pl.BlockSpec((tm, d), row_map),
        out_shape=jax.ShapeDtypeStruct((n_rows, d), F32),
        compiler_params=_params(1),
    )(x, yk, yk, route, mod, final_w.reshape(1, d))


def _slot_plan(route_t, counts, n_experts, tm):
    n_rows = route_t.shape[1]
    n_tiles = (2 * n_rows) // tm + n_experts
    cnt = counts[0, :n_experts].astype(jnp.int32)
    padded = ((cnt + tm - 1) // tm) * tm
    ends = jnp.cumsum(padded)
    starts = ends - padded
    fields = route_t.astype(jnp.int32)
    picks = jnp.stack([fields[0], fields[1]])[..., None] == jnp.arange(n_experts, dtype=jnp.int32)
    base = jnp.einsum('kte,e->kt', picks.astype(F32), starts.astype(F32),
                      precision=lax.Precision.HIGHEST).astype(jnp.int32)
    pos1 = base[0] + fields[4]
    pos2 = base[1] + fields[5]
    first_tile = (starts // tm).astype(jnp.int32)
    tile_count = (padded // tm).astype(jnp.int32)
    row_stride = n_rows
    spare = 2 * row_stride
    n_map = pl.cdiv((n_tiles + 1 + GATHER_DEPTH) * tm, 1024) * 1024
    init_rows = spare + jnp.arange(n_map, dtype=jnp.int32) % tm
    dst_rows = _slot_rows_call(pos1 + tm, pos2 + tm, init_rows, row_stride=row_stride, tokens_per_step=2 * tm)
    n_out_rows = spare + tm
    return first_tile, tile_count, dst_rows, row_stride, n_out_rows


def _rope_tables(batch, seq, ctx_rows):
    pairs = HEAD_DIM // 4
    t = np.arange(seq)
    pos = np.stack([t // GRID_W, t % GRID_W], axis=-1).astype(np.float32)
    inv_freq = (ROPE_THETA ** (-np.arange(pairs, dtype=np.float32) / pairs)).astype(np.float32)
    ang = pos[:, :, None] * inv_freq
    cos = np.repeat(np.cos(ang)[:, :, None, :], 2, axis=2).reshape(seq, HEAD_DIM)
    sin = np.sin(ang)
    zero = np.zeros_like(sin)
    sa = np.stack([-sin, zero], axis=2).reshape(seq, HEAD_DIM)
    sb = np.stack([zero, sin], axis=2).reshape(seq, HEAD_DIM)

    def full(tab, fill):
        return jnp.asarray(np.concatenate([np.tile(tab, (batch, 1)),
                                           np.full((ctx_rows, HEAD_DIM), fill, np.float32)]), F32)

    return full(cos, 1.0), full(sa, 0.0), full(sb, 0.0)


def kernel(x, c, ctx, c_ctx, w_ada, b_ada, norm_mix_w, norm_ffn_w, w_in, sgu_norm_w, sgu_w_s, sgu_b_s, na_rpb,
           w_merge_gate, b_merge_gate, w_branch_a, w_branch_b, w_out, w_router_group, b_router_group,
           w_router_expert, b_router_expert, w_exp_gate, w_exp_up, w_exp_down, final_norm_w):
    batch, seq, d = x.shape
    ctx_len = ctx.shape[1]
    depth = w_ada.shape[0]
    n_experts = w_exp_gate.shape[1]
    n_lat = batch * seq
    n_ctx = batch * ctx_len
    tm = 256
    tm_wide = 512
    assert seq % tm_wide == 0 and n_ctx % tm_wide == 0 and tm % SGU_CHUNK == 0 and ctx_len % SGU_CHUNK == 0

    mods = _ada_call(jnp.concatenate([c, c_ctx[None]], axis=0), w_ada, b_ada)
    cos, sa, sb = _rope_tables(batch, seq, n_ctx)
    grid_rows = seq // GRID_W

    x_cur = x.reshape(n_lat, d)
    ctx_rows = ctx.reshape(n_ctx, d)
    w_in_b, w_gate_b, w_a_b, w_b_b, w_out_b = (w.astype(BF16) for w in
                                               (w_in, w_merge_gate, w_branch_a, w_branch_b, w_out))
    x_all = None
    for l in range(depth):
        last = l == depth - 1
        mod = mods[l, :batch + 1].reshape(batch + 1, 6, d)
        w_s_b = sgu_w_s[l].astype(BF16)
        b_s_t = sgu_b_s[l].T
        if x_all is None:
            src = (x_cur, ctx_rows)
        else:
            src = (x_all, None)
        h, a, qp, qr, kr, v = _inproj_call(src[0], src[1], mod, norm_mix_w[l], w_in_b, sgu_norm_w[l], w_s_b, b_s_t,
                                           cos, sa, sb, layer=l, rows_per_mod=seq, tm=tm_wide)
        o = _attn_call(qr, qp, kr, v, _attn_bias_by_column(na_rpb[l], grid_rows), batch=batch, seq=seq,
                       ctx_len=ctx_len, with_ctx_queries=not last)
        n_rows = n_lat if last else n_lat + n_ctx
        m = _merge_call(h, a, o, w_gate_b, b_merge_gate[l], w_a_b, w_b_b, layer=l, n_rows=n_rows, tm=tm_wide)
        w_router = jnp.zeros((d, LANES), F32)
        w_router = w_router.at[:, :n_experts].set(w_router_expert[l])
        w_router = w_router.at[:, n_experts:n_experts + N_GROUPS].set(w_router_group[l])
        w_router_hi = w_router.astype(BF16)
        w_router_lo = (w_router - w_router_hi.astype(F32)).astype(BF16)
        w_router = jnp.concatenate([w_router_hi, w_router_lo], axis=1)
        b_router = jnp.zeros((1, LANES), F32)
        b_router = b_router.at[0, :n_experts].set(b_router_expert[l])
        b_router = b_router.at[0, n_experts:n_experts + N_GROUPS].set(b_router_group[l])
        x_mid, h_packed, route, route_t, counts = _outproj_call(
            src[0], src[1], m, w_out_b, mod, norm_ffn_w[l], w_router, b_router, layer=l, rows_per_mod=seq,
            tm=tm_wide, n_experts=n_experts)
        first_tile, tile_count, dst_rows, row_stride, n_out_rows = _slot_plan(route_t, counts, n_experts, tm)
        yk = _ffn_call(first_tile, tile_count, dst_rows, h_packed, w_exp_gate, w_exp_up, w_exp_down, layer=l, tm=tm,
                       row_stride=row_stride, n_out_rows=n_out_rows)
        x_all = _combine_call(x_mid, yk, route, mod, final_norm_w, rows_per_mod=seq, tm=tm,
                              row_stride=row_stride, final_norm=last)
    return x_all.reshape(batch, seq, d)
```

```python
import functools

import numpy as np
import jax
import jax.numpy as jnp
from jax import lax
from jax.experimental import pallas as pl
from jax.experimental.pallas import tpu as pltpu

GRID_W = 64
SGU_CHUNK = 128
SGU_GROUPS = 8
NA_HEADS = 8
HEAD_DIM = 128
WIN_ROWS = 8
WIN_COLS = 16
ROPE_THETA = 10000.0
N_GROUPS = 4
EXPERTS_PER_GROUP = 8
RMS_EPS = 1e-6

LANES = 128
Q_ROWS = 4
Q_BLOCK = Q_ROWS * GRID_W
KEY_ROWS = Q_ROWS + WIN_ROWS - 1
HEADS_PER_STEP = 4
SCATTER_DEPTH = 3
GATHER_DEPTH = 4
MASKED = -1e30
LOG2_E = 1.4426950408889634
QK_SCALE = HEAD_DIM ** -0.5 * LOG2_E
VMEM_LIMIT = 56 * 1024 * 1024

BF16 = jnp.bfloat16
F32 = jnp.float32
U32 = jnp.uint32


def _params(n_grid_dims, vmem=VMEM_LIMIT):
    return pltpu.CompilerParams(dimension_semantics=("arbitrary",) * n_grid_dims, vmem_limit_bytes=vmem)


def _resident(shape):
    nd = len(shape)
    return pl.BlockSpec(shape, lambda *_: (0,) * nd, pipeline_mode=pl.Buffered(1))


def _resident_layer(stacked, layer):
    nd = stacked.ndim
    return pl.BlockSpec((None,) + stacked.shape[1:], lambda *_: (layer,) + (0,) * (nd - 1),
                        pipeline_mode=pl.Buffered(1))


def _pack_bf16_pair(lo, hi):
    lo_bits = lax.bitcast_convert_type(lo.astype(BF16).astype(F32), U32)
    hi_bits = lax.bitcast_convert_type(hi.astype(BF16).astype(F32), U32)
    return (hi_bits & jnp.uint32(0xFFFF0000)) | (lo_bits >> 16)


def _unpack_bf16_pair(w):
    lo = lax.bitcast_convert_type(w << 16, F32)
    hi = lax.bitcast_convert_type(w & jnp.uint32(0xFFFF0000), F32)
    return lo, hi


def _rms_modulate(x, norm_w, shift, scale):
    y = x * lax.rsqrt(jnp.mean(x * x, axis=-1, keepdims=True) + RMS_EPS) * norm_w
    return y * (1.0 + scale) + shift


def _ada_body(ct_ref, w_ref, b_ref, o_ref, *, n_rows):
    s = ct_ref[...]
    s = s * jax.nn.sigmoid(s)
    w = w_ref[0]
    o_ref[...] = jnp.zeros_like(o_ref)
    for r in range(n_rows):
        o_ref[0, r:r + 1, :] = jnp.sum(w * s[:, r:r + 1], axis=0, keepdims=True) + b_ref[0]


def _ada_call(cond, w_ada, b_ada):
    n_rows, d = cond.shape
    depth, _, n = w_ada.shape
    tn = next(t for t in (1024, 512, 256, LANES) if n % t == 0)
    ct = jnp.zeros((d, 8), F32).at[:, :n_rows].set(cond.T)
    return pl.pallas_call(
        functools.partial(_ada_body, n_rows=n_rows),
        grid=(depth, n // tn),
        in_specs=[pl.BlockSpec((d, 8), lambda l, j: (0, 0)),
                  pl.BlockSpec((1, d, tn), lambda l, j: (l, 0, j)),
                  pl.BlockSpec((1, 1, tn), lambda l, j: (l, 0, j))],
        out_specs=pl.BlockSpec((1, 8, tn), lambda l, j: (l, 0, j)),
        out_shape=jax.ShapeDtypeStruct((depth, 8, n), F32),
        compiler_params=_params(2),
    )(ct, w_ada, b_ada.reshape(depth, 1, n))


def _inproj_body(*refs, n_main_tiles, two_src, sgu_w, na_w):
    if two_src:
        xa_ref, xb_ref = refs[:2]
        refs = refs[2:]
    else:
        xa_ref = refs[0]
        refs = refs[1:]
    (mod_ref, nw_ref, w_ref, snw_ref, ws_ref, bst_ref, cos_ref, sa_ref, sb_ref,
     h_ref, a_ref, qp_ref, qr_ref, kr_ref, v_ref) = refs
    if two_src:
        x = jnp.where(pl.program_id(0) < n_main_tiles, xa_ref[...], xb_ref[...])
    else:
        x = xa_ref[...]
    tm = x.shape[0]
    h = _rms_modulate(x, nw_ref[...], mod_ref[0, 0:1, :], mod_ref[0, 1:2, :])
    hb = h.astype(BF16)
    h_ref[...] = hb

    def proj(lo, width):
        return jnp.dot(hb, w_ref[:, lo:lo + width], preferred_element_type=F32)

    u = jax.nn.gelu(proj(0, sgu_w))
    v = jax.nn.gelu(proj(sgu_w, sgu_w))
    vn = v * lax.rsqrt(jnp.mean(v * v, axis=-1, keepdims=True) + RMS_EPS) * snw_ref[...]
    vnb = vn.astype(BF16)
    gch = sgu_w // SGU_GROUPS
    for c in range(tm // SGU_CHUNK):
        rows = slice(c * SGU_CHUNK, (c + 1) * SGU_CHUNK)
        for g in range(SGU_GROUPS):
            cols = slice(g * gch, (g + 1) * gch)
            z = jnp.dot(ws_ref[g], vnb[rows, cols], preferred_element_type=F32) + bst_ref[:, g:g + 1]
            a_ref[rows, cols] = (u[rows, cols] * z).astype(BF16)

    cos, sa, sb = cos_ref[...], sa_ref[...], sb_ref[...]

    def rope_into(p, out_ref):
        for hh in range(NA_HEADS):
            cols = slice(hh * HEAD_DIM, (hh + 1) * HEAD_DIM)
            xh = p[:, cols]
            out_ref[:, cols] = (xh * cos + pltpu.roll(xh, HEAD_DIM - 32, 1) * sa
                                + pltpu.roll(xh, 32, 1) * sb).astype(BF16)

    q = proj(2 * sgu_w, na_w) * QK_SCALE
    qp_ref[...] = q.astype(BF16)
    rope_into(q, qr_ref)
    rope_into(proj(2 * sgu_w + na_w, na_w), kr_ref)
    v_ref[...] = proj(2 * sgu_w + 2 * na_w, na_w).astype(BF16)


def _inproj_call(x_main, x_ctx, mod, norm_w, w_in_b, sgu_norm_w, w_s_b, b_s_t, cos, sa, sb,
                 *, layer, rows_per_mod, tm):
    d = x_main.shape[1]
    n_main = x_main.shape[0]
    two_src = x_ctx is not None
    tt = n_main + (x_ctx.shape[0] if two_src else 0)
    n_main_tiles = n_main // tm
    n_mod = mod.shape[0]
    sgu_w = sgu_norm_w.shape[-1]
    na_w = NA_HEADS * HEAD_DIM
    tiles_per_mod = rows_per_mod // tm

    def row_map(i):
        return (i, 0)

    x_specs = [pl.BlockSpec((tm, d), lambda i: (jnp.minimum(i, n_main_tiles - 1), 0))]
    x_args = [x_main]
    if two_src:
        x_specs.append(pl.BlockSpec((tm, d), lambda i: (jnp.maximum(i - n_main_tiles, 0), 0),
                                    pipeline_mode=pl.Buffered(1)))
        x_args.append(x_ctx)
    in_specs = x_specs + [
        pl.BlockSpec((1, 6, d), lambda i: (jnp.minimum(i // tiles_per_mod, n_mod - 1), 0, 0)),
        _resident((1, d)),
        _resident_layer(w_in_b, layer),
        _resident((1, sgu_w)),
        _resident(w_s_b.shape),
        _resident(b_s_t.shape),
        pl.BlockSpec((tm, HEAD_DIM), row_map),
        pl.BlockSpec((tm, HEAD_DIM), row_map),
        pl.BlockSpec((tm, HEAD_DIM), row_map),
    ]
    out_widths = [d, sgu_w, na_w, na_w, na_w, na_w]
    return pl.pallas_call(
        functools.partial(_inproj_body, n_main_tiles=n_main_tiles, two_src=two_src, sgu_w=sgu_w, na_w=na_w),
        grid=(tt // tm,),
        in_specs=in_specs,
        out_specs=[pl.BlockSpec((tm, w), row_map) for w in out_widths],
        out_shape=[jax.ShapeDtypeStruct((tt, w), BF16) for w in out_widths],
        compiler_params=_params(1),
    )(*x_args, mod, norm_w.reshape(1, d), w_in_b, sgu_norm_w.reshape(1, sgu_w), w_s_b, b_s_t, cos, sa, sb)


def _softmax_pv(scores, values):
    m = functools.reduce(jnp.maximum, [jnp.max(s, axis=-1, keepdims=True) for s in scores])
    ps = [jnp.exp2(s - m) for s in scores]
    denom = functools.reduce(jnp.add, [jnp.sum(p, axis=-1, keepdims=True) for p in ps])
    acc = functools.reduce(jnp.add, [jnp.dot(p.astype(BF16), v, preferred_element_type=F32)
                                     for p, v in zip(ps, values)])
    return acc / denom


def _qk(q, k):
    return lax.dot_general(q, k, (((1,), (1,)), ((), ())), preferred_element_type=F32)


def _attn_row_structure(blk, grid_rows):
    wr = min(WIN_ROWS, grid_rows)
    k_row0 = int(np.clip(blk * Q_ROWS - WIN_ROWS // 2, 0, grid_rows - KEY_ROWS))
    qr = blk * Q_ROWS + np.arange(Q_ROWS)
    kr = k_row0 + np.arange(KEY_ROWS)
    r0 = np.clip(qr - wr // 2, 0, grid_rows - wr)
    valid = (kr[None, :] >= r0[:, None]) & (kr[None, :] < r0[:, None] + wr)
    ri = np.clip(kr[None, :] - qr[:, None] + (WIN_ROWS - 1), 0, 2 * WIN_ROWS - 2)
    return valid, ri


def _attn_body(qr_ref, qp_ref, k_ref, v_ref, kc_ref, vc_ref, bcol_ref, o_ref, bias_ref, *, n_blocks, grid_rows):
    i = pl.program_id(2)
    n_keys = KEY_ROWS * GRID_W

    heads = [slice(hh * HEAD_DIM, (hh + 1) * HEAD_DIM) for hh in range(HEADS_PER_STEP)]

    def build_bias(blk):
        valid, ri = _attn_row_structure(blk, grid_rows)
        for hh in range(HEADS_PER_STEP):
            for jr in range(Q_ROWS):
                for kl in range(KEY_ROWS):
                    piece = (bcol_ref[hh, int(ri[jr, kl])] if valid[jr, kl]
                             else jnp.full((GRID_W, GRID_W), MASKED, F32))
                    bias_ref[hh, jr * GRID_W:(jr + 1) * GRID_W, kl * GRID_W:(kl + 1) * GRID_W] = piece

    for blk in sorted({0, min(1, n_blocks - 1), n_blocks - 1}):
        pl.when(i == blk)(functools.partial(build_bias, blk))

    @pl.when(i < n_blocks)
    def _():
        k_row0 = jnp.clip(i * Q_ROWS - WIN_ROWS // 2, 0, grid_rows - KEY_ROWS)
        start = pl.multiple_of(k_row0 * GRID_W, GRID_W)
        for hh, cols in enumerate(heads):
            k_loc = k_ref[pl.ds(start, n_keys), cols]
            v_loc = v_ref[pl.ds(start, n_keys), cols]
            s_loc = _qk(qr_ref[:, cols], k_loc) + bias_ref[hh]
            s_ctx = _qk(qp_ref[:, cols], kc_ref[:, cols])
            o_ref[:, cols] = _softmax_pv([s_loc, s_ctx], [v_loc, vc_ref[:, cols]]).astype(BF16)

    @pl.when(i >= n_blocks)
    def _():
        for cols in heads:
            s_ctx = _qk(qp_ref[:, cols], kc_ref[:, cols])
            o_ref[:, cols] = _softmax_pv([s_ctx], [vc_ref[:, cols]]).astype(BF16)


def _attn_bias_by_column(rpb, grid_rows):
    n_blocks = grid_rows // Q_ROWS
    assert min(WIN_ROWS, grid_rows) == WIN_ROWS and grid_rows >= KEY_ROWS and grid_rows % Q_ROWS == 0
    interior = _attn_row_structure(min(1, n_blocks - 1), grid_rows)
    for blk in range(1, n_blocks - 1):
        assert all(np.array_equal(a, b) for a, b in zip(_attn_row_structure(blk, grid_rows), interior))
    n_ci = 2 * WIN_COLS - 1
    qc = np.arange(GRID_W)
    c0 = np.clip(qc - WIN_COLS // 2, 0, GRID_W - WIN_COLS)
    col_valid = (qc[None, :] >= c0[:, None]) & (qc[None, :] < c0[:, None] + WIN_COLS)
    col_sel = (col_valid[:, :, None]
               & ((qc[None, :, None] - qc[:, None, None] + (WIN_COLS - 1)) == np.arange(n_ci))).astype(np.float32)
    by_col = jnp.einsum('hrc,qkc->hrqk', rpb.astype(F32), jnp.asarray(col_sel), precision=lax.Precision.HIGHEST)
    return jnp.where(jnp.asarray(col_valid)[None, None], by_col * LOG2_E, MASKED)


def _attn_call(qr, qp, kr, v, bias_by_col, *, batch, seq, ctx_len, with_ctx_queries):
    assert ctx_len == Q_BLOCK and seq % Q_BLOCK == 0
    grid_rows = seq // GRID_W
    n_blocks = grid_rows // Q_ROWS
    n_steps = n_blocks + (1 if with_ctx_queries else 0)
    ctx_block0 = batch * n_blocks
    n_out = batch * seq + (batch * ctx_len if with_ctx_queries else 0)
    n_keys = KEY_ROWS * GRID_W

    def q_map(b, h, i):
        return (jnp.where(i < n_blocks, b * n_blocks + i, ctx_block0 + b), h)

    width = HEADS_PER_STEP * HEAD_DIM
    return pl.pallas_call(
        functools.partial(_attn_body, n_blocks=n_blocks, grid_rows=grid_rows),
        grid=(batch, NA_HEADS // HEADS_PER_STEP, n_steps),
        in_specs=[pl.BlockSpec((Q_BLOCK, width), q_map),
                  pl.BlockSpec((Q_BLOCK, width), q_map),
                  pl.BlockSpec((seq, width), lambda b, h, i: (b, h)),
                  pl.BlockSpec((seq, width), lambda b, h, i: (b, h)),
                  pl.BlockSpec((ctx_len, width), lambda b, h, i: (ctx_block0 + b, h)),
                  pl.BlockSpec((ctx_len, width), lambda b, h, i: (ctx_block0 + b, h)),
                  pl.BlockSpec((HEADS_PER_STEP,) + bias_by_col.shape[1:], lambda b, h, i: (h, 0, 0, 0))],
        out_specs=pl.BlockSpec((Q_BLOCK, width), q_map),
        out_shape=jax.ShapeDtypeStruct((n_out, NA_HEADS * HEAD_DIM), BF16),
        scratch_shapes=[pltpu.VMEM((HEADS_PER_STEP, Q_BLOCK, n_keys), F32)],
        compiler_params=_params(3),
    )(qr, qp, kr, v, kr, v, bias_by_col)


def _merge_body(h_ref, a_ref, o_ref, wg_ref, bg_ref, wa_ref, wb_ref, m_ref, *, tn):
    d = m_ref.shape[1]
    hb, ab, ob = h_ref[...], a_ref[...], o_ref[...]
    for n0 in range(0, d, tn):
        cols = slice(n0, n0 + tn)
        gcols = slice(d + n0, d + n0 + tn)
        g_a = jax.nn.sigmoid(jnp.dot(hb, wg_ref[:, cols], preferred_element_type=F32) + bg_ref[:, cols])
        g_b = jax.nn.sigmoid(jnp.dot(hb, wg_ref[:, gcols], preferred_element_type=F32) + bg_ref[:, gcols])
        pa = jnp.dot(ab, wa_ref[:, cols], preferred_element_type=F32)
        pb = jnp.dot(ob, wb_ref[:, cols], preferred_element_type=F32)
        m_ref[:, cols] = (g_a * pa + g_b * pb).astype(BF16)


def _merge_call(h, a, o, w_gate_b, b_gate, w_a_b, w_b_b, *, layer, n_rows, tm):
    d = h.shape[1]

    def row_map(i):
        return (i, 0)

    return pl.pallas_call(
        functools.partial(_merge_body, tn=min(d, 512)),
        grid=(n_rows // tm,),
        in_specs=[pl.BlockSpec((tm, d), row_map),
                  pl.BlockSpec((tm, a.shape[1]), row_map),
                  pl.BlockSpec((tm, o.shape[1]), row_map),
                  _resident_layer(w_gate_b, layer), _resident((1, 2 * d)),
                  _resident_layer(w_a_b, layer), _resident_layer(w_b_b, layer)],
        out_specs=pl.BlockSpec((tm, d), row_map),
        out_shape=jax.ShapeDtypeStruct((n_rows, d), BF16),
        compiler_params=_params(1),
    )(h, a, o, w_gate_b, b_gate.reshape(1, 2 * d), w_a_b, w_b_b)


def _outproj_body(*refs, n_main_tiles, two_src, n_experts):
    if two_src:
        xa_ref, xb_ref = refs[:2]
        refs = refs[2:]
    else:
        xa_ref = refs[0]
        refs = refs[1:]
    (m_ref, wo_ref, mod_ref, nw_ref, wr_ref, br_ref,
     xo_ref, hp_ref, route_ref, route_t_ref, cnt_ref, carry_ref) = refs
    i = pl.program_id(0)
    if two_src:
        x = jnp.where(i < n_main_tiles, xa_ref[...], xb_ref[...])
    else:
        x = xa_ref[...]
    tm, d = x.shape

    @pl.when(i == 0)
    def _():
        carry_ref[...] = jnp.zeros_like(carry_ref)

    y = jnp.dot(m_ref[...], wo_ref[...], preferred_element_type=F32)
    x_new = x + mod_ref[0, 2:3, :] * y
    xo_ref[...] = x_new
    h = _rms_modulate(x_new, nw_ref[...], mod_ref[0, 3:4, :], mod_ref[0, 4:5, :])
    hp_ref[...] = _to_token_tiles(_pack_bf16_pair(h[:, :d // 2], h[:, d // 2:]))

    h_hi = h.astype(BF16)
    h_lo = (h - h_hi.astype(F32)).astype(BF16)
    by_hi = jnp.dot(h_hi, wr_ref[...], preferred_element_type=F32)
    logits = (by_hi[:, :LANES] + by_hi[:, LANES:]
              + jnp.dot(h_lo, wr_ref[:, :LANES], preferred_element_type=F32) + br_ref[...])
    lane = lax.broadcasted_iota(jnp.int32, logits.shape, 1)
    lane_f = lane.astype(F32)
    far = jnp.float32(4 * LANES)

    def first_argmax(vals):
        top = jnp.max(vals, axis=-1, keepdims=True)
        return top, jnp.min(jnp.where(vals == top, lane_f, far), axis=-1, keepdims=True)

    g_logits = jnp.where((lane >= n_experts) & (lane < n_experts + N_GROUPS), logits, MASKED)
    g_top, g_lane = first_argmax(g_logits)
    g_prob = 1.0 / jnp.sum(jnp.exp(g_logits - g_top), axis=-1, keepdims=True)
    e_lo = (g_lane - n_experts) * EXPERTS_PER_GROUP
    e_logits = jnp.where((lane_f >= e_lo) & (lane_f < e_lo + EXPERTS_PER_GROUP), logits, MASKED)
    top1, e1 = first_argmax(e_logits)
    top2, e2 = first_argmax(jnp.where(lane_f == e1, MASKED, e_logits))
    t = jnp.exp(top2 - top1)
    w1 = g_prob / (1.0 + t)
    w2 = g_prob * t / (1.0 + t)

    sel1, sel2 = lane_f == e1, lane_f == e2
    onehot = jnp.where(sel1 | sel2, 1.0, 0.0)
    r_i = lax.broadcasted_iota(jnp.int32, (tm, tm), 0)
    c_i = lax.broadcasted_iota(jnp.int32, (tm, tm), 1)
    earlier = jnp.where(c_i < r_i, 1.0, 0.0).astype(BF16)
    before = jnp.dot(earlier, onehot.astype(BF16), preferred_element_type=F32) + carry_ref[...]
    rank1 = jnp.sum(jnp.where(sel1, before, 0.0), axis=-1, keepdims=True)
    rank2 = jnp.sum(jnp.where(sel2, before, 0.0), axis=-1, keepdims=True)
    carry_ref[...] += jnp.sum(onehot, axis=0, keepdims=True)
    cnt_ref[...] = carry_ref[...]

    route = jnp.zeros_like(logits)
    for k, val in enumerate((e1, e2, w1, w2, rank1, rank2)):
        route = jnp.where(lane == k, val, route)
    route_ref[...] = route
    route_t_ref[...] = route.T[:8, :]


def _outproj_call(x_main, x_ctx, m, w_out_b, mod, norm_w, w_router, b_router, *, layer, rows_per_mod, tm,
                  n_experts):
    d = x_main.shape[1]
    two_src = x_ctx is not None
    n_main = x_main.shape[0] if two_src else m.shape[0]
    n_rows = m.shape[0]
    n_main_tiles = n_main // tm
    n_mod = mod.shape[0]
    tiles_per_mod = rows_per_mod // tm

    def row_map(i):
        return (i, 0)

    x_specs = [pl.BlockSpec((tm, d), lambda i: (jnp.minimum(i, n_main_tiles - 1), 0))]
    x_args = [x_main]
    if two_src:
        x_specs.append(pl.BlockSpec((tm, d), lambda i: (jnp.maximum(i - n_main_tiles, 0), 0),
                                    pipeline_mode=pl.Buffered(1)))
        x_args.append(x_ctx)
    in_specs = x_specs + [
        pl.BlockSpec((tm, d), row_map),
        _resident_layer(w_out_b, layer),
        pl.BlockSpec((1, 6, d), lambda i: (jnp.minimum(i // tiles_per_mod, n_mod - 1), 0, 0)),
        _resident((1, d)),
        _resident(w_router.shape),
        _resident((1, LANES)),
    ]
    return pl.pallas_call(
        functools.partial(_outproj_body, n_main_tiles=n_main_tiles, two_src=two_src, n_experts=n_experts),
        grid=(n_rows // tm,),
        in_specs=in_specs,
        out_specs=[pl.BlockSpec((tm, d), row_map),
                   pl.BlockSpec((tm, d // 2 // LANES, LANES), lambda i: (i, 0, 0)),
                   pl.BlockSpec((tm, LANES), row_map),
                   pl.BlockSpec((8, tm), lambda i: (0, i)),
                   pl.BlockSpec((1, LANES), lambda i: (0, 0))],
        out_shape=[jax.ShapeDtypeStruct((n_rows, d), F32),
                   jax.ShapeDtypeStruct((n_rows, d // 2 // LANES, LANES), U32),
                   jax.ShapeDtypeStruct((n_rows, LANES), F32),
                   jax.ShapeDtypeStruct((8, n_rows), F32),
                   jax.ShapeDtypeStruct((1, LANES), F32)],
        scratch_shapes=[pltpu.VMEM((1, LANES), F32)],
        compiler_params=_params(1),
    )(*x_args, m, w_out_b, mod, norm_w.reshape(1, d), w_router, b_router)


def _row_copy(src_ref, src_row, dst_ref, dst_row, sem):
    return pltpu.make_async_copy(src_ref.at[src_row], dst_ref.at[dst_row], sem)


def _to_token_tiles(rows):
    return rows.reshape(rows.shape[0], rows.shape[1] // LANES, LANES)


def _from_token_tiles(tiles):
    return tiles.reshape(tiles.shape[0], tiles.shape[1] * LANES)


def _slot_rows_body(p1_ref, p2_ref, init_ref, dst_ref, stage_ref, sem, *, tokens_per_step, row_stride):
    i = pl.program_id(0)

    @pl.when(i == 0)
    def _():
        load = pltpu.make_async_copy(init_ref, stage_ref, sem)
        load.start()
        load.wait()

    def put(r, carry):
        t = i * tokens_per_step + r
        stage_ref[p1_ref[t]] = t
        stage_ref[p2_ref[t]] = row_stride + t
        return carry

    lax.fori_loop(0, tokens_per_step, put, 0, unroll=8)

    @pl.when(i == pl.num_programs(0) - 1)
    def _():
        store = pltpu.make_async_copy(stage_ref, dst_ref, sem)
        store.start()
        store.wait()


def _slot_rows_call(pos1, pos2, init_rows, *, row_stride, tokens_per_step):
    n_tok = pos1.shape[0]
    n_map = init_rows.shape[0]
    return pl.pallas_call(
        functools.partial(_slot_rows_body, tokens_per_step=tokens_per_step, row_stride=row_stride),
        grid_spec=pltpu.PrefetchScalarGridSpec(
            num_scalar_prefetch=2,
            grid=(n_tok // tokens_per_step,),
            in_specs=[pl.BlockSpec(memory_space=pl.ANY)],
            out_specs=pl.BlockSpec(memory_space=pl.ANY),
            scratch_shapes=[pltpu.SMEM((n_map,), jnp.int32), pltpu.SemaphoreType.DMA(())],
        ),
        out_shape=jax.ShapeDtypeStruct((n_map,), jnp.int32),
        compiler_params=_params(1),
    )(pos1, pos2, init_rows)


def _ffn_body(first_ref, count_ref, src_ref, dst_ref, h_ref, wg_ref, wu_ref, wd_ref, yk_ref,
              wgb_ref, wub_ref, wdb_ref, xbuf, ybuf, sem_g, sem_s, *, tm, spare_rows):
    e = pl.program_id(0)
    n_experts = pl.num_programs(0)
    half = xbuf.shape[2] * LANES

    def gather_row(tile, r):
        buf = tile % GATHER_DEPTH
        return _row_copy(h_ref, src_ref[(tile + 1) * tm + r], xbuf.at[buf], r, sem_g.at[buf])

    def scatter_row(tile, r, buf):
        return _row_copy(ybuf.at[buf], r, yk_ref, dst_ref[(tile + 1) * tm + r], sem_s.at[buf])

    def wait_tile(buf_ref, sem):
        pltpu.make_async_copy(buf_ref, buf_ref, sem).wait()

    @pl.when(e == 0)
    def _():
        ybuf[...] = jnp.zeros_like(ybuf)

        def prime(r, carry):
            _row_copy(ybuf.at[0], r, yk_ref, spare_rows + r, sem_s.at[0]).start()
            _row_copy(ybuf.at[1], r, yk_ref, spare_rows + tm + r, sem_s.at[1]).start()
            for tile in range(GATHER_DEPTH - 1):
                gather_row(tile, r).start()
            return carry

        lax.fori_loop(0, tm, prime, 0)

    wgb_ref[...] = wg_ref[0, 0].astype(BF16)
    wub_ref[...] = wu_ref[0, 0].astype(BF16)
    wdb_ref[...] = wd_ref[0, 0].astype(BF16)

    def run_tile(j, carry):
        slot = j % SCATTER_DEPTH
        other = (j + SCATTER_DEPTH - 1) % SCATTER_DEPTH
        xslot = j % GATHER_DEPTH
        wait_tile(xbuf.at[xslot], sem_g.at[xslot])
        lo, hi = _unpack_bf16_pair(_from_token_tiles(xbuf[xslot]))
        lo, hi = lo.astype(BF16), hi.astype(BF16)
        wait_tile(ybuf.at[slot], sem_s.at[slot])
        for r in range(tm):
            gather_row(j + GATHER_DEPTH - 1, r).start()
            scatter_row(j - 1, r, other).start()

        def up(w_ref):
            return (jnp.dot(lo, w_ref[:half, :], preferred_element_type=F32)
                    + jnp.dot(hi, w_ref[half:, :], preferred_element_type=F32))

        g = up(wgb_ref)
        hid = (g * jax.nn.sigmoid(g) * up(wub_ref)).astype(BF16)
        y = jnp.dot(hid, wdb_ref[...], preferred_element_type=F32)
        ybuf[slot] = _to_token_tiles(_pack_bf16_pair(y[:, :half], y[:, half:]))
        return carry

    first = first_ref[e]
    lax.fori_loop(first, first + count_ref[e], run_tile, 0)

    @pl.when(e == n_experts - 1)
    def _():
        n_used = first + count_ref[e]
        j = n_used - 1
        slot = j % SCATTER_DEPTH
        other = (j + SCATTER_DEPTH - 1) % SCATTER_DEPTH
        older = (j + SCATTER_DEPTH - 2) % SCATTER_DEPTH

        def flush(r, carry):
            scatter_row(j, r, slot).start()
            return carry

        wait_tile(ybuf.at[older], sem_s.at[older])
        lax.fori_loop(0, tm, flush, 0)
        wait_tile(ybuf.at[other], sem_s.at[other])
        wait_tile(ybuf.at[slot], sem_s.at[slot])
        for ahead in range(GATHER_DEPTH - 1):
            xslot = (n_used + ahead) % GATHER_DEPTH
            wait_tile(xbuf.at[xslot], sem_g.at[xslot])


def _ffn_call(first_tile, tile_count, dst_rows, h_packed, w_gate, w_up, w_down, *, layer, tm, row_stride,
              n_out_rows):
    _, sub, _ = h_packed.shape
    _, n_experts, d, de = w_gate.shape
    src_rows = jnp.where(dst_rows < 2 * row_stride, dst_rows % row_stride, 0)

    def w_map(e, ft, tc, sr, dr):
        return (layer, e, 0, 0)

    return pl.pallas_call(
        functools.partial(_ffn_body, tm=tm, spare_rows=2 * row_stride),
        grid_spec=pltpu.PrefetchScalarGridSpec(
            num_scalar_prefetch=4,
            grid=(n_experts,),
            in_specs=[pl.BlockSpec(memory_space=pl.ANY),
                      pl.BlockSpec((1, 1, d, de), w_map),
                      pl.BlockSpec((1, 1, d, de), w_map),
                      pl.BlockSpec((1, 1, de, d), w_map)],
            out_specs=pl.BlockSpec(memory_space=pl.ANY),
            scratch_shapes=[pltpu.VMEM((d, de), BF16), pltpu.VMEM((d, de), BF16), pltpu.VMEM((de, d), BF16),
                            pltpu.VMEM((GATHER_DEPTH, tm, sub, LANES), U32),
                            pltpu.VMEM((SCATTER_DEPTH, tm, sub, LANES), U32),
                            pltpu.SemaphoreType.DMA((GATHER_DEPTH,)), pltpu.SemaphoreType.DMA((SCATTER_DEPTH,))],
        ),
        out_shape=jax.ShapeDtypeStruct((n_out_rows, sub, LANES), U32),
        compiler_params=_params(1),
    )(first_tile, tile_count, src_rows, dst_rows, h_packed, w_gate, w_up, w_down)


def _combine_body(x_ref, y1_ref, y2_ref, route_ref, mod_ref, fw_ref, o_ref, *, final_norm):
    w1, w2 = route_ref[:, 2:3], route_ref[:, 3:4]
    lo1, hi1 = _unpack_bf16_pair(_from_token_tiles(y1_ref[...]))
    lo2, hi2 = _unpack_bf16_pair(_from_token_tiles(y2_ref[...]))
    y = jnp.concatenate([w1 * lo1 + w2 * lo2, w1 * hi1 + w2 * hi2], axis=-1)
    x_new = x_ref[...] + mod_ref[0, 5:6, :] * y
    if final_norm:
        x_new = x_new * lax.rsqrt(jnp.mean(x_new * x_new, axis=-1, keepdims=True) + RMS_EPS) * fw_ref[...]
    o_ref[...] = x_new


def _combine_call(x, yk, route, mod, final_w, *, rows_per_mod, tm, row_stride, final_norm):
    n_rows, d = x.shape
    sub = yk.shape[1]
    n_mod = mod.shape[0]
    tiles_per_mod = rows_per_mod // tm
    second = row_stride // tm

    def row_map(i):
        return (i, 0)

    return pl.pallas_call(
        functools.partial(_combine_body, final_norm=final_norm),
        grid=(n_rows // tm,),
        in_specs=[pl.BlockSpec((tm, d), row_map),
                  pl.BlockSpec((tm, sub, LANES), lambda i: (i, 0, 0)),
                  pl.BlockSpec((tm, sub, LANES), lambda i: (second + i, 0, 0)),
                  pl.BlockSpec((tm, LANES), row_map),
                  pl.BlockSpec((1, 6, d), lambda i: (jnp.minimum(i // tiles_per_mod, n_mod - 1), 0, 0)),
                  pl.BlockSpec((1, d), lambda i: (0, 0))],
        out_specs=pl.BlockSpec((tm, d), row_map),
        out_shape=jax.ShapeDtypeStruct((n_rows, d), F32),
        compiler_params=_params(1),
    )(x, yk, yk, route, mod, final_w.reshape(1, d))


def _slot_plan(route_t, counts, n_experts, tm):
    n_rows = route_t.shape[1]
    n_tiles = (2 * n_rows) // tm + n_experts
    cnt = counts[0, :n_experts].astype(jnp.int32)
    padded = ((cnt + tm - 1) // tm) * tm
    ends = jnp.cumsum(padded)
    starts = ends - padded
    fields = route_t.astype(jnp.int32)
    picks = jnp.stack([fields[0], fields[1]])[..., None] == jnp.arange(n_experts, dtype=jnp.int32)
    base = jnp.einsum('kte,e->kt', picks.astype(F32), starts.astype(F32),
                      precision=lax.Precision.HIGHEST).astype(jnp.int32)
    pos1 = base[0] + fields[4]
    pos2 = base[1] + fields[5]
    first_tile = (starts // tm).astype(jnp.int32)
    tile_count = (padded // tm).astype(jnp.int32)
    row_stride = n_rows
    spare = 2 * row_stride
    n_map = pl.cdiv((n_tiles + 1 + GATHER_DEPTH) * tm, 1024) * 1024
    idx = jnp.arange(n_map, dtype=jnp.int32)
    init_rows = spare + ((idx // tm) % 2) * tm + idx % tm
    dst_rows = _slot_rows_call(pos1 + tm, pos2 + tm, init_rows, row_stride=row_stride, tokens_per_step=2 * tm)
    n_out_rows = spare + 2 * tm
    return first_tile, tile_count, dst_rows, row_stride, n_out_rows


def _rope_tables(batch, seq, ctx_rows):
    pairs = HEAD_DIM // 4
    t = np.arange(seq)
    pos = np.stack([t // GRID_W, t % GRID_W], axis=-1).astype(np.float32)
    inv_freq = (ROPE_THETA ** (-np.arange(pairs, dtype=np.float32) / pairs)).astype(np.float32)
    ang = pos[:, :, None] * inv_freq
    cos = np.repeat(np.cos(ang)[:, :, None, :], 2, axis=2).reshape(seq, HEAD_DIM)
    sin = np.sin(ang)
    zero = np.zeros_like(sin)
    sa = np.stack([-sin, zero], axis=2).reshape(seq, HEAD_DIM)
    sb = np.stack([zero, sin], axis=2).reshape(seq, HEAD_DIM)

    def full(tab, fill):
        return jnp.asarray(np.concatenate([np.tile(tab, (batch, 1)),
                                           np.full((ctx_rows, HEAD_DIM), fill, np.float32)]), F32)

    return full(cos, 1.0), full(sa, 0.0), full(sb, 0.0)


def kernel(x, c, ctx, c_ctx, w_ada, b_ada, norm_mix_w, norm_ffn_w, w_in, sgu_norm_w, sgu_w_s, sgu_b_s, na_rpb,
           w_merge_gate, b_merge_gate, w_branch_a, w_branch_b, w_out, w_router_group, b_router_group,
           w_router_expert, b_router_expert, w_exp_gate, w_exp_up, w_exp_down, final_norm_w):
    batch, seq, d = x.shape
    ctx_len = ctx.shape[1]
    depth = w_ada.shape[0]
    n_experts = w_exp_gate.shape[1]
    n_lat = batch * seq
    n_ctx = batch * ctx_len
    tm = 256
    tm_wide = 512
    assert seq % tm_wide == 0 and n_ctx % tm_wide == 0 and tm % SGU_CHUNK == 0 and ctx_len % SGU_CHUNK == 0

    mods = _ada_call(jnp.concatenate([c, c_ctx[None]], axis=0), w_ada, b_ada)
    cos, sa, sb = _rope_tables(batch, seq, n_ctx)
    grid_rows = seq // GRID_W

    x_cur = x.reshape(n_lat, d)
    ctx_rows = ctx.reshape(n_ctx, d)
    w_in_b, w_gate_b, w_a_b, w_b_b, w_out_b = (w.astype(BF16) for w in
                                               (w_in, w_merge_gate, w_branch_a, w_branch_b, w_out))
    x_all = None
    for l in range(depth):
        last = l == depth - 1
        mod = mods[l, :batch + 1].reshape(batch + 1, 6, d)
        w_s_b = sgu_w_s[l].astype(BF16)
        b_s_t = sgu_b_s[l].T
        if x_all is None:
            src = (x_cur, ctx_rows)
        else:
            src = (x_all, None)
        h, a, qp, qr, kr, v = _inproj_call(src[0], src[1], mod, norm_mix_w[l], w_in_b, sgu_norm_w[l], w_s_b, b_s_t,
                                           cos, sa, sb, layer=l, rows_per_mod=seq, tm=tm_wide)
        o = _attn_call(qr, qp, kr, v, _attn_bias_by_column(na_rpb[l], grid_rows), batch=batch, seq=seq,
                       ctx_len=ctx_len, with_ctx_queries=not last)
        n_rows = n_lat if last else n_lat + n_ctx
        m = _merge_call(h, a, o, w_gate_b, b_merge_gate[l], w_a_b, w_b_b, layer=l, n_rows=n_rows, tm=tm_wide)
        w_router = jnp.zeros((d, LANES), F32)
        w_router = w_router.at[:, :n_experts].set(w_router_expert[l])
        w_router = w_router.at[:, n_experts:n_experts + N_GROUPS].set(w_router_group[l])
        w_router_hi = w_router.astype(BF16)
        w_router_lo = (w_router - w_router_hi.astype(F32)).astype(BF16)
        w_router = jnp.concatenate([w_router_hi, w_router_lo], axis=1)
        b_router = jnp.zeros((1, LANES), F32)
        b_router = b_router.at[0, :n_experts].set(b_router_expert[l])
        b_router = b_router.at[0, n_experts:n_experts + N_GROUPS].set(b_router_group[l])
        x_mid, h_packed, route, route_t, counts = _outproj_call(
            src[0], src[1], m, w_out_b, mod, norm_ffn_w[l], w_router, b_router, layer=l, rows_per_mod=seq,
            tm=tm_wide, n_experts=n_experts)
        first_tile, tile_count, dst_rows, row_stride, n_out_rows = _slot_plan(route_t, counts, n_experts, tm)
        yk = _ffn_call(first_tile, tile_count, dst_rows, h_packed, w_exp_gate, w_exp_up, w_exp_down, layer=l, tm=tm,
                       row_stride=row_stride, n_out_rows=n_out_rows)
        x_all = _combine_call(x_mid, yk, route, mod, final_norm_w, rows_per_mod=seq, tm=tm,
                              row_stride=row_stride, final_norm=last)
    return x_all.reshape(batch, seq, d)
```

```python
import functools

import numpy as np
import jax
import jax.numpy as jnp
from jax import lax
from jax.experimental import pallas as pl
from jax.experimental.pallas import tpu as pltpu

GRID_W = 64
SGU_CHUNK = 128
SGU_GROUPS = 8
NA_HEADS = 8
HEAD_DIM = 128
WIN_ROWS = 8
WIN_COLS = 16
ROPE_THETA = 10000.0
N_GROUPS = 4
EXPERTS_PER_GROUP = 8
RMS_EPS = 1e-6

LANES = 128
Q_ROWS = 4
Q_BLOCK = Q_ROWS * GRID_W
KEY_ROWS = Q_ROWS + WIN_ROWS - 1
HEADS_PER_STEP = 4
GATHER_DEPTH = 6
MASKED = -1e30
LOG2_E = 1.4426950408889634
QK_SCALE = HEAD_DIM ** -0.5 * LOG2_E
VMEM_LIMIT = 56 * 1024 * 1024

BF16 = jnp.bfloat16
F32 = jnp.float32
U32 = jnp.uint32


def _params(n_grid_dims, vmem=VMEM_LIMIT):
    return pltpu.CompilerParams(dimension_semantics=("arbitrary",) * n_grid_dims, vmem_limit_bytes=vmem)


def _resident(shape):
    nd = len(shape)
    return pl.BlockSpec(shape, lambda *_: (0,) * nd, pipeline_mode=pl.Buffered(1))


def _resident_layer(stacked, layer):
    nd = stacked.ndim
    return pl.BlockSpec((None,) + stacked.shape[1:], lambda *_: (layer,) + (0,) * (nd - 1),
                        pipeline_mode=pl.Buffered(1))


def _pack_bf16_pair(lo, hi):
    lo_bits = lax.bitcast_convert_type(lo.astype(BF16).astype(F32), U32)
    hi_bits = lax.bitcast_convert_type(hi.astype(BF16).astype(F32), U32)
    return (hi_bits & jnp.uint32(0xFFFF0000)) | (lo_bits >> 16)


def _unpack_bf16_pair(w):
    lo = lax.bitcast_convert_type(w << 16, F32)
    hi = lax.bitcast_convert_type(w & jnp.uint32(0xFFFF0000), F32)
    return lo, hi


def _rms_modulate(x, norm_w, shift, scale):
    y = x * lax.rsqrt(jnp.mean(x * x, axis=-1, keepdims=True) + RMS_EPS) * norm_w
    return y * (1.0 + scale) + shift


def _ada_body(ct_ref, w_ref, b_ref, o_ref, *, n_rows):
    s = ct_ref[...]
    s = s * jax.nn.sigmoid(s)
    w = w_ref[0]
    o_ref[...] = jnp.zeros_like(o_ref)
    for r in range(n_rows):
        o_ref[0, r:r + 1, :] = jnp.sum(w * s[:, r:r + 1], axis=0, keepdims=True) + b_ref[0]


def _ada_call(cond, w_ada, b_ada):
    n_rows, d = cond.shape
    depth, _, n = w_ada.shape
    tn = next(t for t in (1024, 512, 256, LANES) if n % t == 0)
    ct = jnp.zeros((d, 8), F32).at[:, :n_rows].set(cond.T)
    return pl.pallas_call(
        functools.partial(_ada_body, n_rows=n_rows),
        grid=(depth, n // tn),
        in_specs=[pl.BlockSpec((d, 8), lambda l, j: (0, 0)),
                  pl.BlockSpec((1, d, tn), lambda l, j: (l, 0, j)),
                  pl.BlockSpec((1, 1, tn), lambda l, j: (l, 0, j))],
        out_specs=pl.BlockSpec((1, 8, tn), lambda l, j: (l, 0, j)),
        out_shape=jax.ShapeDtypeStruct((depth, 8, n), F32),
        compiler_params=_params(2),
    )(ct, w_ada, b_ada.reshape(depth, 1, n))


def _inproj_body(*refs, n_main_tiles, two_src, sgu_w, na_w):
    if two_src:
        xa_ref, xb_ref = refs[:2]
        refs = refs[2:]
    else:
        xa_ref = refs[0]
        refs = refs[1:]
    (mod_ref, nw_ref, w_ref, snw_ref, ws_ref, bst_ref, cos_ref, sa_ref, sb_ref,
     h_ref, a_ref, qp_ref, qr_ref, kr_ref, v_ref) = refs
    if two_src:
        x = jnp.where(pl.program_id(0) < n_main_tiles, xa_ref[...], xb_ref[...])
    else:
        x = xa_ref[...]
    tm = x.shape[0]
    h = _rms_modulate(x, nw_ref[...], mod_ref[0, 0:1, :], mod_ref[0, 1:2, :])
    hb = h.astype(BF16)
    h_ref[...] = hb

    def proj(lo, width):
        return jnp.dot(hb, w_ref[:, lo:lo + width], preferred_element_type=F32)

    u = jax.nn.gelu(proj(0, sgu_w))
    v = jax.nn.gelu(proj(sgu_w, sgu_w))
    vn = v * lax.rsqrt(jnp.mean(v * v, axis=-1, keepdims=True) + RMS_EPS) * snw_ref[...]
    vnb = vn.astype(BF16)
    gch = sgu_w // SGU_GROUPS
    for c in range(tm // SGU_CHUNK):
        rows = slice(c * SGU_CHUNK, (c + 1) * SGU_CHUNK)
        for g in range(SGU_GROUPS):
            cols = slice(g * gch, (g + 1) * gch)
            z = jnp.dot(ws_ref[g], vnb[rows, cols], preferred_element_type=F32) + bst_ref[:, g:g + 1]
            a_ref[rows, cols] = (u[rows, cols] * z).astype(BF16)

    cos, sa, sb = cos_ref[...], sa_ref[...], sb_ref[...]

    def rope_into(p, out_ref):
        for hh in range(NA_HEADS):
            cols = slice(hh * HEAD_DIM, (hh + 1) * HEAD_DIM)
            xh = p[:, cols]
            out_ref[:, cols] = (xh * cos + pltpu.roll(xh, HEAD_DIM - 32, 1) * sa
                                + pltpu.roll(xh, 32, 1) * sb).astype(BF16)

    q = proj(2 * sgu_w, na_w) * QK_SCALE
    qp_ref[...] = q.astype(BF16)
    rope_into(q, qr_ref)
    rope_into(proj(2 * sgu_w + na_w, na_w), kr_ref)
    v_ref[...] = proj(2 * sgu_w + 2 * na_w, na_w).astype(BF16)


def _inproj_call(x_main, x_ctx, mod, norm_w, w_in_b, sgu_norm_w, w_s_b, b_s_t, cos, sa, sb,
                 *, layer, rows_per_mod, tm):
    d = x_main.shape[1]
    n_main = x_main.shape[0]
    two_src = x_ctx is not None
    tt = n_main + (x_ctx.shape[0] if two_src else 0)
    n_main_tiles = n_main // tm
    n_mod = mod.shape[0]
    sgu_w = sgu_norm_w.shape[-1]
    na_w = NA_HEADS * HEAD_DIM
    tiles_per_mod = rows_per_mod // tm

    def row_map(i):
        return (i, 0)

    x_specs = [pl.BlockSpec((tm, d), lambda i: (jnp.minimum(i, n_main_tiles - 1), 0))]
    x_args = [x_main]
    if two_src:
        x_specs.append(pl.BlockSpec((tm, d), lambda i: (jnp.maximum(i - n_main_tiles, 0), 0),
                                    pipeline_mode=pl.Buffered(1)))
        x_args.append(x_ctx)
    in_specs = x_specs + [
        pl.BlockSpec((1, 6, d), lambda i: (jnp.minimum(i // tiles_per_mod, n_mod - 1), 0, 0)),
        _resident((1, d)),
        _resident_layer(w_in_b, layer),
        _resident((1, sgu_w)),
        _resident(w_s_b.shape),
        _resident(b_s_t.shape),
        pl.BlockSpec((tm, HEAD_DIM), row_map),
        pl.BlockSpec((tm, HEAD_DIM), row_map),
        pl.BlockSpec((tm, HEAD_DIM), row_map),
    ]
    out_widths = [d, sgu_w, na_w, na_w, na_w, na_w]
    return pl.pallas_call(
        functools.partial(_inproj_body, n_main_tiles=n_main_tiles, two_src=two_src, sgu_w=sgu_w, na_w=na_w),
        grid=(tt // tm,),
        in_specs=in_specs,
        out_specs=[pl.BlockSpec((tm, w), row_map) for w in out_widths],
        out_shape=[jax.ShapeDtypeStruct((tt, w), BF16) for w in out_widths],
        compiler_params=_params(1),
    )(*x_args, mod, norm_w.reshape(1, d), w_in_b, sgu_norm_w.reshape(1, sgu_w), w_s_b, b_s_t, cos, sa, sb)


def _softmax_pv(scores, values):
    m = functools.reduce(jnp.maximum, [jnp.max(s, axis=-1, keepdims=True) for s in scores])
    ps = [jnp.exp2(s - m) for s in scores]
    denom = functools.reduce(jnp.add, [jnp.sum(p, axis=-1, keepdims=True) for p in ps])
    acc = functools.reduce(jnp.add, [jnp.dot(p.astype(BF16), v, preferred_element_type=F32)
                                     for p, v in zip(ps, values)])
    return acc / denom


def _qk(q, k):
    return lax.dot_general(q, k, (((1,), (1,)), ((), ())), preferred_element_type=F32)


def _attn_row_structure(blk, grid_rows):
    wr = min(WIN_ROWS, grid_rows)
    k_row0 = int(np.clip(blk * Q_ROWS - WIN_ROWS // 2, 0, grid_rows - KEY_ROWS))
    qr = blk * Q_ROWS + np.arange(Q_ROWS)
    kr = k_row0 + np.arange(KEY_ROWS)
    r0 = np.clip(qr - wr // 2, 0, grid_rows - wr)
    valid = (kr[None, :] >= r0[:, None]) & (kr[None, :] < r0[:, None] + wr)
    ri = np.clip(kr[None, :] - qr[:, None] + (WIN_ROWS - 1), 0, 2 * WIN_ROWS - 2)
    return valid, ri


def _attn_body(qr_ref, qp_ref, k_ref, v_ref, kc_ref, vc_ref, bcol_ref, o_ref, bias_ref, *, n_blocks, grid_rows):
    i = pl.program_id(2)
    n_keys = KEY_ROWS * GRID_W

    heads = [slice(hh * HEAD_DIM, (hh + 1) * HEAD_DIM) for hh in range(HEADS_PER_STEP)]

    def build_bias(blk):
        valid, ri = _attn_row_structure(blk, grid_rows)
        for hh in range(HEADS_PER_STEP):
            for jr in range(Q_ROWS):
                for kl in range(KEY_ROWS):
                    piece = (bcol_ref[hh, int(ri[jr, kl])] if valid[jr, kl]
                             else jnp.full((GRID_W, GRID_W), MASKED, F32))
                    bias_ref[hh, jr * GRID_W:(jr + 1) * GRID_W, kl * GRID_W:(kl + 1) * GRID_W] = piece

    for blk in sorted({0, min(1, n_blocks - 1), n_blocks - 1}):
        pl.when(i == blk)(functools.partial(build_bias, blk))

    @pl.when(i < n_blocks)
    def _():
        k_row0 = jnp.clip(i * Q_ROWS - WIN_ROWS // 2, 0, grid_rows - KEY_ROWS)
        start = pl.multiple_of(k_row0 * GRID_W, GRID_W)
        for hh, cols in enumerate(heads):
            k_loc = k_ref[pl.ds(start, n_keys), cols]
            v_loc = v_ref[pl.ds(start, n_keys), cols]
            s_loc = _qk(qr_ref[:, cols], k_loc) + bias_ref[hh]
            s_ctx = _qk(qp_ref[:, cols], kc_ref[:, cols])
            o_ref[:, cols] = _softmax_pv([s_loc, s_ctx], [v_loc, vc_ref[:, cols]]).astype(BF16)

    @pl.when(i >= n_blocks)
    def _():
        for cols in heads:
            s_ctx = _qk(qp_ref[:, cols], kc_ref[:, cols])
            o_ref[:, cols] = _softmax_pv([s_ctx], [vc_ref[:, cols]]).astype(BF16)


def _attn_bias_by_column(rpb, grid_rows):
    n_blocks = grid_rows // Q_ROWS
    assert min(WIN_ROWS, grid_rows) == WIN_ROWS and grid_rows >= KEY_ROWS and grid_rows % Q_ROWS == 0
    interior = _attn_row_structure(min(1, n_blocks - 1), grid_rows)
    for blk in range(1, n_blocks - 1):
        assert all(np.array_equal(a, b) for a, b in zip(_attn_row_structure(blk, grid_rows), interior))
    n_ci = 2 * WIN_COLS - 1
    qc = np.arange(GRID_W)
    c0 = np.clip(qc - WIN_COLS // 2, 0, GRID_W - WIN_COLS)
    col_valid = (qc[None, :] >= c0[:, None]) & (qc[None, :] < c0[:, None] + WIN_COLS)
    col_sel = (col_valid[:, :, None]
               & ((qc[None, :, None] - qc[:, None, None] + (WIN_COLS - 1)) == np.arange(n_ci))).astype(np.float32)
    by_col = jnp.einsum('hrc,qkc->hrqk', rpb.astype(F32), jnp.asarray(col_sel), precision=lax.Precision.HIGHEST)
    return jnp.where(jnp.asarray(col_valid)[None, None], by_col * LOG2_E, MASKED)


def _attn_call(qr, qp, kr, v, bias_by_col, *, batch, seq, ctx_len, with_ctx_queries):
    assert ctx_len == Q_BLOCK and seq % Q_BLOCK == 0
    grid_rows = seq // GRID_W
    n_blocks = grid_rows // Q_ROWS
    n_steps = n_blocks + (1 if with_ctx_queries else 0)
    ctx_block0 = batch * n_blocks
    n_out = batch * seq + (batch * ctx_len if with_ctx_queries else 0)
    n_keys = KEY_ROWS * GRID_W

    def q_map(b, h, i):
        return (jnp.where(i < n_blocks, b * n_blocks + i, ctx_block0 + b), h)

    width = HEADS_PER_STEP * HEAD_DIM
    return pl.pallas_call(
        functools.partial(_attn_body, n_blocks=n_blocks, grid_rows=grid_rows),
        grid=(batch, NA_HEADS // HEADS_PER_STEP, n_steps),
        in_specs=[pl.BlockSpec((Q_BLOCK, width), q_map),
                  pl.BlockSpec((Q_BLOCK, width), q_map),
                  pl.BlockSpec((seq, width), lambda b, h, i: (b, h)),
                  pl.BlockSpec((seq, width), lambda b, h, i: (b, h)),
                  pl.BlockSpec((ctx_len, width), lambda b, h, i: (ctx_block0 + b, h)),
                  pl.BlockSpec((ctx_len, width), lambda b, h, i: (ctx_block0 + b, h)),
                  pl.BlockSpec((HEADS_PER_STEP,) + bias_by_col.shape[1:], lambda b, h, i: (h, 0, 0, 0))],
        out_specs=pl.BlockSpec((Q_BLOCK, width), q_map),
        out_shape=jax.ShapeDtypeStruct((n_out, NA_HEADS * HEAD_DIM), BF16),
        scratch_shapes=[pltpu.VMEM((HEADS_PER_STEP, Q_BLOCK, n_keys), F32)],
        compiler_params=_params(3),
    )(qr, qp, kr, v, kr, v, bias_by_col)


def _merge_body(h_ref, a_ref, o_ref, wg_ref, bg_ref, wa_ref, wb_ref, m_ref, *, tn):
    d = m_ref.shape[1]
    hb, ab, ob = h_ref[...], a_ref[...], o_ref[...]
    for n0 in range(0, d, tn):
        cols = slice(n0, n0 + tn)
        gcols = slice(d + n0, d + n0 + tn)
        g_a = jax.nn.sigmoid(jnp.dot(hb, wg_ref[:, cols], preferred_element_type=F32) + bg_ref[:, cols])
        g_b = jax.nn.sigmoid(jnp.dot(hb, wg_ref[:, gcols], preferred_element_type=F32) + bg_ref[:, gcols])
        pa = jnp.dot(ab, wa_ref[:, cols], preferred_element_type=F32)
        pb = jnp.dot(ob, wb_ref[:, cols], preferred_element_type=F32)
        m_ref[:, cols] = (g_a * pa + g_b * pb).astype(BF16)


def _merge_call(h, a, o, w_gate_b, b_gate, w_a_b, w_b_b, *, layer, n_rows, tm):
    d = h.shape[1]

    def row_map(i):
        return (i, 0)

    return pl.pallas_call(
        functools.partial(_merge_body, tn=min(d, 512)),
        grid=(n_rows // tm,),
        in_specs=[pl.BlockSpec((tm, d), row_map),
                  pl.BlockSpec((tm, a.shape[1]), row_map),
                  pl.BlockSpec((tm, o.shape[1]), row_map),
                  _resident_layer(w_gate_b, layer), _resident((1, 2 * d)),
                  _resident_layer(w_a_b, layer), _resident_layer(w_b_b, layer)],
        out_specs=pl.BlockSpec((tm, d), row_map),
        out_shape=jax.ShapeDtypeStruct((n_rows, d), BF16),
        compiler_params=_params(1),
    )(h, a, o, w_gate_b, b_gate.reshape(1, 2 * d), w_a_b, w_b_b)


def _outproj_body(*refs, n_main_tiles, two_src, n_experts):
    if two_src:
        xa_ref, xb_ref = refs[:2]
        refs = refs[2:]
    else:
        xa_ref = refs[0]
        refs = refs[1:]
    (m_ref, wo_ref, mod_ref, nw_ref, wr_ref, br_ref,
     xo_ref, hp_ref, route_ref, route_t_ref, cnt_ref, carry_ref) = refs
    i = pl.program_id(0)
    if two_src:
        x = jnp.where(i < n_main_tiles, xa_ref[...], xb_ref[...])
    else:
        x = xa_ref[...]
    tm, d = x.shape

    @pl.when(i == 0)
    def _():
        carry_ref[...] = jnp.zeros_like(carry_ref)

    y = jnp.dot(m_ref[...], wo_ref[...], preferred_element_type=F32)
    x_new = x + mod_ref[0, 2:3, :] * y
    xo_ref[...] = x_new
    h = _rms_modulate(x_new, nw_ref[...], mod_ref[0, 3:4, :], mod_ref[0, 4:5, :])
    hp_ref[...] = _to_token_tiles(_pack_bf16_pair(h[:, :d // 2], h[:, d // 2:]))

    h_hi = h.astype(BF16)
    h_lo = (h - h_hi.astype(F32)).astype(BF16)
    by_hi = jnp.dot(h_hi, wr_ref[...], preferred_element_type=F32)
    logits = (by_hi[:, :LANES] + by_hi[:, LANES:]
              + jnp.dot(h_lo, wr_ref[:, :LANES], preferred_element_type=F32) + br_ref[...])
    lane = lax.broadcasted_iota(jnp.int32, logits.shape, 1)
    lane_f = lane.astype(F32)
    far = jnp.float32(4 * LANES)

    def first_argmax(vals):
        top = jnp.max(vals, axis=-1, keepdims=True)
        return top, jnp.min(jnp.where(vals == top, lane_f, far), axis=-1, keepdims=True)

    g_logits = jnp.where((lane >= n_experts) & (lane < n_experts + N_GROUPS), logits, MASKED)
    g_top, g_lane = first_argmax(g_logits)
    g_prob = 1.0 / jnp.sum(jnp.exp(g_logits - g_top), axis=-1, keepdims=True)
    e_lo = (g_lane - n_experts) * EXPERTS_PER_GROUP
    e_logits = jnp.where((lane_f >= e_lo) & (lane_f < e_lo + EXPERTS_PER_GROUP), logits, MASKED)
    top1, e1 = first_argmax(e_logits)
    top2, e2 = first_argmax(jnp.where(lane_f == e1, MASKED, e_logits))
    t = jnp.exp(top2 - top1)
    w1 = g_prob / (1.0 + t)
    w2 = g_prob * t / (1.0 + t)

    sel1, sel2 = lane_f == e1, lane_f == e2
    onehot = jnp.where(sel1 | sel2, 1.0, 0.0)
    r_i = lax.broadcasted_iota(jnp.int32, (tm, tm), 0)
    c_i = lax.broadcasted_iota(jnp.int32, (tm, tm), 1)
    earlier = jnp.where(c_i < r_i, 1.0, 0.0).astype(BF16)
    before = jnp.dot(earlier, onehot.astype(BF16), preferred_element_type=F32) + carry_ref[...]
    rank1 = jnp.sum(jnp.where(sel1, before, 0.0), axis=-1, keepdims=True)
    rank2 = jnp.sum(jnp.where(sel2, before, 0.0), axis=-1, keepdims=True)
    carry_ref[...] += jnp.sum(onehot, axis=0, keepdims=True)
    cnt_ref[...] = carry_ref[...]

    route = jnp.zeros_like(logits)
    for k, val in enumerate((e1, e2, w1, w2, rank1, rank2)):
        route = jnp.where(lane == k, val, route)
    route_ref[...] = route
    route_t_ref[...] = route.T[:8, :]


def _outproj_call(x_main, x_ctx, m, w_out_b, mod, norm_w, w_router, b_router, *, layer, rows_per_mod, tm,
                  n_experts):
    d = x_main.shape[1]
    two_src = x_ctx is not None
    n_main = x_main.shape[0] if two_src else m.shape[0]
    n_rows = m.shape[0]
    n_main_tiles = n_main // tm
    n_mod = mod.shape[0]
    tiles_per_mod = rows_per_mod // tm

    def row_map(i):
        return (i, 0)

    x_specs = [pl.BlockSpec((tm, d), lambda i: (jnp.minimum(i, n_main_tiles - 1), 0))]
    x_args = [x_main]
    if two_src:
        x_specs.append(pl.BlockSpec((tm, d), lambda i: (jnp.maximum(i - n_main_tiles, 0), 0),
                                    pipeline_mode=pl.Buffered(1)))
        x_args.append(x_ctx)
    in_specs = x_specs + [
        pl.BlockSpec((tm, d), row_map),
        _resident_layer(w_out_b, layer),
        pl.BlockSpec((1, 6, d), lambda i: (jnp.minimum(i // tiles_per_mod, n_mod - 1), 0, 0)),
        _resident((1, d)),
        _resident(w_router.shape),
        _resident((1, LANES)),
    ]
    return pl.pallas_call(
        functools.partial(_outproj_body, n_main_tiles=n_main_tiles, two_src=two_src, n_experts=n_experts),
        grid=(n_rows // tm,),
        in_specs=in_specs,
        out_specs=[pl.BlockSpec((tm, d), row_map),
                   pl.BlockSpec((tm, d // 2 // LANES, LANES), lambda i: (i, 0, 0)),
                   pl.BlockSpec((tm, LANES), row_map),
                   pl.BlockSpec((8, tm), lambda i: (0, i)),
                   pl.BlockSpec((1, LANES), lambda i: (0, 0))],
        out_shape=[jax.ShapeDtypeStruct((n_rows, d), F32),
                   jax.ShapeDtypeStruct((n_rows, d // 2 // LANES, LANES), U32),
                   jax.ShapeDtypeStruct((n_rows, LANES), F32),
                   jax.ShapeDtypeStruct((8, n_rows), F32),
                   jax.ShapeDtypeStruct((1, LANES), F32)],
        scratch_shapes=[pltpu.VMEM((1, LANES), F32)],
        compiler_params=_params(1),
    )(*x_args, m, w_out_b, mod, norm_w.reshape(1, d), w_router, b_router)


def _row_copy(src_ref, src_row, dst_ref, dst_row, sem):
    return pltpu.make_async_copy(src_ref.at[src_row], dst_ref.at[dst_row], sem)


def _to_token_tiles(rows):
    return rows.reshape(rows.shape[0], rows.shape[1] // LANES, LANES)


def _from_token_tiles(tiles):
    return tiles.reshape(tiles.shape[0], tiles.shape[1] * LANES)


def _slot_rows_body(p1_ref, p2_ref, init_ref, dst_ref, stage_ref, sem, *, tokens_per_step, row_stride):
    i = pl.program_id(0)

    @pl.when(i == 0)
    def _():
        load = pltpu.make_async_copy(init_ref, stage_ref, sem)
        load.start()
        load.wait()

    def put(r, carry):
        t = i * tokens_per_step + r
        stage_ref[p1_ref[t]] = t
        stage_ref[p2_ref[t]] = row_stride + t
        return carry

    lax.fori_loop(0, tokens_per_step, put, 0, unroll=8)

    @pl.when(i == pl.num_programs(0) - 1)
    def _():
        store = pltpu.make_async_copy(stage_ref, dst_ref, sem)
        store.start()
        store.wait()


def _slot_rows_call(pos1, pos2, init_rows, *, row_stride, tokens_per_step):
    n_tok = pos1.shape[0]
    n_map = init_rows.shape[0]
    return pl.pallas_call(
        functools.partial(_slot_rows_body, tokens_per_step=tokens_per_step, row_stride=row_stride),
        grid_spec=pltpu.PrefetchScalarGridSpec(
            num_scalar_prefetch=2,
            grid=(n_tok // tokens_per_step,),
            in_specs=[pl.BlockSpec(memory_space=pl.ANY)],
            out_specs=pl.BlockSpec(memory_space=pl.ANY),
            scratch_shapes=[pltpu.SMEM((n_map,), jnp.int32), pltpu.SemaphoreType.DMA(())],
        ),
        out_shape=jax.ShapeDtypeStruct((n_map,), jnp.int32),
        compiler_params=_params(1),
    )(pos1, pos2, init_rows)


def _ffn_body(first_ref, count_ref, src_ref, dst_ref, h_ref, wg_ref, wu_ref, wd_ref, yk_ref,
              wgb_ref, wub_ref, wdb_ref, xbuf, ybuf, sem_g, sem_s, *, tm, spare_rows):
    e = pl.program_id(0)
    n_experts = pl.num_programs(0)
    half = xbuf.shape[2] * LANES

    def gather_row(tile, r):
        buf = tile % GATHER_DEPTH
        return _row_copy(h_ref, src_ref[(tile + 1) * tm + r], xbuf.at[buf], r, sem_g.at[buf])

    def scatter_row(tile, r, buf):
        return _row_copy(ybuf.at[buf], r, yk_ref, dst_ref[(tile + 1) * tm + r], sem_s.at[buf])

    def wait_tile(buf_ref, sem):
        pltpu.make_async_copy(buf_ref, buf_ref, sem).wait()

    @pl.when(e == 0)
    def _():
        ybuf[...] = jnp.zeros_like(ybuf)

        def prime(r, carry):
            _row_copy(ybuf.at[0], r, yk_ref, spare_rows + r, sem_s.at[0]).start()
            for tile in range(GATHER_DEPTH - 1):
                gather_row(tile, r).start()
            return carry

        lax.fori_loop(0, tm, prime, 0)

    wgb_ref[...] = wg_ref[0, 0].astype(BF16)
    wub_ref[...] = wu_ref[0, 0].astype(BF16)
    wdb_ref[...] = wd_ref[0, 0].astype(BF16)

    def run_tile(j, carry):
        slot = j % 2
        other = 1 - slot
        xslot = j % GATHER_DEPTH
        wait_tile(xbuf.at[xslot], sem_g.at[xslot])
        lo, hi = _unpack_bf16_pair(_from_token_tiles(xbuf[xslot]))
        lo, hi = lo.astype(BF16), hi.astype(BF16)
        wait_tile(ybuf.at[slot], sem_s.at[slot])
        for r in range(tm):
            gather_row(j + GATHER_DEPTH - 1, r).start()
            scatter_row(j - 1, r, other).start()

        def up(w_ref):
            return (jnp.dot(lo, w_ref[:half, :], preferred_element_type=F32)
                    + jnp.dot(hi, w_ref[half:, :], preferred_element_type=F32))

        g = up(wgb_ref)
        hid = (g * jax.nn.sigmoid(g) * up(wub_ref)).astype(BF16)
        y = jnp.dot(hid, wdb_ref[...], preferred_element_type=F32)
        ybuf[slot] = _to_token_tiles(_pack_bf16_pair(y[:, :half], y[:, half:]))
        return carry

    first = first_ref[e]
    lax.fori_loop(first, first + count_ref[e], run_tile, 0)

    @pl.when(e == n_experts - 1)
    def _():
        n_used = first + count_ref[e]
        j = n_used - 1
        slot = j % 2
        other = 1 - slot

        def flush(r, carry):
            scatter_row(j, r, slot).start()
            return carry

        wait_tile(ybuf.at[other], sem_s.at[other])
        lax.fori_loop(0, tm, flush, 0)
        wait_tile(ybuf.at[slot], sem_s.at[slot])
        for ahead in range(GATHER_DEPTH - 1):
            xslot = (n_used + ahead) % GATHER_DEPTH
            wait_tile(xbuf.at[xslot], sem_g.at[xslot])


def _ffn_call(first_tile, tile_count, dst_rows, h_packed, w_gate, w_up, w_down, *, layer, tm, row_stride,
              n_out_rows):
    _, sub, _ = h_packed.shape
    _, n_experts, d, de = w_gate.shape
    src_rows = jnp.where(dst_rows < 2 * row_stride, dst_rows % row_stride, 0)

    def w_map(e, ft, tc, sr, dr):
        return (layer, e, 0, 0)

    return pl.pallas_call(
        functools.partial(_ffn_body, tm=tm, spare_rows=2 * row_stride),
        grid_spec=pltpu.PrefetchScalarGridSpec(
            num_scalar_prefetch=4,
            grid=(n_experts,),
            in_specs=[pl.BlockSpec(memory_space=pl.ANY),
                      pl.BlockSpec((1, 1, d, de), w_map),
                      pl.BlockSpec((1, 1, d, de), w_map),
                      pl.BlockSpec((1, 1, de, d), w_map)],
            out_specs=pl.BlockSpec(memory_space=pl.ANY),
            scratch_shapes=[pltpu.VMEM((d, de), BF16), pltpu.VMEM((d, de), BF16), pltpu.VMEM((de, d), BF16),
                            pltpu.VMEM((GATHER_DEPTH, tm, sub, LANES), U32), pltpu.VMEM((2, tm, sub, LANES), U32),
                            pltpu.SemaphoreType.DMA((GATHER_DEPTH,)), pltpu.SemaphoreType.DMA((2,))],
        ),
        out_shape=jax.ShapeDtypeStruct((n_out_rows, sub, LANES), U32),
        compiler_params=_params(1),
    )(first_tile, tile_count, src_rows, dst_rows, h_packed, w_gate, w_up, w_down)


def _combine_body(x_ref, y1_ref, y2_ref, route_ref, mod_ref, fw_ref, o_ref, *, final_norm):
    w1, w2 = route_ref[:, 2:3], route_ref[:, 3:4]
    lo1, hi1 = _unpack_bf16_pair(_from_token_tiles(y1_ref[...]))
    lo2, hi2 = _unpack_bf16_pair(_from_token_tiles(y2_ref[...]))
    y = jnp.concatenate([w1 * lo1 + w2 * lo2, w1 * hi1 + w2 * hi2], axis=-1)
    x_new = x_ref[...] + mod_ref[0, 5:6, :] * y
    if final_norm:
        x_new = x_new * lax.rsqrt(jnp.mean(x_new * x_new, axis=-1, keepdims=True) + RMS_EPS) * fw_ref[...]
    o_ref[...] = x_new


def _combine_call(x, yk, route, mod, final_w, *, rows_per_mod, tm, row_stride, final_norm):
    n_rows, d = x.shape
    sub = yk.shape[1]
    n_mod = mod.shape[0]
    tiles_per_mod = rows_per_mod // tm
    second = row_stride // tm

    def row_map(i):
        return (i, 0)

    return pl.pallas_call(
        functools.partial(_combine_body, final_norm=final_norm),
        grid=(n_rows // tm,),
        in_specs=[pl.BlockSpec((tm, d), row_map),
                  pl.BlockSpec((tm, sub, LANES), lambda i: (i, 0, 0)),
                  pl.BlockSpec((tm, sub, LANES), lambda i: (second + i, 0, 0)),
                  pl.BlockSpec((tm, LANES), row_map),
                  pl.BlockSpec((1, 6, d), lambda i: (jnp.minimum(i // tiles_per_mod, n_mod - 1), 0, 0)),
                  pl.BlockSpec((1, d), lambda i: (0, 0))],
        out_specs=pl.BlockSpec((tm, d), row_map),
        out_shape=jax.ShapeDtypeStruct((n_rows, d), F32),
        compiler_params=_params(1),
    )(x, yk, yk, route, mod, final_w.reshape(1, d))


def _slot_plan(route_t, counts, n_experts, tm):
    n_rows = route_t.shape[1]
    n_tiles = (2 * n_rows) // tm + n_experts
    cnt = counts[0, :n_experts].astype(jnp.int32)
    padded = ((cnt + tm - 1) // tm) * tm
    ends = jnp.cumsum(padded)
    starts = ends - padded
    fields = route_t.astype(jnp.int32)
    picks = jnp.stack([fields[0], fields[1]])[..., None] == jnp.arange(n_experts, dtype=jnp.int32)
    base = jnp.einsum('kte,e->kt', picks.astype(F32), starts.astype(F32),
                      precision=lax.Precision.HIGHEST).astype(jnp.int32)
    pos1 = base[0] + fields[4]
    pos2 = base[1] + fields[5]
    first_tile = (starts // tm).astype(jnp.int32)
    tile_count = (padded // tm).astype(jnp.int32)
    row_stride = n_rows
    spare = 2 * row_stride
    n_map = pl.cdiv((n_tiles + 1 + GATHER_DEPTH) * tm, 1024) * 1024
    init_rows = spare + jnp.arange(n_map, dtype=jnp.int32) % tm
    dst_rows = _slot_rows_call(pos1 + tm, pos2 + tm, init_rows, row_stride=row_stride, tokens_per_step=2 * tm)
    n_out_rows = spare + tm
    return first_tile, tile_count, dst_rows, row_stride, n_out_rows


def _rope_tables(batch, seq, ctx_rows):
    pairs = HEAD_DIM // 4
    t = np.arange(seq)
    pos = np.stack([t // GRID_W, t % GRID_W], axis=-1).astype(np.float32)
    inv_freq = (ROPE_THETA ** (-np.arange(pairs, dtype=np.float32) / pairs)).astype(np.float32)
    ang = pos[:, :, None] * inv_freq
    cos = np.repeat(np.cos(ang)[:, :, None, :], 2, axis=2).reshape(seq, HEAD_DIM)
    sin = np.sin(ang)
    zero = np.zeros_like(sin)
    sa = np.stack([-sin, zero], axis=2).reshape(seq, HEAD_DIM)
    sb = np.stack([zero, sin], axis=2).reshape(seq, HEAD_DIM)

    def full(tab, fill):
        return jnp.asarray(np.concatenate([np.tile(tab, (batch, 1)),
                                           np.full((ctx_rows, HEAD_DIM), fill, np.float32)]), F32)

    return full(cos, 1.0), full(sa, 0.0), full(sb, 0.0)


def kernel(x, c, ctx, c_ctx, w_ada, b_ada, norm_mix_w, norm_ffn_w, w_in, sgu_norm_w, sgu_w_s, sgu_b_s, na_rpb,
           w_merge_gate, b_merge_gate, w_branch_a, w_branch_b, w_out, w_router_group, b_router_group,
           w_router_expert, b_router_expert, w_exp_gate, w_exp_up, w_exp_down, final_norm_w):
    batch, seq, d = x.shape
    ctx_len = ctx.shape[1]
    depth = w_ada.shape[0]
    n_experts = w_exp_gate.shape[1]
    n_lat = batch * seq
    n_ctx = batch * ctx_len
    tm = 256
    tm_wide = 512
    assert seq % tm_wide == 0 and n_ctx % tm_wide == 0 and tm % SGU_CHUNK == 0 and ctx_len % SGU_CHUNK == 0

    mods = _ada_call(jnp.concatenate([c, c_ctx[None]], axis=0), w_ada, b_ada)
    cos, sa, sb = _rope_tables(batch, seq, n_ctx)
    grid_rows = seq // GRID_W

    x_cur = x.reshape(n_lat, d)
    ctx_rows = ctx.reshape(n_ctx, d)
    w_in_b, w_gate_b, w_a_b, w_b_b, w_out_b = (w.astype(BF16) for w in
                                               (w_in, w_merge_gate, w_branch_a, w_branch_b, w_out))
    x_all = None
    for l in range(depth):
        last = l == depth - 1
        mod = mods[l, :batch + 1].reshape(batch + 1, 6, d)
        w_s_b = sgu_w_s[l].astype(BF16)
        b_s_t = sgu_b_s[l].T
        if x_all is None:
            src = (x_cur, ctx_rows)
        else:
            src = (x_all, None)
        h, a, qp, qr, kr, v = _inproj_call(src[0], src[1], mod, norm_mix_w[l], w_in_b, sgu_norm_w[l], w_s_b, b_s_t,
                                           cos, sa, sb, layer=l, rows_per_mod=seq, tm=tm_wide)
        o = _attn_call(qr, qp, kr, v, _attn_bias_by_column(na_rpb[l], grid_rows), batch=batch, seq=seq,
                       ctx_len=ctx_len, with_ctx_queries=not last)
        n_rows = n_lat if last else n_lat + n_ctx
        m = _merge_call(h, a, o, w_gate_b, b_merge_gate[l], w_a_b, w_b_b, layer=l, n_rows=n_rows, tm=tm_wide)
        w_router = jnp.zeros((d, LANES), F32)
        w_router = w_router.at[:, :n_experts].set(w_router_expert[l])
        w_router = w_router.at[:, n_experts:n_experts + N_GROUPS].set(w_router_group[l])
        w_router_hi = w_router.astype(BF16)
        w_router_lo = (w_router - w_router_hi.astype(F32)).astype(BF16)
        w_router = jnp.concatenate([w_router_hi, w_router_lo], axis=1)
        b_router = jnp.zeros((1, LANES), F32)
        b_router = b_router.at[0, :n_experts].set(b_router_expert[l])
        b_router = b_router.at[0, n_experts:n_experts + N_GROUPS].set(b_router_group[l])
        x_mid, h_packed, route, route_t, counts = _outproj_call(
            src[0], src[1], m, w_out_b, mod, norm_ffn_w[l], w_router, b_router, layer=l, rows_per_mod=seq,
            tm=tm_wide, n_experts=n_experts)
        first_tile, tile_count, dst_rows, row_stride, n_out_rows = _slot_plan(route_t, counts, n_experts, tm)
        yk = _ffn_call(first_tile, tile_count, dst_rows, h_packed, w_exp_gate, w_exp_up, w_exp_down, layer=l, tm=tm,
                       row_stride=row_stride, n_out_rows=n_out_rows)
        x_all = _combine_call(x_mid, yk, route, mod, final_norm_w, rows_per_mod=seq, tm=tm_wide,
                              row_stride=row_stride, final_norm=last)
    return x_all.reshape(batch, seq, d)
```

```python
import functools

import numpy as np
import jax
import jax.numpy as jnp
from jax import lax
from jax.experimental import pallas as pl
from jax.experimental.pallas import tpu as pltpu

GRID_W = 64
SGU_CHUNK = 128
SGU_GROUPS = 8
NA_HEADS = 8
HEAD_DIM = 128
WIN_ROWS = 8
WIN_COLS = 16
ROPE_THETA = 10000.0
N_GROUPS = 4
EXPERTS_PER_GROUP = 8
RMS_EPS = 1e-6

LANES = 128
Q_ROWS = 4
Q_BLOCK = Q_ROWS * GRID_W
KEY_ROWS = Q_ROWS + WIN_ROWS - 1
HEADS_PER_STEP = 4
SCATTER_PRIORITY = 1
GATHER_DEPTH = 6
MASKED = -1e30
LOG2_E = 1.4426950408889634
QK_SCALE = HEAD_DIM ** -0.5 * LOG2_E
VMEM_LIMIT = 56 * 1024 * 1024

BF16 = jnp.bfloat16
F32 = jnp.float32
U32 = jnp.uint32


def _params(n_grid_dims, vmem=VMEM_LIMIT):
    return pltpu.CompilerParams(dimension_semantics=("arbitrary",) * n_grid_dims, vmem_limit_bytes=vmem)


def _resident(shape):
    nd = len(shape)
    return pl.BlockSpec(shape, lambda *_: (0,) * nd, pipeline_mode=pl.Buffered(1))


def _resident_layer(stacked, layer):
    nd = stacked.ndim
    return pl.BlockSpec((None,) + stacked.shape[1:], lambda *_: (layer,) + (0,) * (nd - 1),
                        pipeline_mode=pl.Buffered(1))


def _pack_bf16_pair(lo, hi):
    lo_bits = lax.bitcast_convert_type(lo.astype(BF16).astype(F32), U32)
    hi_bits = lax.bitcast_convert_type(hi.astype(BF16).astype(F32), U32)
    return (hi_bits & jnp.uint32(0xFFFF0000)) | (lo_bits >> 16)


def _unpack_bf16_pair(w):
    lo = lax.bitcast_convert_type(w << 16, F32)
    hi = lax.bitcast_convert_type(w & jnp.uint32(0xFFFF0000), F32)
    return lo, hi


def _rms_modulate(x, norm_w, shift, scale):
    y = x * lax.rsqrt(jnp.mean(x * x, axis=-1, keepdims=True) + RMS_EPS) * norm_w
    return y * (1.0 + scale) + shift


def _ada_body(ct_ref, w_ref, b_ref, o_ref, *, n_rows):
    s = ct_ref[...]
    s = s * jax.nn.sigmoid(s)
    w = w_ref[0]
    o_ref[...] = jnp.zeros_like(o_ref)
    for r in range(n_rows):
        o_ref[0, r:r + 1, :] = jnp.sum(w * s[:, r:r + 1], axis=0, keepdims=True) + b_ref[0]


def _ada_call(cond, w_ada, b_ada):
    n_rows, d = cond.shape
    depth, _, n = w_ada.shape
    tn = next(t for t in (1024, 512, 256, LANES) if n % t == 0)
    ct = jnp.zeros((d, 8), F32).at[:, :n_rows].set(cond.T)
    return pl.pallas_call(
        functools.partial(_ada_body, n_rows=n_rows),
        grid=(depth, n // tn),
        in_specs=[pl.BlockSpec((d, 8), lambda l, j: (0, 0)),
                  pl.BlockSpec((1, d, tn), lambda l, j: (l, 0, j)),
                  pl.BlockSpec((1, 1, tn), lambda l, j: (l, 0, j))],
        out_specs=pl.BlockSpec((1, 8, tn), lambda l, j: (l, 0, j)),
        out_shape=jax.ShapeDtypeStruct((depth, 8, n), F32),
        compiler_params=_params(2),
    )(ct, w_ada, b_ada.reshape(depth, 1, n))


def _inproj_body(*refs, n_main_tiles, two_src, sgu_w, na_w):
    if two_src:
        xa_ref, xb_ref = refs[:2]
        refs = refs[2:]
    else:
        xa_ref = refs[0]
        refs = refs[1:]
    (mod_ref, nw_ref, w_ref, snw_ref, ws_ref, bst_ref, cos_ref, sa_ref, sb_ref,
     h_ref, a_ref, qp_ref, qr_ref, kr_ref, v_ref) = refs
    if two_src:
        x = jnp.where(pl.program_id(0) < n_main_tiles, xa_ref[...], xb_ref[...])
    else:
        x = xa_ref[...]
    tm = x.shape[0]
    h = _rms_modulate(x, nw_ref[...], mod_ref[0, 0:1, :], mod_ref[0, 1:2, :])
    hb = h.astype(BF16)
    h_ref[...] = hb

    def proj(lo, width):
        return jnp.dot(hb, w_ref[:, lo:lo + width], preferred_element_type=F32)

    u = jax.nn.gelu(proj(0, sgu_w))
    v = jax.nn.gelu(proj(sgu_w, sgu_w))
    vn = v * lax.rsqrt(jnp.mean(v * v, axis=-1, keepdims=True) + RMS_EPS) * snw_ref[...]
    vnb = vn.astype(BF16)
    gch = sgu_w // SGU_GROUPS
    for c in range(tm // SGU_CHUNK):
        rows = slice(c * SGU_CHUNK, (c + 1) * SGU_CHUNK)
        for g in range(SGU_GROUPS):
            cols = slice(g * gch, (g + 1) * gch)
            z = jnp.dot(ws_ref[g], vnb[rows, cols], preferred_element_type=F32) + bst_ref[:, g:g + 1]
            a_ref[rows, cols] = (u[rows, cols] * z).astype(BF16)

    cos, sa, sb = cos_ref[...], sa_ref[...], sb_ref[...]

    def rope_into(p, out_ref):
        for hh in range(NA_HEADS):
            cols = slice(hh * HEAD_DIM, (hh + 1) * HEAD_DIM)
            xh = p[:, cols]
            out_ref[:, cols] = (xh * cos + pltpu.roll(xh, HEAD_DIM - 32, 1) * sa
                                + pltpu.roll(xh, 32, 1) * sb).astype(BF16)

    q = proj(2 * sgu_w, na_w) * QK_SCALE
    qp_ref[...] = q.astype(BF16)
    rope_into(q, qr_ref)
    rope_into(proj(2 * sgu_w + na_w, na_w), kr_ref)
    v_ref[...] = proj(2 * sgu_w + 2 * na_w, na_w).astype(BF16)


def _inproj_call(x_main, x_ctx, mod, norm_w, w_in_b, sgu_norm_w, w_s_b, b_s_t, cos, sa, sb,
                 *, layer, rows_per_mod, tm):
    d = x_main.shape[1]
    n_main = x_main.shape[0]
    two_src = x_ctx is not None
    tt = n_main + (x_ctx.shape[0] if two_src else 0)
    n_main_tiles = n_main // tm
    n_mod = mod.shape[0]
    sgu_w = sgu_norm_w.shape[-1]
    na_w = NA_HEADS * HEAD_DIM
    tiles_per_mod = rows_per_mod // tm

    def row_map(i):
        return (i, 0)

    x_specs = [pl.BlockSpec((tm, d), lambda i: (jnp.minimum(i, n_main_tiles - 1), 0))]
    x_args = [x_main]
    if two_src:
        x_specs.append(pl.BlockSpec((tm, d), lambda i: (jnp.maximum(i - n_main_tiles, 0), 0),
                                    pipeline_mode=pl.Buffered(1)))
        x_args.append(x_ctx)
    in_specs = x_specs + [
        pl.BlockSpec((1, 6, d), lambda i: (jnp.minimum(i // tiles_per_mod, n_mod - 1), 0, 0)),
        _resident((1, d)),
        _resident_layer(w_in_b, layer),
        _resident((1, sgu_w)),
        _resident(w_s_b.shape),
        _resident(b_s_t.shape),
        pl.BlockSpec((tm, HEAD_DIM), row_map),
        pl.BlockSpec((tm, HEAD_DIM), row_map),
        pl.BlockSpec((tm, HEAD_DIM), row_map),
    ]
    out_widths = [d, sgu_w, na_w, na_w, na_w, na_w]
    return pl.pallas_call(
        functools.partial(_inproj_body, n_main_tiles=n_main_tiles, two_src=two_src, sgu_w=sgu_w, na_w=na_w),
        grid=(tt // tm,),
        in_specs=in_specs,
        out_specs=[pl.BlockSpec((tm, w), row_map) for w in out_widths],
        out_shape=[jax.ShapeDtypeStruct((tt, w), BF16) for w in out_widths],
        compiler_params=_params(1),
    )(*x_args, mod, norm_w.reshape(1, d), w_in_b, sgu_norm_w.reshape(1, sgu_w), w_s_b, b_s_t, cos, sa, sb)


def _softmax_pv(scores, values):
    m = functools.reduce(jnp.maximum, [jnp.max(s, axis=-1, keepdims=True) for s in scores])
    ps = [jnp.exp2(s - m) for s in scores]
    denom = functools.reduce(jnp.add, [jnp.sum(p, axis=-1, keepdims=True) for p in ps])
    acc = functools.reduce(jnp.add, [jnp.dot(p.astype(BF16), v, preferred_element_type=F32)
                                     for p, v in zip(ps, values)])
    return acc / denom


def _qk(q, k):
    return lax.dot_general(q, k, (((1,), (1,)), ((), ())), preferred_element_type=F32)


def _attn_row_structure(blk, grid_rows):
    wr = min(WIN_ROWS, grid_rows)
    k_row0 = int(np.clip(blk * Q_ROWS - WIN_ROWS // 2, 0, grid_rows - KEY_ROWS))
    qr = blk * Q_ROWS + np.arange(Q_ROWS)
    kr = k_row0 + np.arange(KEY_ROWS)
    r0 = np.clip(qr - wr // 2, 0, grid_rows - wr)
    valid = (kr[None, :] >= r0[:, None]) & (kr[None, :] < r0[:, None] + wr)
    ri = np.clip(kr[None, :] - qr[:, None] + (WIN_ROWS - 1), 0, 2 * WIN_ROWS - 2)
    return valid, ri


def _attn_body(qr_ref, qp_ref, k_ref, v_ref, kc_ref, vc_ref, bcol_ref, o_ref, bias_ref, *, n_blocks, grid_rows):
    i = pl.program_id(2)
    n_keys = KEY_ROWS * GRID_W

    heads = [slice(hh * HEAD_DIM, (hh + 1) * HEAD_DIM) for hh in range(HEADS_PER_STEP)]

    def build_bias(blk):
        valid, ri = _attn_row_structure(blk, grid_rows)
        for hh in range(HEADS_PER_STEP):
            for jr in range(Q_ROWS):
                for kl in range(KEY_ROWS):
                    piece = (bcol_ref[hh, int(ri[jr, kl])] if valid[jr, kl]
                             else jnp.full((GRID_W, GRID_W), MASKED, F32))
                    bias_ref[hh, jr * GRID_W:(jr + 1) * GRID_W, kl * GRID_W:(kl + 1) * GRID_W] = piece

    for blk in sorted({0, min(1, n_blocks - 1), n_blocks - 1}):
        pl.when(i == blk)(functools.partial(build_bias, blk))

    @pl.when(i < n_blocks)
    def _():
        k_row0 = jnp.clip(i * Q_ROWS - WIN_ROWS // 2, 0, grid_rows - KEY_ROWS)
        start = pl.multiple_of(k_row0 * GRID_W, GRID_W)
        for hh, cols in enumerate(heads):
            k_loc = k_ref[pl.ds(start, n_keys), cols]
            v_loc = v_ref[pl.ds(start, n_keys), cols]
            s_loc = _qk(qr_ref[:, cols], k_loc) + bias_ref[hh]
            s_ctx = _qk(qp_ref[:, cols], kc_ref[:, cols])
            o_ref[:, cols] = _softmax_pv([s_loc, s_ctx], [v_loc, vc_ref[:, cols]]).astype(BF16)

    @pl.when(i >= n_blocks)
    def _():
        for cols in heads:
            s_ctx = _qk(qp_ref[:, cols], kc_ref[:, cols])
            o_ref[:, cols] = _softmax_pv([s_ctx], [vc_ref[:, cols]]).astype(BF16)


def _attn_bias_by_column(rpb, grid_rows):
    n_blocks = grid_rows // Q_ROWS
    assert min(WIN_ROWS, grid_rows) == WIN_ROWS and grid_rows >= KEY_ROWS and grid_rows % Q_ROWS == 0
    interior = _attn_row_structure(min(1, n_blocks - 1), grid_rows)
    for blk in range(1, n_blocks - 1):
        assert all(np.array_equal(a, b) for a, b in zip(_attn_row_structure(blk, grid_rows), interior))
    n_ci = 2 * WIN_COLS - 1
    qc = np.arange(GRID_W)
    c0 = np.clip(qc - WIN_COLS // 2, 0, GRID_W - WIN_COLS)
    col_valid = (qc[None, :] >= c0[:, None]) & (qc[None, :] < c0[:, None] + WIN_COLS)
    col_sel = (col_valid[:, :, None]
               & ((qc[None, :, None] - qc[:, None, None] + (WIN_COLS - 1)) == np.arange(n_ci))).astype(np.float32)
    by_col = jnp.einsum('hrc,qkc->hrqk', rpb.astype(F32), jnp.asarray(col_sel), precision=lax.Precision.HIGHEST)
    return jnp.where(jnp.asarray(col_valid)[None, None], by_col * LOG2_E, MASKED)


def _attn_call(qr, qp, kr, v, bias_by_col, *, batch, seq, ctx_len, with_ctx_queries):
    assert ctx_len == Q_BLOCK and seq % Q_BLOCK == 0
    grid_rows = seq // GRID_W
    n_blocks = grid_rows // Q_ROWS
    n_steps = n_blocks + (1 if with_ctx_queries else 0)
    ctx_block0 = batch * n_blocks
    n_out = batch * seq + (batch * ctx_len if with_ctx_queries else 0)
    n_keys = KEY_ROWS * GRID_W

    def q_map(b, h, i):
        return (jnp.where(i < n_blocks, b * n_blocks + i, ctx_block0 + b), h)

    width = HEADS_PER_STEP * HEAD_DIM
    return pl.pallas_call(
        functools.partial(_attn_body, n_blocks=n_blocks, grid_rows=grid_rows),
        grid=(batch, NA_HEADS // HEADS_PER_STEP, n_steps),
        in_specs=[pl.BlockSpec((Q_BLOCK, width), q_map),
                  pl.BlockSpec((Q_BLOCK, width), q_map),
                  pl.BlockSpec((seq, width), lambda b, h, i: (b, h)),
                  pl.BlockSpec((seq, width), lambda b, h, i: (b, h)),
                  pl.BlockSpec((ctx_len, width), lambda b, h, i: (ctx_block0 + b, h)),
                  pl.BlockSpec((ctx_len, width), lambda b, h, i: (ctx_block0 + b, h)),
                  pl.BlockSpec((HEADS_PER_STEP,) + bias_by_col.shape[1:], lambda b, h, i: (h, 0, 0, 0))],
        out_specs=pl.BlockSpec((Q_BLOCK, width), q_map),
        out_shape=jax.ShapeDtypeStruct((n_out, NA_HEADS * HEAD_DIM), BF16),
        scratch_shapes=[pltpu.VMEM((HEADS_PER_STEP, Q_BLOCK, n_keys), F32)],
        compiler_params=_params(3),
    )(qr, qp, kr, v, kr, v, bias_by_col)


def _merge_body(h_ref, a_ref, o_ref, wg_ref, bg_ref, wa_ref, wb_ref, m_ref, *, tn):
    d = m_ref.shape[1]
    hb, ab, ob = h_ref[...], a_ref[...], o_ref[...]
    for n0 in range(0, d, tn):
        cols = slice(n0, n0 + tn)
        gcols = slice(d + n0, d + n0 + tn)
        g_a = jax.nn.sigmoid(jnp.dot(hb, wg_ref[:, cols], preferred_element_type=F32) + bg_ref[:, cols])
        g_b = jax.nn.sigmoid(jnp.dot(hb, wg_ref[:, gcols], preferred_element_type=F32) + bg_ref[:, gcols])
        pa = jnp.dot(ab, wa_ref[:, cols], preferred_element_type=F32)
        pb = jnp.dot(ob, wb_ref[:, cols], preferred_element_type=F32)
        m_ref[:, cols] = (g_a * pa + g_b * pb).astype(BF16)


def _merge_call(h, a, o, w_gate_b, b_gate, w_a_b, w_b_b, *, layer, n_rows, tm):
    d = h.shape[1]

    def row_map(i):
        return (i, 0)

    return pl.pallas_call(
        functools.partial(_merge_body, tn=min(d, 512)),
        grid=(n_rows // tm,),
        in_specs=[pl.BlockSpec((tm, d), row_map),
                  pl.BlockSpec((tm, a.shape[1]), row_map),
                  pl.BlockSpec((tm, o.shape[1]), row_map),
                  _resident_layer(w_gate_b, layer), _resident((1, 2 * d)),
                  _resident_layer(w_a_b, layer), _resident_layer(w_b_b, layer)],
        out_specs=pl.BlockSpec((tm, d), row_map),
        out_shape=jax.ShapeDtypeStruct((n_rows, d), BF16),
        compiler_params=_params(1),
    )(h, a, o, w_gate_b, b_gate.reshape(1, 2 * d), w_a_b, w_b_b)


def _outproj_body(*refs, n_main_tiles, two_src, n_experts):
    if two_src:
        xa_ref, xb_ref = refs[:2]
        refs = refs[2:]
    else:
        xa_ref = refs[0]
        refs = refs[1:]
    (m_ref, wo_ref, mod_ref, nw_ref, wr_ref, br_ref,
     xo_ref, hp_ref, route_ref, route_t_ref, cnt_ref, carry_ref) = refs
    i = pl.program_id(0)
    if two_src:
        x = jnp.where(i < n_main_tiles, xa_ref[...], xb_ref[...])
    else:
        x = xa_ref[...]
    tm, d = x.shape

    @pl.when(i == 0)
    def _():
        carry_ref[...] = jnp.zeros_like(carry_ref)

    y = jnp.dot(m_ref[...], wo_ref[...], preferred_element_type=F32)
    x_new = x + mod_ref[0, 2:3, :] * y
    xo_ref[...] = x_new
    h = _rms_modulate(x_new, nw_ref[...], mod_ref[0, 3:4, :], mod_ref[0, 4:5, :])
    hp_ref[...] = _to_token_tiles(_pack_bf16_pair(h[:, :d // 2], h[:, d // 2:]))

    h_hi = h.astype(BF16)
    h_lo = (h - h_hi.astype(F32)).astype(BF16)
    by_hi = jnp.dot(h_hi, wr_ref[...], preferred_element_type=F32)
    logits = (by_hi[:, :LANES] + by_hi[:, LANES:]
              + jnp.dot(h_lo, wr_ref[:, :LANES], preferred_element_type=F32) + br_ref[...])
    lane = lax.broadcasted_iota(jnp.int32, logits.shape, 1)
    lane_f = lane.astype(F32)
    far = jnp.float32(4 * LANES)

    def first_argmax(vals):
        top = jnp.max(vals, axis=-1, keepdims=True)
        return top, jnp.min(jnp.where(vals == top, lane_f, far), axis=-1, keepdims=True)

    g_logits = jnp.where((lane >= n_experts) & (lane < n_experts + N_GROUPS), logits, MASKED)
    g_top, g_lane = first_argmax(g_logits)
    g_prob = 1.0 / jnp.sum(jnp.exp(g_logits - g_top), axis=-1, keepdims=True)
    e_lo = (g_lane - n_experts) * EXPERTS_PER_GROUP
    e_logits = jnp.where((lane_f >= e_lo) & (lane_f < e_lo + EXPERTS_PER_GROUP), logits, MASKED)
    top1, e1 = first_argmax(e_logits)
    top2, e2 = first_argmax(jnp.where(lane_f == e1, MASKED, e_logits))
    t = jnp.exp(top2 - top1)
    w1 = g_prob / (1.0 + t)
    w2 = g_prob * t / (1.0 + t)

    sel1, sel2 = lane_f == e1, lane_f == e2
    onehot = jnp.where(sel1 | sel2, 1.0, 0.0)
    r_i = lax.broadcasted_iota(jnp.int32, (tm, tm), 0)
    c_i = lax.broadcasted_iota(jnp.int32, (tm, tm), 1)
    earlier = jnp.where(c_i < r_i, 1.0, 0.0).astype(BF16)
    before = jnp.dot(earlier, onehot.astype(BF16), preferred_element_type=F32) + carry_ref[...]
    rank1 = jnp.sum(jnp.where(sel1, before, 0.0), axis=-1, keepdims=True)
    rank2 = jnp.sum(jnp.where(sel2, before, 0.0), axis=-1, keepdims=True)
    carry_ref[...] += jnp.sum(onehot, axis=0, keepdims=True)
    cnt_ref[...] = carry_ref[...]

    route = jnp.zeros_like(logits)
    for k, val in enumerate((e1, e2, w1, w2, rank1, rank2)):
        route = jnp.where(lane == k, val, route)
    route_ref[...] = route
    route_t_ref[...] = route.T[:8, :]


def _outproj_call(x_main, x_ctx, m, w_out_b, mod, norm_w, w_router, b_router, *, layer, rows_per_mod, tm,
                  n_experts):
    d = x_main.shape[1]
    two_src = x_ctx is not None
    n_main = x_main.shape[0] if two_src else m.shape[0]
    n_rows = m.shape[0]
    n_main_tiles = n_main // tm
    n_mod = mod.shape[0]
    tiles_per_mod = rows_per_mod // tm

    def row_map(i):
        return (i, 0)

    x_specs = [pl.BlockSpec((tm, d), lambda i: (jnp.minimum(i, n_main_tiles - 1), 0))]
    x_args = [x_main]
    if two_src:
        x_specs.append(pl.BlockSpec((tm, d), lambda i: (jnp.maximum(i - n_main_tiles, 0), 0),
                                    pipeline_mode=pl.Buffered(1)))
        x_args.append(x_ctx)
    in_specs = x_specs + [
        pl.BlockSpec((tm, d), row_map),
        _resident_layer(w_out_b, layer),
        pl.BlockSpec((1, 6, d), lambda i: (jnp.minimum(i // tiles_per_mod, n_mod - 1), 0, 0)),
        _resident((1, d)),
        _resident(w_router.shape),
        _resident((1, LANES)),
    ]
    return pl.pallas_call(
        functools.partial(_outproj_body, n_main_tiles=n_main_tiles, two_src=two_src, n_experts=n_experts),
        grid=(n_rows // tm,),
        in_specs=in_specs,
        out_specs=[pl.BlockSpec((tm, d), row_map),
                   pl.BlockSpec((tm, d // 2 // LANES, LANES), lambda i: (i, 0, 0)),
                   pl.BlockSpec((tm, LANES), row_map),
                   pl.BlockSpec((8, tm), lambda i: (0, i)),
                   pl.BlockSpec((1, LANES), lambda i: (0, 0))],
        out_shape=[jax.ShapeDtypeStruct((n_rows, d), F32),
                   jax.ShapeDtypeStruct((n_rows, d // 2 // LANES, LANES), U32),
                   jax.ShapeDtypeStruct((n_rows, LANES), F32),
                   jax.ShapeDtypeStruct((8, n_rows), F32),
                   jax.ShapeDtypeStruct((1, LANES), F32)],
        scratch_shapes=[pltpu.VMEM((1, LANES), F32)],
        compiler_params=_params(1),
    )(*x_args, m, w_out_b, mod, norm_w.reshape(1, d), w_router, b_router)


def _row_copy(src_ref, src_row, dst_ref, dst_row, sem):
    return pltpu.make_async_copy(src_ref.at[src_row], dst_ref.at[dst_row], sem)


def _to_token_tiles(rows):
    return rows.reshape(rows.shape[0], rows.shape[1] // LANES, LANES)


def _from_token_tiles(tiles):
    return tiles.reshape(tiles.shape[0], tiles.shape[1] * LANES)


def _slot_rows_body(p1_ref, p2_ref, init_ref, dst_ref, stage_ref, sem, *, tokens_per_step, row_stride):
    i = pl.program_id(0)

    @pl.when(i == 0)
    def _():
        load = pltpu.make_async_copy(init_ref, stage_ref, sem)
        load.start()
        load.wait()

    def put(r, carry):
        t = i * tokens_per_step + r
        stage_ref[p1_ref[t]] = t
        stage_ref[p2_ref[t]] = row_stride + t
        return carry

    lax.fori_loop(0, tokens_per_step, put, 0, unroll=8)

    @pl.when(i == pl.num_programs(0) - 1)
    def _():
        store = pltpu.make_async_copy(stage_ref, dst_ref, sem)
        store.start()
        store.wait()


def _slot_rows_call(pos1, pos2, init_rows, *, row_stride, tokens_per_step):
    n_tok = pos1.shape[0]
    n_map = init_rows.shape[0]
    return pl.pallas_call(
        functools.partial(_slot_rows_body, tokens_per_step=tokens_per_step, row_stride=row_stride),
        grid_spec=pltpu.PrefetchScalarGridSpec(
            num_scalar_prefetch=2,
            grid=(n_tok // tokens_per_step,),
            in_specs=[pl.BlockSpec(memory_space=pl.ANY)],
            out_specs=pl.BlockSpec(memory_space=pl.ANY),
            scratch_shapes=[pltpu.SMEM((n_map,), jnp.int32), pltpu.SemaphoreType.DMA(())],
        ),
        out_shape=jax.ShapeDtypeStruct((n_map,), jnp.int32),
        compiler_params=_params(1),
    )(pos1, pos2, init_rows)


def _ffn_body(first_ref, count_ref, src_ref, dst_ref, h_ref, wg_ref, wu_ref, wd_ref, yk_ref,
              wgb_ref, wub_ref, wdb_ref, xbuf, ybuf, sem_g, sem_s, *, tm, spare_rows):
    e = pl.program_id(0)
    n_experts = pl.num_programs(0)
    half = xbuf.shape[2] * LANES

    def gather_row(tile, r):
        buf = tile % GATHER_DEPTH
        return _row_copy(h_ref, src_ref[(tile + 1) * tm + r], xbuf.at[buf], r, sem_g.at[buf])

    def scatter_row(tile, r, buf):
        return _row_copy(ybuf.at[buf], r, yk_ref, dst_ref[(tile + 1) * tm + r], sem_s.at[buf])

    def wait_tile(buf_ref, sem):
        pltpu.make_async_copy(buf_ref, buf_ref, sem).wait()

    @pl.when(e == 0)
    def _():
        ybuf[...] = jnp.zeros_like(ybuf)

        def prime(r, carry):
            _row_copy(ybuf.at[0], r, yk_ref, spare_rows + r, sem_s.at[0]).start()
            for tile in range(GATHER_DEPTH - 1):
                gather_row(tile, r).start()
            return carry

        lax.fori_loop(0, tm, prime, 0)

    wgb_ref[...] = wg_ref[0, 0].astype(BF16)
    wub_ref[...] = wu_ref[0, 0].astype(BF16)
    wdb_ref[...] = wd_ref[0, 0].astype(BF16)

    def run_tile(j, carry):
        slot = j % 2
        other = 1 - slot
        xslot = j % GATHER_DEPTH
        wait_tile(xbuf.at[xslot], sem_g.at[xslot])
        lo, hi = _unpack_bf16_pair(_from_token_tiles(xbuf[xslot]))
        lo, hi = lo.astype(BF16), hi.astype(BF16)
        wait_tile(ybuf.at[slot], sem_s.at[slot])
        for r in range(tm):
            gather_row(j + GATHER_DEPTH - 1, r).start()
            scatter_row(j - 1, r, other).start(priority=SCATTER_PRIORITY)

        def up(w_ref):
            return (jnp.dot(lo, w_ref[:half, :], preferred_element_type=F32)
                    + jnp.dot(hi, w_ref[half:, :], preferred_element_type=F32))

        g = up(wgb_ref)
        hid = (g * jax.nn.sigmoid(g) * up(wub_ref)).astype(BF16)
        y = jnp.dot(hid, wdb_ref[...], preferred_element_type=F32)
        ybuf[slot] = _to_token_tiles(_pack_bf16_pair(y[:, :half], y[:, half:]))
        return carry

    first = first_ref[e]
    lax.fori_loop(first, first + count_ref[e], run_tile, 0)

    @pl.when(e == n_experts - 1)
    def _():
        n_used = first + count_ref[e]
        j = n_used - 1
        slot = j % 2
        other = 1 - slot

        def flush(r, carry):
            scatter_row(j, r, slot).start()
            return carry

        wait_tile(ybuf.at[other], sem_s.at[other])
        lax.fori_loop(0, tm, flush, 0)
        wait_tile(ybuf.at[slot], sem_s.at[slot])
        for ahead in range(GATHER_DEPTH - 1):
            xslot = (n_used + ahead) % GATHER_DEPTH
            wait_tile(xbuf.at[xslot], sem_g.at[xslot])


def _ffn_call(first_tile, tile_count, dst_rows, h_packed, w_gate, w_up, w_down, *, layer, tm, row_stride,
              n_out_rows):
    _, sub, _ = h_packed.shape
    _, n_experts, d, de = w_gate.shape
    src_rows = jnp.where(dst_rows < 2 * row_stride, dst_rows % row_stride, 0)

    def w_map(e, ft, tc, sr, dr):
        return (layer, e, 0, 0)

    return pl.pallas_call(
        functools.partial(_ffn_body, tm=tm, spare_rows=2 * row_stride),
        grid_spec=pltpu.PrefetchScalarGridSpec(
            num_scalar_prefetch=4,
            grid=(n_experts,),
            in_specs=[pl.BlockSpec(memory_space=pl.ANY),
                      pl.BlockSpec((1, 1, d, de), w_map),
                      pl.BlockSpec((1, 1, d, de), w_map),
                      pl.BlockSpec((1, 1, de, d), w_map)],
            out_specs=pl.BlockSpec(memory_space=pl.ANY),
            scratch_shapes=[pltpu.VMEM((d, de), BF16), pltpu.VMEM((d, de), BF16), pltpu.VMEM((de, d), BF16),
                            pltpu.VMEM((GATHER_DEPTH, tm, sub, LANES), U32), pltpu.VMEM((2, tm, sub, LANES), U32),
                            pltpu.SemaphoreType.DMA((GATHER_DEPTH,)), pltpu.SemaphoreType.DMA((2,))],
        ),
        out_shape=jax.ShapeDtypeStruct((n_out_rows, sub, LANES), U32),
        compiler_params=_params(1),
    )(first_tile, tile_count, src_rows, dst_rows, h_packed, w_gate, w_up, w_down)


def _combine_body(x_ref, y1_ref, y2_ref, route_ref, mod_ref, fw_ref, o_ref, *, final_norm):
    w1, w2 = route_ref[:, 2:3], route_ref[:, 3:4]
    lo1, hi1 = _unpack_bf16_pair(_from_token_tiles(y1_ref[...]))
    lo2, hi2 = _unpack_bf16_pair(_from_token_tiles(y2_ref[...]))
    y = jnp.concatenate([w1 * lo1 + w2 * lo2, w1 * hi1 + w2 * hi2], axis=-1)
    x_new = x_ref[...] + mod_ref[0, 5:6, :] * y
    if final_norm:
        x_new = x_new * lax.rsqrt(jnp.mean(x_new * x_new, axis=-1, keepdims=True) + RMS_EPS) * fw_ref[...]
    o_ref[...] = x_new


def _combine_call(x, yk, route, mod, final_w, *, rows_per_mod, tm, row_stride, final_norm):
    n_rows, d = x.shape
    sub = yk.shape[1]
    n_mod = mod.shape[0]
    tiles_per_mod = rows_per_mod // tm
    second = row_stride // tm

    def row_map(i):
        return (i, 0)

    return pl.pallas_call(
        functools.partial(_combine_body, final_norm=final_norm),
        grid=(n_rows // tm,),
        in_specs=[pl.BlockSpec((tm, d), row_map),
                  pl.BlockSpec((tm, sub, LANES), lambda i: (i, 0, 0)),
                  pl.BlockSpec((tm, sub, LANES), lambda i: (second + i, 0, 0)),
                  pl.BlockSpec((tm, LANES), row_map),
                  pl.BlockSpec((1, 6, d), lambda i: (jnp.minimum(i // tiles_per_mod, n_mod - 1), 0, 0)),
                  pl.BlockSpec((1, d), lambda i: (0, 0))],
        out_specs=pl.BlockSpec((tm, d), row_map),
        out_shape=jax.ShapeDtypeStruct((n_rows, d), F32),
        compiler_params=_params(1),
    )(x, yk, yk, route, mod, final_w.reshape(1, d))


def _slot_plan(route_t, counts, n_experts, tm):
    n_rows = route_t.shape[1]
    n_tiles = (2 * n_rows) // tm + n_experts
    cnt = counts[0, :n_experts].astype(jnp.int32)
    padded = ((cnt + tm - 1) // tm) * tm
    ends = jnp.cumsum(padded)
    starts = ends - padded
    fields = route_t.astype(jnp.int32)
    picks = jnp.stack([fields[0], fields[1]])[..., None] == jnp.arange(n_experts, dtype=jnp.int32)
    base = jnp.einsum('kte,e->kt', picks.astype(F32), starts.astype(F32),
                      precision=lax.Precision.HIGHEST).astype(jnp.int32)
    pos1 = base[0] + fields[4]
    pos2 = base[1] + fields[5]
    first_tile = (starts // tm).astype(jnp.int32)
    tile_count = (padded // tm).astype(jnp.int32)
    row_stride = n_rows
    spare = 2 * row_stride
    n_map = pl.cdiv((n_tiles + 1 + GATHER_DEPTH) * tm, 1024) * 1024
    init_rows = spare + jnp.arange(n_map, dtype=jnp.int32) % tm
    dst_rows = _slot_rows_call(pos1 + tm, pos2 + tm, init_rows, row_stride=row_stride, tokens_per_step=2 * tm)
    n_out_rows = spare + tm
    return first_tile, tile_count, dst_rows, row_stride, n_out_rows


def _rope_tables(batch, seq, ctx_rows):
    pairs = HEAD_DIM // 4
    t = np.arange(seq)
    pos = np.stack([t // GRID_W, t % GRID_W], axis=-1).astype(np.float32)
    inv_freq = (ROPE_THETA ** (-np.arange(pairs, dtype=np.float32) / pairs)).astype(np.float32)
    ang = pos[:, :, None] * inv_freq
    cos = np.repeat(np.cos(ang)[:, :, None, :], 2, axis=2).reshape(seq, HEAD_DIM)
    sin = np.sin(ang)
    zero = np.zeros_like(sin)
    sa = np.stack([-sin, zero], axis=2).reshape(seq, HEAD_DIM)
    sb = np.stack([zero, sin], axis=2).reshape(seq, HEAD_DIM)

    def full(tab, fill):
        return jnp.asarray(np.concatenate([np.tile(tab, (batch, 1)),
                                           np.full((ctx_rows, HEAD_DIM), fill, np.float32)]), F32)

    return full(cos, 1.0), full(sa, 0.0), full(sb, 0.0)


def kernel(x, c, ctx, c_ctx, w_ada, b_ada, norm_mix_w, norm_ffn_w, w_in, sgu_norm_w, sgu_w_s, sgu_b_s, na_rpb,
           w_merge_gate, b_merge_gate, w_branch_a, w_branch_b, w_out, w_router_group, b_router_group,
           w_router_expert, b_router_expert, w_exp_gate, w_exp_up, w_exp_down, final_norm_w):
    batch, seq, d = x.shape
    ctx_len = ctx.shape[1]
    depth = w_ada.shape[0]
    n_experts = w_exp_gate.shape[1]
    n_lat = batch * seq
    n_ctx = batch * ctx_len
    tm = 256
    tm_wide = 512
    assert seq % tm_wide == 0 and n_ctx % tm_wide == 0 and tm % SGU_CHUNK == 0 and ctx_len % SGU_CHUNK == 0

    mods = _ada_call(jnp.concatenate([c, c_ctx[None]], axis=0), w_ada, b_ada)
    cos, sa, sb = _rope_tables(batch, seq, n_ctx)
    grid_rows = seq // GRID_W

    x_cur = x.reshape(n_lat, d)
    ctx_rows = ctx.reshape(n_ctx, d)
    w_in_b, w_gate_b, w_a_b, w_b_b, w_out_b = (w.astype(BF16) for w in
                                               (w_in, w_merge_gate, w_branch_a, w_branch_b, w_out))
    x_all = None
    for l in range(depth):
        last = l == depth - 1
        mod = mods[l, :batch + 1].reshape(batch + 1, 6, d)
        w_s_b = sgu_w_s[l].astype(BF16)
        b_s_t = sgu_b_s[l].T
        if x_all is None:
            src = (x_cur, ctx_rows)
        else:
            src = (x_all, None)
        h, a, qp, qr, kr, v = _inproj_call(src[0], src[1], mod, norm_mix_w[l], w_in_b, sgu_norm_w[l], w_s_b, b_s_t,
                                           cos, sa, sb, layer=l, rows_per_mod=seq, tm=tm_wide)
        o = _attn_call(qr, qp, kr, v, _attn_bias_by_column(na_rpb[l], grid_rows), batch=batch, seq=seq,
                       ctx_len=ctx_len, with_ctx_queries=not last)
        n_rows = n_lat if last else n_lat + n_ctx
        m = _merge_call(h, a, o, w_gate_b, b_merge_gate[l], w_a_b, w_b_b, layer=l, n_rows=n_rows, tm=tm_wide)
        w_router = jnp.zeros((d, LANES), F32)
        w_router = w_router.at[:, :n_experts].set(w_router_expert[l])
        w_router = w_router.at[:, n_experts:n_experts + N_GROUPS].set(w_router_group[l])
        w_router_hi = w_router.astype(BF16)
        w_router_lo = (w_router - w_router_hi.astype(F32)).astype(BF16)
        w_router = jnp.concatenate([w_router_hi, w_router_lo], axis=1)
        b_router = jnp.zeros((1, LANES), F32)
        b_router = b_router.at[0, :n_experts].set(b_router_expert[l])
        b_router = b_router.at[0, n_experts:n_experts + N_GROUPS].set(b_router_group[l])
        x_mid, h_packed, route, route_t, counts = _outproj_call(
            src[0], src[1], m, w_out_b, mod, norm_ffn_w[l], w_router, b_router, layer=l, rows_per_mod=seq,
            tm=tm_wide, n_experts=n_experts)
        first_tile, tile_count, dst_rows, row_stride, n_out_rows = _slot_plan(route_t, counts, n_experts, tm)
        yk = _ffn_call(first_tile, tile_count, dst_rows, h_packed, w_exp_gate, w_exp_up, w_exp_down, layer=l, tm=tm,
                       row_stride=row_stride, n_out_rows=n_out_rows)
        x_all = _combine_call(x_mid, yk, route, mod, final_norm_w, rows_per_mod=seq, tm=tm_wide,
                              row_stride=row_stride, final_norm=last)
    return x_all.reshape(batch, seq, d)
```
